```python
import math
import jax, jax.numpy as jnp
from jax import lax
import numpy as np

D_MODEL = 2048
BATCH = 8
SEQ = 2048
DEPTH = 2

N_EVEN = (DEPTH + 1) // 2
N_ODD = DEPTH // 2
MIX_HALF = D_MODEL // 2
Q_BLOCK = 128
MAX_POS_OFFSET = 1024
RMS_EPS = 1e-6

MLA_NOPE = 128
MLA_ROPE = 64
MLA_V = 128
MLA_HEADS = MIX_HALF // MLA_V
MLA_Q_RANK = D_MODEL // 4
MLA_KV_RANK = D_MODEL // 4
ROPE_THETA = 10000.0

GLA_HEADS = 4
GLA_DK = MIX_HALF // 2 // GLA_HEADS
GLA_DV = MIX_HALF // GLA_HEADS
GLA_GATE_RANK = 16
GLA_GATE_NORM = 16.0
GLA_CHUNK = 64

S5_WIDTH = MIX_HALF
S5_GROUP = 16
S5_GROUPS = S5_WIDTH // S5_GROUP
S5_STATE = 64
S5_DT_MIN = 0.001
S5_DT_MAX = 0.1

DIFF_DQK = 64
DIFF_DV = 2 * DIFF_DQK
DIFF_HEADS = MIX_HALF // DIFF_DV

FFN_HIDDEN = -(-8 * D_MODEL // (3 * 256)) * 256

EVEN_SIZES = (MLA_Q_RANK, MLA_KV_RANK, MLA_ROPE,
              GLA_HEADS * GLA_DK, GLA_HEADS * GLA_DK, GLA_HEADS * GLA_DV,
              GLA_GATE_RANK, GLA_HEADS * GLA_DV)
EVEN_COLS = (MLA_Q_RANK + MLA_KV_RANK + MLA_ROPE + 2 * GLA_HEADS * GLA_DK
             + 2 * GLA_HEADS * GLA_DV + GLA_GATE_RANK)
ODD_SIZES = (S5_WIDTH, DIFF_HEADS * 2 * DIFF_DQK, DIFF_HEADS * 2 * DIFF_DQK, DIFF_HEADS * DIFF_DV)
ODD_COLS = S5_WIDTH + 4 * DIFF_HEADS * DIFF_DQK + DIFF_HEADS * DIFF_DV

kernel_name = 'hybrid_mla_gla_s5_diffattn'


def rms_norm(x, g):
    xf = x.astype(jnp.float32)
    y = xf * lax.rsqrt(jnp.mean(xf * xf, axis=-1, keepdims=True) + RMS_EPS)
    return y.astype(x.dtype) * g


def split_cols(t, sizes):
    idx, acc = [], 0
    for s in sizes[:-1]:
        acc += s
        idx.append(acc)
    return jnp.split(t, idx, axis=-1)


def apply_rope(t, pos):
    half = t.shape[-1] // 2
    inv_freq = ROPE_THETA ** (-jnp.arange(half, dtype=jnp.float32) / half)
    ang = pos.astype(jnp.float32)[..., None] * inv_freq
    ang = ang.reshape(ang.shape[:2] + (1,) * (t.ndim - 3) + (half,))
    cos, sin = jnp.cos(ang), jnp.sin(ang)
    tf = t.astype(jnp.float32)
    t1, t2 = tf[..., :half], tf[..., half:]
    return jnp.concatenate([t1 * cos - t2 * sin, t2 * cos + t1 * sin], axis=-1).astype(t.dtype)


def to_blocks(t):
    b, s = t.shape[:2]
    return jnp.moveaxis(t.reshape((b, s // Q_BLOCK, Q_BLOCK) + t.shape[2:]), 1, 0)


def from_blocks(t):
    nb, b = t.shape[:2]
    return jnp.moveaxis(t, 0, 1).reshape((b, nb * t.shape[2]) + t.shape[3:])


def alibi_slopes(n_heads):
    return jnp.exp2(-8.0 * jnp.arange(1, n_heads + 1, dtype=jnp.float32) / n_heads)


def mla_attention(q_nope, q_rope, k_nope, k_rope, v):
    s_len = q_nope.shape[1]
    scale = (MLA_NOPE + MLA_ROPE) ** -0.5
    k_idx = jnp.arange(s_len)

    def block(args):
        qn, qr, start = args
        s = (jnp.einsum('bqhd,bkhd->bhqk', qn, k_nope)
             + jnp.einsum('bqhd,bkd->bhqk', qr, k_rope)).astype(jnp.float32) * scale
        mask = k_idx[None, :] <= (start + jnp.arange(Q_BLOCK))[:, None]
        p = jax.nn.softmax(jnp.where(mask, s, -jnp.inf), axis=-1).astype(v.dtype)
        return jnp.einsum('bhqk,bkhd->bqhd', p, v)

    starts = jnp.arange(s_len // Q_BLOCK, dtype=jnp.int32) * Q_BLOCK
    out = lax.map(block, (to_blocks(q_nope), to_blocks(q_rope), starts))
    return from_blocks(out)


def gla_chunked(q, k, v, log_a):
    b, s_len, h, dk = q.shape
    dv = v.shape[-1]
    nc = s_len // GLA_CHUNK

    def chunks(t):
        return t.reshape(b, nc, GLA_CHUNK, h, t.shape[-1]).transpose(1, 0, 3, 2, 4)

    causal = jnp.tril(jnp.ones((GLA_CHUNK, GLA_CHUNK), dtype=bool))[None, None, :, :, None]

    def step(state, inp):
        qc, kc, vc, gc = inp
        cum = jnp.cumsum(gc, axis=2)
        o_inter = jnp.einsum('bhid,bhde->bhie', qc * jnp.exp(cum), state)
        rel = cum[:, :, :, None, :] - cum[:, :, None, :, :]
        decay = jnp.exp(jnp.where(causal, rel, -jnp.inf))
        attn = jnp.einsum('bhid,bhjd,bhijd->bhij', qc, kc, decay)
        o_intra = jnp.einsum('bhij,bhje->bhie', attn, vc)
        last = cum[:, :, -1:, :]
        new_state = (jnp.exp(last[:, :, 0, :])[..., None] * state
                     + jnp.einsum('bhjd,bhje->bhde', kc * jnp.exp(last - cum), vc))
        return new_state, o_inter + o_intra

    state0 = jnp.zeros((b, h, dk, dv), jnp.float32)
    _, out = lax.scan(step, state0, (chunks(q), chunks(k), chunks(v), chunks(log_a)))
    return out.transpose(1, 0, 3, 2, 4).reshape(b, s_len, h, dv)


def _complex_affine_combine(e1, e2):
    a1r, a1i, b1r, b1i = e1
    a2r, a2i, b2r, b2i = e2
    return (a2r * a1r - a2i * a1i, a2r * a1i + a2i * a1r,
            a2r * b1r - a2i * b1i + b2r, a2r * b1i + a2i * b1r + b2i)


def s5_mixer(u, a_re, a_im, log_dt, b_re, b_im, c_re, c_im, d_skip, w_glu, b_glu):
    f32 = jnp.float32
    bsz, s_len, _ = u.shape
    uf = u.astype(f32).reshape(bsz, s_len, S5_GROUPS, S5_GROUP)
    dt = jnp.exp(log_dt.astype(f32))[:, None]
    lr, li = a_re.astype(f32), a_im.astype(f32)
    mag = jnp.exp(lr * dt)
    abar_r, abar_i = mag * jnp.cos(li * dt), mag * jnp.sin(li * dt)
    den = lr * lr + li * li
    zr, zi = abar_r - 1.0, abar_i
    fr = (zr * lr + zi * li) / den
    fi = (zi * lr - zr * li) / den
    br, bi = b_re.astype(f32), b_im.astype(f32)
    bbar_r = fr[..., None] * br - fi[..., None] * bi
    bbar_i = fr[..., None] * bi + fi[..., None] * br
    bu_r = jnp.einsum('gnp,bsgp->bsgn', bbar_r, uf)
    bu_i = jnp.einsum('gnp,bsgp->bsgn', bbar_i, uf)
    a_r = jnp.broadcast_to(abar_r, (1, s_len, S5_GROUPS, S5_STATE))
    a_i = jnp.broadcast_to(abar_i, (1, s_len, S5_GROUPS, S5_STATE))
    _, _, x_r, x_i = lax.associative_scan(_complex_affine_combine, (a_r, a_i, bu_r, bu_i), axis=1)
    y = (jnp.einsum('gpn,bsgn->bsgp', c_re.astype(f32), x_r)
         - jnp.einsum('gpn,bsgn->bsgp', c_im.astype(f32), x_i)
         + d_skip.astype(f32) * uf).reshape(bsz, s_len, S5_WIDTH)
    z = jax.nn.gelu(y)
    out = z * jax.nn.sigmoid(z @ w_glu.astype(f32) + b_glu.astype(f32))
    return out.astype(u.dtype)


def diff_attention(q, k, v, pos, lam, slopes):
    s_len = q.shape[1]
    scale = DIFF_DQK ** -0.5
    k_idx = jnp.arange(s_len)

    def block(args):
        qb, pb, start = args
        s = jnp.einsum('bqhcd,bkhcd->bhcqk', qb, k).astype(jnp.float32) * scale
        dist = jnp.abs(pb[:, :, None] - pos[:, None, :]).astype(jnp.float32)
        s = s - slopes[None, :, None, None, None] * dist[:, None, None]
        mask = k_idx[None, :] <= (start + jnp.arange(Q_BLOCK))[:, None]
        p = jax.nn.softmax(jnp.where(mask, s, -jnp.inf), axis=-1)
        attn = (p[:, :, 0] - lam * p[:, :, 1]).astype(v.dtype)
        return jnp.einsum('bhqk,bkhd->bqhd', attn, v)

    starts = jnp.arange(s_len // Q_BLOCK, dtype=jnp.int32) * Q_BLOCK
    out = lax.map(block, (to_blocks(q), to_blocks(pos), starts))
    return from_blocks(out)


def mla_gla_mixer(hn, pos, w_in, q_norm, w_uq, kv_norm, w_ukv, w_gate_up, b_gate, g_norm, w_out):
    bsz, s_len, _ = hn.shape
    c_q, c_kv, k_rope, g_q, g_k, g_v, g_lr, g_out = split_cols(hn @ w_in, EVEN_SIZES)
    q = (rms_norm(c_q, q_norm) @ w_uq).reshape(bsz, s_len, MLA_HEADS, MLA_NOPE + MLA_ROPE)
    q_nope, q_rope = q[..., :MLA_NOPE], apply_rope(q[..., MLA_NOPE:], pos)
    kv = (rms_norm(c_kv, kv_norm) @ w_ukv).reshape(bsz, s_len, MLA_HEADS, MLA_NOPE + MLA_V)
    k_nope, v = kv[..., :MLA_NOPE], kv[..., MLA_NOPE:]
    o_mla = mla_attention(q_nope, q_rope, k_nope, apply_rope(k_rope, pos), v)
    o_mla = o_mla.reshape(bsz, s_len, MLA_HEADS * MLA_V)
    f32 = jnp.float32
    gq = g_q.reshape(bsz, s_len, GLA_HEADS, GLA_DK).astype(f32) * GLA_DK ** -0.5
    gk = g_k.reshape(bsz, s_len, GLA_HEADS, GLA_DK).astype(f32)
    gv = g_v.reshape(bsz, s_len, GLA_HEADS, GLA_DV).astype(f32)
    log_a = jax.nn.log_sigmoid((g_lr @ w_gate_up + b_gate).astype(f32)) / GLA_GATE_NORM
    log_a = log_a.reshape(bsz, s_len, GLA_HEADS, GLA_DK)
    o = rms_norm(gla_chunked(gq, gk, gv, log_a).astype(hn.dtype), g_norm)
    o_gla = o.reshape(bsz, s_len, GLA_HEADS * GLA_DV) * jax.nn.silu(g_out)
    return jnp.concatenate([o_mla, o_gla], axis=-1) @ w_out


def s5_diff_mixer(hn, pos, slopes, lambda_init, w_in, a_re, a_im, log_dt, b_re, b_im, c_re, c_im,
                  d_skip, w_glu, b_glu, lq1, lk1, lq2, lk2, d_norm, w_out):
    bsz, s_len, _ = hn.shape
    u, dq, dk, dv = split_cols(hn @ w_in, ODD_SIZES)
    o_s5 = s5_mixer(u, a_re, a_im, log_dt, b_re, b_im, c_re, c_im, d_skip, w_glu, b_glu)
    f32 = jnp.float32
    lam = (jnp.exp(jnp.sum(lq1.astype(f32) * lk1.astype(f32)))
           - jnp.exp(jnp.sum(lq2.astype(f32) * lk2.astype(f32))) + lambda_init)
    q = dq.reshape(bsz, s_len, DIFF_HEADS, 2, DIFF_DQK)
    k = dk.reshape(bsz, s_len, DIFF_HEADS, 2, DIFF_DQK)
    v = dv.reshape(bsz, s_len, DIFF_HEADS, DIFF_DV)
    o = rms_norm(diff_attention(q, k, v, pos, lam, slopes), d_norm) * (1.0 - lambda_init)
    o_diff = o.reshape(bsz, s_len, DIFF_HEADS * DIFF_DV)
    return jnp.concatenate([o_s5, o_diff], axis=-1) @ w_out


def swiglu(h, w_gate, w_up, w_down):
    return (jax.nn.silu(h @ w_gate) * (h @ w_up)) @ w_down


def setup_inputs(seed: int = 0) -> dict:
    key = jax.random.key(seed)
    keys = jax.random.split(key, 48)
    counter = [0]
    f32 = jnp.float32

    def nk():
        counter[0] += 1
        return keys[counter[0] - 1]

    def nrm(shape, fan_in):
        return jax.random.normal(nk(), shape, f32) * fan_in ** -0.5

    def gain(shape):
        return 1.0 + 0.02 * jax.random.normal(nk(), shape, f32)

    def small(shape, s):
        return s * jax.random.normal(nk(), shape, f32)

    x = jax.random.normal(nk(), (BATCH, SEQ, D_MODEL), f32)
    positions = (jax.random.randint(nk(), (BATCH, 1), 0, MAX_POS_OFFSET, dtype=jnp.int32)
                 + jnp.arange(SEQ, dtype=jnp.int32)[None, :])
    n_idx = jnp.arange(S5_STATE, dtype=f32)
    return {
        'x': x,
        'positions': positions,
        'norm_mix': gain((DEPTH, D_MODEL)),
        'norm_ffn': gain((DEPTH, D_MODEL)),
        'final_norm': gain((D_MODEL,)),
        'ffn_w_gate': nrm((DEPTH, D_MODEL, FFN_HIDDEN), D_MODEL),
        'ffn_w_up': nrm((DEPTH, D_MODEL, FFN_HIDDEN), D_MODEL),
        'ffn_w_down': nrm((DEPTH, FFN_HIDDEN, D_MODEL), FFN_HIDDEN),
        'ag_w_in': nrm((N_EVEN, D_MODEL, EVEN_COLS), D_MODEL),
        'mla_q_norm': gain((N_EVEN, MLA_Q_RANK)),
        'mla_w_uq': nrm((N_EVEN, MLA_Q_RANK, MLA_HEADS * (MLA_NOPE + MLA_ROPE)), MLA_Q_RANK),
        'mla_kv_norm': gain((N_EVEN, MLA_KV_RANK)),
        'mla_w_ukv': nrm((N_EVEN, MLA_KV_RANK, MLA_HEADS * (MLA_NOPE + MLA_V)), MLA_KV_RANK),
        'gla_w_gate_up': nrm((N_EVEN, GLA_GATE_RANK, GLA_HEADS * GLA_DK), GLA_GATE_RANK),
        'gla_b_gate': small((N_EVEN, GLA_HEADS * GLA_DK), 0.1),
        'gla_norm': gain((N_EVEN, GLA_DV)),
        'ag_w_out': nrm((N_EVEN, D_MODEL, D_MODEL), D_MODEL),
        'cd_w_in': nrm((N_ODD, D_MODEL, ODD_COLS), D_MODEL),
        's5_a_re': -0.5 + small((N_ODD, S5_GROUPS, S5_STATE), 0.01),
        's5_a_im': math.pi * n_idx + small((N_ODD, S5_GROUPS, S5_STATE), 0.01),
        's5_log_dt': jax.random.uniform(nk(), (N_ODD, S5_GROUPS), f32,
                                        math.log(S5_DT_MIN), math.log(S5_DT_MAX)),
        's5_b_re': nrm((N_ODD, S5_GROUPS, S5_STATE, S5_GROUP), 2 * S5_GROUP),
        's5_b_im': nrm((N_ODD, S5_GROUPS, S5_STATE, S5_GROUP), 2 * S5_GROUP),
        's5_c_re': nrm((N_ODD, S5_GROUPS, S5_GROUP, S5_STATE), S5_STATE),
        's5_c_im': nrm((N_ODD, S5_GROUPS, S5_GROUP, S5_STATE), S5_STATE),
        's5_d': jax.random.normal(nk(), (N_ODD, S5_GROUPS, S5_GROUP), f32),
        's5_w_glu': nrm((N_ODD, S5_WIDTH, S5_WIDTH), S5_WIDTH),
        's5_b_glu': small((N_ODD, S5_WIDTH), 0.01),
        'diff_lambda_q1': small((N_ODD, DIFF_DQK), 0.1),
        'diff_lambda_k1': small((N_ODD, DIFF_DQK), 0.1),
        'diff_lambda_q2': small((N_ODD, DIFF_DQK), 0.1),
        'diff_lambda_k2': small((N_ODD, DIFF_DQK), 0.1),
        'diff_norm': gain((N_ODD, DIFF_DV)),
        'cd_w_out': nrm((N_ODD, D_MODEL, D_MODEL), D_MODEL),
    }


def reference(x, positions, norm_mix, norm_ffn, final_norm, ffn_w_gate, ffn_w_up, ffn_w_down,
              ag_w_in, mla_q_norm, mla_w_uq, mla_kv_norm, mla_w_ukv, gla_w_gate_up, gla_b_gate,
              gla_norm, ag_w_out, cd_w_in, s5_a_re, s5_a_im, s5_log_dt, s5_b_re, s5_b_im,
              s5_c_re, s5_c_im, s5_d, s5_w_glu, s5_b_glu, diff_lambda_q1, diff_lambda_k1,
              diff_lambda_q2, diff_lambda_k2, diff_norm, cd_w_out):
    slopes = alibi_slopes(DIFF_HEADS)
    h = x
    for layer in range(DEPTH):
        i = layer // 2
        hn = rms_norm(h, norm_mix[layer])
        if layer % 2 == 0:
            mix = mla_gla_mixer(hn, positions, ag_w_in[i], mla_q_norm[i], mla_w_uq[i],
                                mla_kv_norm[i], mla_w_ukv[i], gla_w_gate_up[i], gla_b_gate[i],
                                gla_norm[i], ag_w_out[i])
        else:
            lambda_init = 0.8 - 0.6 * math.exp(-0.3 * layer)
            mix = s5_diff_mixer(hn, positions, slopes, lambda_init, cd_w_in[i], s5_a_re[i],
                                s5_a_im[i], s5_log_dt[i], s5_b_re[i], s5_b_im[i], s5_c_re[i],
                                s5_c_im[i], s5_d[i], s5_w_glu[i], s5_b_glu[i],
                                diff_lambda_q1[i], diff_lambda_k1[i], diff_lambda_q2[i],
                                diff_lambda_k2[i], diff_norm[i], cd_w_out[i])
        h = h + mix
        h = h + swiglu(rms_norm(h, norm_ffn[layer]), ffn_w_gate[layer], ffn_w_up[layer],
                       ffn_w_down[layer])
    return rms_norm(h, final_norm)
```

```python
import functools
import math

import jax
import jax.numpy as jnp
from jax import lax
from jax.experimental import pallas as pl
from jax.experimental.pallas import tpu as pltpu

F32 = jnp.float32
BF16 = jnp.bfloat16

D_MODEL = 2048
BATCH = 8
SEQ = 2048
DEPTH = 2
TOKENS = BATCH * SEQ
MIX_HALF = D_MODEL // 2
RMS_EPS = 1e-6

MLA_NOPE = 128
MLA_ROPE = 64
MLA_V = 128
MLA_HEADS = 8
MLA_RANK = 512
ROPE_THETA = 10000.0
MLA_QK = 256

GLA_HEADS = 4
GLA_DK = 128
GLA_DV = 256
GLA_GATE_RANK = 16
GLA_GATE_NORM = 16.0
GLA_CHUNK = 128

S5_WIDTH = MIX_HALF
S5_P = 16
S5_GROUPS = 64
S5_N = 64
S5_L = 16
S5_ROW = S5_L * S5_P
S5_CHUNKS = SEQ // S5_L

DIFF_DQK = 64
DIFF_DV = 128
DIFF_HEADS = 8

FFN_HIDDEN = 5632

LANE = 128
NEG_BIG = -1e30

EV_CQ = 0
EV_CKV = 512
EV_GQ = 1024
EV_GK = 1536
EV_GV = 2048
EV_GOUT = 3072
EV_KROPE = 4096
EV_GLR = 4224
EV_COLS = 4608


def _cparams(sem, vmem_mb=48):
    return pltpu.CompilerParams(dimension_semantics=sem,
                                vmem_limit_bytes=vmem_mb * 1024 * 1024)


def _rms(x, g):
    ms = jnp.mean(x * x, axis=-1, keepdims=True)
    return x * lax.rsqrt(ms + RMS_EPS) * g


def _nt_dot(a, b):
    return lax.dot_general(a, b, (((1,), (1,)), ((), ())), preferred_element_type=F32)


def _norm_matmul_kernel(x_ref, g_ref, w_ref, o_ref, xn_ref):
    @pl.when(pl.program_id(1) == 0)
    def _():
        xn_ref[...] = _rms(x_ref[...].astype(F32), g_ref[...]).astype(BF16)

    o_ref[...] = jnp.dot(xn_ref[...], w_ref[...],
                         preferred_element_type=F32).astype(o_ref.dtype)


def norm_matmul(x, g, w, *, tm=512, tn=512, out_dtype=BF16):
    t, k = x.shape
    n = w.shape[1]
    return pl.pallas_call(
        _norm_matmul_kernel,
        grid=(t // tm, n // tn),
        in_specs=[pl.BlockSpec((tm, k), lambda i, j: (i, 0)),
                  pl.BlockSpec((1, k), lambda i, j: (0, 0)),
                  pl.BlockSpec((k, tn), lambda i, j: (0, j))],
        out_specs=pl.BlockSpec((tm, tn), lambda i, j: (i, j)),
        out_shape=jax.ShapeDtypeStruct((t, n), out_dtype),
        scratch_shapes=[pltpu.VMEM((tm, k), BF16)],
        compiler_params=_cparams(("parallel", "arbitrary")),
        name="norm_matmul",
    )(x, g.reshape(1, k), w)


def _rope_half(blk, ct, st):
    return blk * ct + pltpu.roll(blk, 64, 1) * st


def _qproj_kernel(c_ref, g_ref, w_ref, ct_ref, st_ref, o_ref, xn_ref, *, scale):
    @pl.when(pl.program_id(1) == 0)
    def _():
        xn_ref[...] = _rms(c_ref[...].astype(F32), g_ref[...]).astype(BF16)

    a = jnp.dot(xn_ref[...], w_ref[...], preferred_element_type=F32)
    o_ref[:, :LANE] = (a[:, :LANE] * scale).astype(BF16)
    o_ref[:, LANE:] = (_rope_half(a[:, LANE:], ct_ref[...], st_ref[...]) * scale).astype(BF16)


def mla_q_proj(proj, g, w, ct, st, *, tm=512):
    scale = (MLA_NOPE + MLA_ROPE) ** -0.5
    return pl.pallas_call(
        functools.partial(_qproj_kernel, scale=scale),
        grid=(TOKENS // tm, MLA_HEADS),
        in_specs=[pl.BlockSpec((tm, MLA_RANK), lambda i, h: (i, EV_CQ // MLA_RANK)),
                  pl.BlockSpec((1, MLA_RANK), lambda i, h: (0, 0)),
                  pl.BlockSpec((MLA_RANK, MLA_QK), lambda i, h: (0, h)),
                  pl.BlockSpec((tm, LANE), lambda i, h: (i, 0)),
                  pl.BlockSpec((tm, LANE), lambda i, h: (i, 0))],
        out_specs=pl.BlockSpec((tm, MLA_QK), lambda i, h: (i, h)),
        out_shape=jax.ShapeDtypeStruct((TOKENS, MLA_HEADS * MLA_QK), BF16),
        scratch_shapes=[pltpu.VMEM((tm, MLA_RANK), BF16)],
        compiler_params=_cparams(("parallel", "arbitrary")),
        name="mla_q_proj",
    )(proj, g.reshape(1, MLA_RANK), w, ct, st)


def _kvproj_kernel(c_ref, kr_ref, g_ref, w_ref, ct_ref, st_ref, k_ref, v_ref, xn_ref, kr_s):
    @pl.when(pl.program_id(1) == 0)
    def _():
        xn_ref[...] = _rms(c_ref[...].astype(F32), g_ref[...]).astype(BF16)
        kr_s[...] = _rope_half(kr_ref[...].astype(F32), ct_ref[...], st_ref[...]).astype(BF16)

    kv = jnp.dot(xn_ref[...], w_ref[...], preferred_element_type=F32)
    k_ref[:, :LANE] = kv[:, :LANE].astype(BF16)
    k_ref[:, LANE:] = kr_s[...]
    v_ref[...] = kv[:, LANE:].astype(BF16)


def mla_kv_proj(proj, g, w, ct, st, *, tm=512):
    return pl.pallas_call(
        _kvproj_kernel,
        grid=(TOKENS // tm, MLA_HEADS),
        in_specs=[pl.BlockSpec((tm, MLA_RANK), lambda i, h: (i, EV_CKV // MLA_RANK)),
                  pl.BlockSpec((tm, LANE), lambda i, h: (i, EV_KROPE // LANE)),
                  pl.BlockSpec((1, MLA_RANK), lambda i, h: (0, 0)),
                  pl.BlockSpec((MLA_RANK, MLA_NOPE + MLA_V), lambda i, h: (0, h)),
                  pl.BlockSpec((tm, LANE), lambda i, h: (i, 0)),
                  pl.BlockSpec((tm, LANE), lambda i, h: (i, 0))],
        out_specs=[pl.BlockSpec((tm, MLA_QK), lambda i, h: (i, h)),
                   pl.BlockSpec((tm, MLA_V), lambda i, h: (i, h))],
        out_shape=[jax.ShapeDtypeStruct((TOKENS, MLA_HEADS * MLA_QK), BF16),
                   jax.ShapeDtypeStruct((TOKENS, MLA_HEADS * MLA_V), BF16)],
        scratch_shapes=[pltpu.VMEM((tm, MLA_RANK), BF16), pltpu.VMEM((tm, LANE), BF16)],
        compiler_params=_cparams(("parallel", "arbitrary")),
        name="mla_kv_proj",
    )(proj, proj, g.reshape(1, MLA_RANK), w, ct, st)


def _causal_mask(tq, tk):
    row = lax.broadcasted_iota(jnp.int32, (tq, tk), 0)
    col = lax.broadcasted_iota(jnp.int32, (tq, tk), 1)
    return col <= row


def _softmax_step(s, v, m_ref, l_ref, acc_ref):
    m_prev = m_ref[...]
    m_new = jnp.maximum(m_prev, jnp.max(s, axis=1, keepdims=True))
    p = jnp.exp(s - m_new)
    alpha = jnp.exp(m_prev - m_new)
    l_ref[...] = alpha * l_ref[...] + jnp.sum(p, axis=1, keepdims=True)
    acc_ref[...] = alpha * acc_ref[...] + jnp.dot(p.astype(BF16), v,
                                                  preferred_element_type=F32)
    m_ref[...] = m_new


def _mla_attn_kernel(q_ref, k_ref, v_ref, o_ref, m_ref, l_ref, acc_ref, *, tq):
    qi = pl.program_id(2)
    m_ref[...] = jnp.full_like(m_ref, NEG_BIG)
    l_ref[...] = jnp.zeros_like(l_ref)
    acc_ref[...] = jnp.zeros_like(acc_ref)
    q = q_ref[...]

    def body(j, carry):
        off = pl.multiple_of(j * tq, tq)
        s = _nt_dot(q, k_ref[pl.ds(off, tq), :])
        _softmax_step(s, v_ref[pl.ds(off, tq), :], m_ref, l_ref, acc_ref)
        return carry

    lax.fori_loop(0, qi, body, 0)
    off = pl.multiple_of(qi * tq, tq)
    s = _nt_dot(q, k_ref[pl.ds(off, tq), :])
    s = jnp.where(_causal_mask(tq, tq), s, NEG_BIG)
    _softmax_step(s, v_ref[pl.ds(off, tq), :], m_ref, l_ref, acc_ref)
    o_ref[...] = (acc_ref[...] / l_ref[...]).astype(BF16)


def mla_attention(q, k, v, *, tq=512):
    nq = SEQ // tq
    return pl.pallas_call(
        functools.partial(_mla_attn_kernel, tq=tq),
        grid=(BATCH, MLA_HEADS, nq),
        in_specs=[pl.BlockSpec((tq, MLA_QK), lambda b, h, i: (b * nq + i, h)),
                  pl.BlockSpec((SEQ, MLA_QK), lambda b, h, i: (b, h)),
                  pl.BlockSpec((SEQ, MLA_V), lambda b, h, i: (b, h))],
        out_specs=pl.BlockSpec((tq, MLA_V), lambda b, h, i: (b * nq + i, h)),
        out_shape=jax.ShapeDtypeStruct((TOKENS, MLA_HEADS * MLA_V), BF16),
        scratch_shapes=[pltpu.VMEM((tq, 1), F32), pltpu.VMEM((tq, 1), F32),
                        pltpu.VMEM((tq, MLA_V), F32)],
        compiler_params=_cparams(("parallel", "parallel", "arbitrary")),
        name="mla_attention",
    )(q, k, v)


def _split3(x):
    hi = x.astype(BF16)
    r1 = x - hi.astype(F32)
    mid = r1.astype(BF16)
    lo = (r1 - mid.astype(F32)).astype(BF16)
    return hi, mid, lo


def _gla_kernel(q_ref, k_ref, v_ref, go_ref, glr_ref, wg_ref, bg_ref, gn_ref, o_ref, st_ref,
                *, chunk):
    @pl.when(pl.program_id(2) == 0)
    def _():
        st_ref[...] = jnp.zeros_like(st_ref)

    z = jnp.dot(glr_ref[...], wg_ref[0], preferred_element_type=F32) + bg_ref[0]
    log_a = -(jnp.maximum(-z, 0.0) + jnp.log1p(jnp.exp(-jnp.abs(z)))) / GLA_GATE_NORM
    tri = _causal_mask(chunk, chunk).astype(BF16)
    hi, mid, lo = _split3(log_a)
    cum = (jnp.dot(tri, hi, preferred_element_type=F32)
           + jnp.dot(tri, mid, preferred_element_type=F32)
           + jnp.dot(tri, lo, preferred_element_type=F32))
    last = cum[chunk - 1:chunk, :]

    q = q_ref[...].astype(F32) * GLA_DK ** -0.5
    k = k_ref[...].astype(F32)
    v = v_ref[...]
    st = st_ref[...]
    q_in = (q * jnp.exp(cum)).astype(BF16)
    q_hat = (q * jnp.exp(cum - last)).astype(BF16)
    k_hat = (k * jnp.exp(last - cum)).astype(BF16)

    o = _nt_dot(q_in, st.astype(BF16))
    attn = jnp.where(_causal_mask(chunk, chunk), _nt_dot(q_hat, k_hat), 0.0)
    o = o + jnp.dot(attn.astype(BF16), v, preferred_element_type=F32)
    vt = v.astype(F32).T.astype(BF16)
    st_ref[...] = st * jnp.exp(last) + jnp.dot(vt, k_hat, preferred_element_type=F32)

    o = _rms(o, gn_ref[...])
    g = go_ref[...].astype(F32)
    o_ref[...] = (o * (g * jax.nn.sigmoid(g))).astype(BF16)


def gla_mixer(proj, wg, bg, gn, *, chunk=GLA_CHUNK):
    nc = SEQ // chunk
    row = lambda b, h, c: b * nc + c
    return pl.pallas_call(
        functools.partial(_gla_kernel, chunk=chunk),
        grid=(BATCH, GLA_HEADS, nc),
        in_specs=[pl.BlockSpec((chunk, GLA_DK), lambda b, h, c: (row(b, h, c), EV_GQ // GLA_DK + h)),
                  pl.BlockSpec((chunk, GLA_DK), lambda b, h, c: (row(b, h, c), EV_GK // GLA_DK + h)),
                  pl.BlockSpec((chunk, GLA_DV), lambda b, h, c: (row(b, h, c), EV_GV // GLA_DV + h)),
                  pl.BlockSpec((chunk, GLA_DV), lambda b, h, c: (row(b, h, c), EV_GOUT // GLA_DV + h)),
                  pl.BlockSpec((chunk, LANE), lambda b, h, c: (row(b, h, c), EV_GLR // LANE)),
                  pl.BlockSpec((1, LANE, GLA_DK), lambda b, h, c: (h, 0, 0)),
                  pl.BlockSpec((1, 1, GLA_DK), lambda b, h, c: (h, 0, 0)),
                  pl.BlockSpec((1, GLA_DV), lambda b, h, c: (0, 0))],
        out_specs=pl.BlockSpec((chunk, GLA_DV), lambda b, h, c: (row(b, h, c), h)),
        out_shape=jax.ShapeDtypeStruct((TOKENS, GLA_HEADS * GLA_DV), BF16),
        scratch_shapes=[pltpu.VMEM((GLA_DV, GLA_DK), F32)],
        compiler_params=_cparams(("parallel", "parallel", "arbitrary")),
        name="gla_mixer",
    )(proj, proj, proj, proj, proj, wg, bg, gn.reshape(1, GLA_DV))


def _out_proj_kernel(r_ref, a1_ref, a2_ref, w1_ref, w2_ref, o_ref):
    o_ref[...] = (r_ref[...]
                  + jnp.dot(a1_ref[...], w1_ref[...], preferred_element_type=F32)
                  + jnp.dot(a2_ref[...], w2_ref[...], preferred_element_type=F32))


def out_proj_residual(r, a1, a2, w, *, tm=512, tn=512):
    half = a1.shape[1]
    n = w.shape[1]
    return pl.pallas_call(
        _out_proj_kernel,
        grid=(TOKENS // tm, n // tn),
        in_specs=[pl.BlockSpec((tm, tn), lambda i, j: (i, j)),
                  pl.BlockSpec((tm, half), lambda i, j: (i, 0)),
                  pl.BlockSpec((tm, half), lambda i, j: (i, 0)),
                  pl.BlockSpec((half, tn), lambda i, j: (0, j)),
                  pl.BlockSpec((half, tn), lambda i, j: (1, j))],
        out_specs=pl.BlockSpec((tm, tn), lambda i, j: (i, j)),
        out_shape=jax.ShapeDtypeStruct((TOKENS, n), F32),
        compiler_params=_cparams(("parallel", "arbitrary")),
        name="out_proj_residual",
    )(r, a1, a2, w, w)


def _ffn_kernel(h_ref, g_ref, wg_ref, wu_ref, wd_ref, fg_ref, o_ref, xn_ref, *, final_norm):
    f = pl.program_id(1)

    @pl.when(f == 0)
    def _():
        x = h_ref[...]
        xn_ref[...] = _rms(x, g_ref[...]).astype(BF16)
        o_ref[...] = x

    xn = xn_ref[...]
    a = jnp.dot(xn, wg_ref[...], preferred_element_type=F32)
    b = jnp.dot(xn, wu_ref[...], preferred_element_type=F32)
    act = (a * jax.nn.sigmoid(a) * b).astype(BF16)
    o_ref[...] += jnp.dot(act, wd_ref[...], preferred_element_type=F32)

    if final_norm:
        @pl.when(f == pl.num_programs(1) - 1)
        def _():
            o_ref[...] = _rms(o_ref[...], fg_ref[...])


def ffn_residual(h, g, wg, wu, wd, fg, *, final_norm, tm=512, tf=512):
    d = h.shape[1]
    hidden = wg.shape[1]
    return pl.pallas_call(
        functools.partial(_ffn_kernel, final_norm=final_norm),
        grid=(TOKENS // tm, hidden // tf),
        in_specs=[pl.BlockSpec((tm, d), lambda i, f: (i, 0)),
                  pl.BlockSpec((1, d), lambda i, f: (0, 0)),
                  pl.BlockSpec((d, tf), lambda i, f: (0, f)),
                  pl.BlockSpec((d, tf), lambda i, f: (0, f)),
                  pl.BlockSpec((tf, d), lambda i, f: (f, 0)),
                  pl.BlockSpec((1, d), lambda i, f: (0, 0))],
        out_specs=pl.BlockSpec((tm, d), lambda i, f: (i, 0)),
        out_shape=jax.ShapeDtypeStruct((TOKENS, d), F32),
        scratch_shapes=[pltpu.VMEM((tm, d), BF16)],
        compiler_params=_cparams(("parallel", "arbitrary")),
        name="ffn_residual",
    )(h, g.reshape(1, d), wg, wu, wd, fg.reshape(1, d))


def _s5_kernel(u_ref, m_ref, wr_ref, wi_ref, sor_ref, soi_ref, al_ref, d_ref, z_ref,
               sr_ref, si_ref, xr_ref, xi_ref):
    u = u_ref[0]
    sr_ref[...] = jnp.dot(u, wr_ref[0], preferred_element_type=F32)
    si_ref[...] = jnp.dot(u, wi_ref[0], preferred_element_type=F32)
    ar = al_ref[0, 0:1, :]
    ai = al_ref[0, 1:2, :]

    def step(c, carry):
        xr, xi = carry
        rows = pl.ds(pl.multiple_of(c * BATCH, BATCH), BATCH)
        xr_ref[rows, :] = xr
        xi_ref[rows, :] = xi
        nr = ar * xr - ai * xi + sr_ref[rows, :]
        ni = ar * xi + ai * xr + si_ref[rows, :]
        return nr, ni

    zero = jnp.zeros((BATCH, S5_N), F32)
    lax.fori_loop(0, S5_CHUNKS, step, (zero, zero))

    y = (jnp.dot(u, m_ref[0], preferred_element_type=F32)
         + jnp.dot(xr_ref[...].astype(BF16), sor_ref[0], preferred_element_type=F32)
         + jnp.dot(xi_ref[...].astype(BF16), soi_ref[0], preferred_element_type=F32)
         + d_ref[0] * u.astype(F32))
    z_ref[0] = jax.nn.gelu(y, approximate=True).astype(BF16)


def s5_scan(u_t, ops):
    m, wr, wi, sor, soi, al, dt = ops
    rows = S5_CHUNKS * BATCH
    g3 = lambda g: (g, 0, 0)
    return pl.pallas_call(
        _s5_kernel,
        grid=(S5_GROUPS,),
        in_specs=[pl.BlockSpec((1, rows, S5_ROW), g3),
                  pl.BlockSpec((1, S5_ROW, S5_ROW), g3),
                  pl.BlockSpec((1, S5_ROW, S5_N), g3),
                  pl.BlockSpec((1, S5_ROW, S5_N), g3),
                  pl.BlockSpec((1, S5_N, S5_ROW), g3),
                  pl.BlockSpec((1, S5_N, S5_ROW), g3),
                  pl.BlockSpec((1, 2, S5_N), g3),
                  pl.BlockSpec((1, 1, S5_ROW), g3)],
        out_specs=pl.BlockSpec((1, rows, S5_ROW), g3),
        out_shape=jax.ShapeDtypeStruct((S5_GROUPS, rows, S5_ROW), BF16),
        scratch_shapes=[pltpu.VMEM((rows, S5_N), F32)] * 4,
        compiler_params=_cparams(("parallel",)),
        name="s5_scan",
    )(u_t, m, wr, wi, sor, soi, al, dt)


def _s5_operators(a_re, a_im, log_dt, b_re, b_im, c_re, c_im, d_skip):
    hp = lax.Precision.HIGHEST
    dt = jnp.exp(log_dt)[:, None]
    lr, li = a_re, a_im
    mag = jnp.exp(lr * dt)
    ar, ai = mag * jnp.cos(li * dt), mag * jnp.sin(li * dt)
    den = lr * lr + li * li
    zr, zi = ar - 1.0, ai
    fr = (zr * lr + zi * li) / den
    fi = (zi * lr - zr * li) / den
    bbr = fr[..., None] * b_re - fi[..., None] * b_im
    bbi = fr[..., None] * b_im + fi[..., None] * b_re
    prs, pis = [jnp.ones_like(ar)], [jnp.zeros_like(ar)]
    for _ in range(S5_L):
        pr, pi = prs[-1], pis[-1]
        prs.append(pr * ar - pi * ai)
        pis.append(pr * ai + pi * ar)
    pw_r, pw_i = jnp.stack(prs), jnp.stack(pis)
    cpr = c_re[None] * pw_r[:, :, None, :] - c_im[None] * pw_i[:, :, None, :]
    cpi = c_re[None] * pw_i[:, :, None, :] + c_im[None] * pw_r[:, :, None, :]
    kern = (jnp.einsum('lgpn,gnq->lgpq', cpr[:S5_L], bbr, precision=hp)
            - jnp.einsum('lgpn,gnq->lgpq', cpi[:S5_L], bbi, precision=hp))
    s_idx = jnp.arange(S5_L)[:, None]
    t_idx = jnp.arange(S5_L)[None, :]
    lag = t_idx - s_idx
    k_st = jnp.where((lag >= 0)[:, :, None, None, None], kern[jnp.clip(lag, 0)], 0.0)
    m = k_st.transpose(2, 0, 4, 1, 3).reshape(S5_GROUPS, S5_ROW, S5_ROW)
    rr, ri = pw_r[S5_L - 1::-1][:S5_L], pw_i[S5_L - 1::-1][:S5_L]
    wr = rr[..., None] * bbr[None] - ri[..., None] * bbi[None]
    wi = rr[..., None] * bbi[None] + ri[..., None] * bbr[None]
    wr = wr.transpose(1, 0, 3, 2).reshape(S5_GROUPS, S5_ROW, S5_N)
    wi = wi.transpose(1, 0, 3, 2).reshape(S5_GROUPS, S5_ROW, S5_N)
    sor = cpr[1:].transpose(1, 3, 0, 2).reshape(S5_GROUPS, S5_N, S5_ROW)
    soi = (-cpi[1:]).transpose(1, 3, 0, 2).reshape(S5_GROUPS, S5_N, S5_ROW)
    al = jnp.stack([pw_r[S5_L], pw_i[S5_L]], axis=1)
    dtile = jnp.tile(d_skip, (1, S5_L)).reshape(S5_GROUPS, 1, S5_ROW)
    return (m.astype(BF16), wr.astype(BF16), wi.astype(BF16), sor.astype(BF16),
            soi.astype(BF16), al, dtile)


def _glu_kernel(z_ref, zc_ref, w_ref, b_ref, o_ref):
    gate = jnp.dot(z_ref[...], w_ref[...], preferred_element_type=F32) + b_ref[...]
    o_ref[...] = (zc_ref[...].astype(F32) * jax.nn.sigmoid(gate)).astype(BF16)


def s5_glu(z, w, b, *, tm=1024, tn=512):
    n = w.shape[1]
    return pl.pallas_call(
        _glu_kernel,
        grid=(TOKENS // tm, n // tn),
        in_specs=[pl.BlockSpec((tm, n), lambda i, j: (i, 0)),
                  pl.BlockSpec((tm, tn), lambda i, j: (i, j)),
                  pl.BlockSpec((n, tn), lambda i, j: (0, j)),
                  pl.BlockSpec((1, tn), lambda i, j: (0, j))],
        out_specs=pl.BlockSpec((tm, tn), lambda i, j: (i, j)),
        out_shape=jax.ShapeDtypeStruct((TOKENS, n), BF16),
        compiler_params=_cparams(("parallel", "arbitrary")),
        name="s5_glu",
    )(z, z, w, b.reshape(1, n))


def _diff_attn_kernel(scal_ref, q_ref, k_ref, v_ref, pq_ref, pk_ref, gn_ref, o_ref,
                      m0, l0, acc0, m1, l1, acc1, *, tq, out_scale):
    h = pl.program_id(1)
    qi = pl.program_id(2)
    slope = scal_ref[h]
    lam = scal_ref[DIFF_HEADS]
    for m_ref, l_ref, acc_ref in ((m0, l0, acc0), (m1, l1, acc1)):
        m_ref[...] = jnp.full_like(m_ref, NEG_BIG)
        l_ref[...] = jnp.zeros_like(l_ref)
        acc_ref[...] = jnp.zeros_like(acc_ref)

    lane = lax.broadcasted_iota(jnp.int32, (tq, 2 * DIFF_DQK), 1)
    q = q_ref[...].astype(F32) * DIFF_DQK ** -0.5
    qa = jnp.where(lane < DIFF_DQK, q, 0.0).astype(BF16)
    qb = jnp.where(lane >= DIFF_DQK, q, 0.0).astype(BF16)
    pq = pq_ref[...]

    def scores(off):
        kb = k_ref[pl.ds(off, tq), :]
        pk = pk_ref[0, :, pl.ds(off, tq)]
        bias = slope * jnp.abs(pq - pk)
        return _nt_dot(qa, kb) - bias, _nt_dot(qb, kb) - bias

    def body(j, carry):
        off = pl.multiple_of(j * tq, tq)
        sa, sb = scores(off)
        vb = v_ref[pl.ds(off, tq), :]
        _softmax_step(sa, vb, m0, l0, acc0)
        _softmax_step(sb, vb, m1, l1, acc1)
        return carry

    lax.fori_loop(0, qi, body, 0)
    off = pl.multiple_of(qi * tq, tq)
    sa, sb = scores(off)
    mask = _causal_mask(tq, tq)
    vb = v_ref[pl.ds(off, tq), :]
    _softmax_step(jnp.where(mask, sa, NEG_BIG), vb, m0, l0, acc0)
    _softmax_step(jnp.where(mask, sb, NEG_BIG), vb, m1, l1, acc1)
    o = acc0[...] / l0[...] - lam * (acc1[...] / l1[...])
    o_ref[...] = (_rms(o, gn_ref[...]) * out_scale).astype(BF16)


def diff_attention(scal, proj, posq, posk, gn, *, out_scale, tq=512):
    nq = SEQ // tq
    hb = lambda base: (lambda b, h, i, s: (b, base // DIFF_DV + h))
    return pl.pallas_call(
        functools.partial(_diff_attn_kernel, tq=tq, out_scale=out_scale),
        grid_spec=pltpu.PrefetchScalarGridSpec(
            num_scalar_prefetch=1,
            grid=(BATCH, DIFF_HEADS, nq),
            in_specs=[pl.BlockSpec((tq, DIFF_DV), lambda b, h, i, s: (b * nq + i, OD_Q // DIFF_DV + h)),
                      pl.BlockSpec((SEQ, DIFF_DV), hb(OD_K)),
                      pl.BlockSpec((SEQ, DIFF_DV), hb(OD_V)),
                      pl.BlockSpec((tq, 1), lambda b, h, i, s: (b * nq + i, 0)),
                      pl.BlockSpec((1, 1, SEQ), lambda b, h, i, s: (b, 0, 0)),
                      pl.BlockSpec((1, DIFF_DV), lambda b, h, i, s: (0, 0))],
            out_specs=pl.BlockSpec((tq, DIFF_DV), lambda b, h, i, s: (b * nq + i, h)),
            scratch_shapes=[pltpu.VMEM((tq, 1), F32), pltpu.VMEM((tq, 1), F32),
                            pltpu.VMEM((tq, DIFF_DV), F32)] * 2),
        out_shape=jax.ShapeDtypeStruct((TOKENS, DIFF_HEADS * DIFF_DV), BF16),
        compiler_params=_cparams(("parallel", "parallel", "arbitrary")),
        name="diff_attention",
    )(scal, proj, proj, proj, posq, posk, gn.reshape(1, DIFF_DV))


OD_U = 0
OD_Q = 1024
OD_K = 2048
OD_V = 3072


def _pack_even_w_in(w):
    cq, ckv, kr, gq, gk, gv, glr, gout = jnp.split(
        w, [512, 1024, 1088, 1600, 2112, 3136, 3152], axis=1)
    half = MLA_ROPE // 2
    kr_dup = jnp.concatenate([kr, kr[:, half:], kr[:, :half]], axis=1)
    glr_pad = jnp.pad(glr, ((0, 0), (0, EV_COLS - EV_GLR - GLA_GATE_RANK)))
    return jnp.concatenate([cq, ckv, gq, gk, gv, gout, kr_dup, glr_pad], axis=1).astype(BF16)


def _pack_w_uq(w):
    w = w.reshape(MLA_RANK, MLA_HEADS, MLA_NOPE + MLA_ROPE)
    half = MLA_ROPE // 2
    t1 = w[..., MLA_NOPE:MLA_NOPE + half]
    t2 = w[..., MLA_NOPE + half:]
    w = jnp.concatenate([w[..., :MLA_NOPE], t1, t2, t2, t1], axis=-1)
    return w.reshape(MLA_RANK, MLA_HEADS * MLA_QK).astype(BF16)


def _rope_tables(positions):
    half = MLA_ROPE // 2
    inv_freq = ROPE_THETA ** (-jnp.arange(half, dtype=F32) / half)
    ang = positions.astype(F32).reshape(TOKENS, 1) * inv_freq
    cos, sin = jnp.cos(ang), jnp.sin(ang)
    zeros = jnp.zeros((TOKENS, 2 * half), F32)
    return (jnp.concatenate([cos, cos, zeros], axis=1),
            jnp.concatenate([-sin, sin, zeros], axis=1))


def _even_mixer(h, g_mix, positions, w_in, q_norm, w_uq, kv_norm, w_ukv, w_gate_up, b_gate,
                g_norm, w_out):
    proj = norm_matmul(h, g_mix, _pack_even_w_in(w_in), tm=1024, tn=768)
    ct, st = _rope_tables(positions)
    q = mla_q_proj(proj, q_norm, _pack_w_uq(w_uq), ct, st)
    k, v = mla_kv_proj(proj, kv_norm, w_ukv.astype(BF16), ct, st)
    o_mla = mla_attention(q, k, v)
    wg = w_gate_up.reshape(GLA_GATE_RANK, GLA_HEADS, GLA_DK).transpose(1, 0, 2)
    wg = jnp.pad(wg, ((0, 0), (0, LANE - GLA_GATE_RANK), (0, 0))).astype(BF16)
    bg = b_gate.reshape(GLA_HEADS, 1, GLA_DK)
    o_gla = gla_mixer(proj, wg, bg, g_norm)
    return out_proj_residual(h, o_mla, o_gla, w_out.astype(BF16))


def _odd_mixer(h, g_mix, positions, layer, w_in, a_re, a_im, log_dt, b_re, b_im, c_re, c_im,
               d_skip, w_glu, b_glu, lq1, lk1, lq2, lk2, d_norm, w_out):
    lambda_init = 0.8 - 0.6 * math.exp(-0.3 * layer)
    proj = norm_matmul(h, g_mix, w_in.astype(BF16), tm=1024, tn=1024)
    u = proj[:, OD_U:OD_U + S5_WIDTH].reshape(BATCH, S5_CHUNKS, S5_L, S5_GROUPS, S5_P)
    u_t = u.transpose(3, 1, 0, 2, 4).reshape(S5_GROUPS, S5_CHUNKS * BATCH, S5_ROW)
    z_t = s5_scan(u_t, _s5_operators(a_re, a_im, log_dt, b_re, b_im, c_re, c_im, d_skip))
    z = z_t.reshape(S5_GROUPS, S5_CHUNKS, BATCH, S5_L, S5_P).transpose(2, 1, 3, 0, 4)
    o_s5 = s5_glu(z.reshape(TOKENS, S5_WIDTH), w_glu.astype(BF16), b_glu)
    lam = (jnp.exp(jnp.sum(lq1 * lk1)) - jnp.exp(jnp.sum(lq2 * lk2)) + lambda_init)
    slopes = jnp.exp2(-8.0 * jnp.arange(1, DIFF_HEADS + 1, dtype=F32) / DIFF_HEADS)
    scal = jnp.concatenate([slopes, lam.reshape(1)]).astype(F32)
    posf = positions.astype(F32)
    o_diff = diff_attention(scal, proj, posf.reshape(TOKENS, 1), posf.reshape(BATCH, 1, SEQ),
                            d_norm, out_scale=1.0 - lambda_init)
    return out_proj_residual(h, o_s5, o_diff, w_out.astype(BF16))


def kernel(x, positions, norm_mix, norm_ffn, final_norm, ffn_w_gate, ffn_w_up, ffn_w_down, ag_w_in, mla_q_norm, mla_w_uq, mla_kv_norm, mla_w_ukv, gla_w_gate_up, gla_b_gate, gla_norm, ag_w_out, cd_w_in, s5_a_re, s5_a_im, s5_log_dt, s5_b_re, s5_b_im, s5_c_re, s5_c_im, s5_d, s5_w_glu, s5_b_glu, diff_lambda_q1, diff_lambda_k1, diff_lambda_q2, diff_lambda_k2, diff_norm, cd_w_out):
    h = x.reshape(TOKENS, D_MODEL)
    for layer in range(DEPTH):
        i = layer // 2
        if layer % 2 == 0:
            h = _even_mixer(h, norm_mix[layer], positions, ag_w_in[i], mla_q_norm[i],
                            mla_w_uq[i], mla_kv_norm[i], mla_w_ukv[i], gla_w_gate_up[i],
                            gla_b_gate[i], gla_norm[i], ag_w_out[i])
        else:
            h = _odd_mixer(h, norm_mix[layer], positions, layer, cd_w_in[i], s5_a_re[i],
                           s5_a_im[i], s5_log_dt[i], s5_b_re[i], s5_b_im[i], s5_c_re[i],
                           s5_c_im[i], s5_d[i], s5_w_glu[i], s5_b_glu[i], diff_lambda_q1[i],
                           diff_lambda_k1[i], diff_lambda_q2[i], diff_lambda_k2[i],
                           diff_norm[i], cd_w_out[i])
        h = ffn_residual(h, norm_ffn[layer], ffn_w_gate[layer].astype(BF16),
                         ffn_w_up[layer].astype(BF16), ffn_w_down[layer].astype(BF16),
                         final_norm, final_norm=(layer == DEPTH - 1))
    return h.reshape(BATCH, SEQ, D_MODEL)
```

```python
import functools
import math

import jax
import jax.numpy as jnp
from jax import lax
from jax.experimental import pallas as pl
from jax.experimental.pallas import tpu as pltpu

F32 = jnp.float32
BF16 = jnp.bfloat16

D_MODEL = 2048
BATCH = 8
SEQ = 2048
DEPTH = 2
TOKENS = BATCH * SEQ
MIX_HALF = D_MODEL // 2
RMS_EPS = 1e-6

MLA_NOPE = 128
MLA_ROPE = 64
MLA_V = 128
MLA_HEADS = 8
MLA_RANK = 512
ROPE_THETA = 10000.0
MLA_QK = 256

GLA_HEADS = 4
GLA_DK = 128
GLA_DV = 256
GLA_GATE_RANK = 16
GLA_GATE_NORM = 16.0
GLA_CHUNK = 128

S5_WIDTH = MIX_HALF
S5_P = 16
S5_GROUPS = 64
S5_N = 64
S5_GB = 8
S5_LANE_BLOCKS = S5_GROUPS // S5_GB
S5_SL = S5_GB * S5_N

DIFF_DQK = 64
DIFF_DV = 128
DIFF_HEADS = 8

FFN_HIDDEN = 5632

LANE = 128
NEG_BIG = -1e30

EV_CQ = 0
EV_CKV = 512
EV_GQ = 1024
EV_GK = 1536
EV_GV = 2048
EV_GOUT = 3072
EV_KROPE = 4096
EV_GLR = 4224
EV_COLS = 4608


def _cparams(sem, vmem_mb=48):
    return pltpu.CompilerParams(dimension_semantics=sem,
                                vmem_limit_bytes=vmem_mb * 1024 * 1024)


def _rms(x, g):
    ms = jnp.mean(x * x, axis=-1, keepdims=True)
    return x * lax.rsqrt(ms + RMS_EPS) * g


def _nt_dot(a, b):
    return lax.dot_general(a, b, (((1,), (1,)), ((), ())), preferred_element_type=F32)


def _norm_matmul_kernel(x_ref, g_ref, w_ref, o_ref, xn_ref):
    @pl.when(pl.program_id(1) == 0)
    def _():
        xn_ref[...] = _rms(x_ref[...].astype(F32), g_ref[...]).astype(BF16)

    o_ref[...] = jnp.dot(xn_ref[...], w_ref[...],
                         preferred_element_type=F32).astype(o_ref.dtype)


def norm_matmul(x, g, w, *, tm=512, tn=512, out_dtype=BF16):
    t, k = x.shape
    n = w.shape[1]
    return pl.pallas_call(
        _norm_matmul_kernel,
        grid=(t // tm, n // tn),
        in_specs=[pl.BlockSpec((tm, k), lambda i, j: (i, 0)),
                  pl.BlockSpec((1, k), lambda i, j: (0, 0)),
                  pl.BlockSpec((k, tn), lambda i, j: (0, j))],
        out_specs=pl.BlockSpec((tm, tn), lambda i, j: (i, j)),
        out_shape=jax.ShapeDtypeStruct((t, n), out_dtype),
        scratch_shapes=[pltpu.VMEM((tm, k), BF16)],
        compiler_params=_cparams(("parallel", "arbitrary")),
        name="norm_matmul",
    )(x, g.reshape(1, k), w)


def _odd_in_proj_kernel(x_ref, g_ref, w_ref, u_ref, qkv_ref, xn_ref):
    j = pl.program_id(1)

    @pl.when(j == 0)
    def _():
        xn_ref[...] = _rms(x_ref[...], g_ref[...]).astype(BF16)

    y = jnp.dot(xn_ref[...], w_ref[...], preferred_element_type=F32).astype(BF16)

    @pl.when(j == 0)
    def _():
        u_ref[...] = y

    @pl.when(j > 0)
    def _():
        qkv_ref[...] = y


def odd_in_proj(x, g, w, *, tm=1024):
    tn = S5_WIDTH
    nt = SEQ // tm
    n_att = w.shape[1] - S5_WIDTH
    u, qkv = pl.pallas_call(
        _odd_in_proj_kernel,
        grid=(TOKENS // tm, w.shape[1] // tn),
        in_specs=[pl.BlockSpec((tm, D_MODEL), lambda i, j: (i, 0)),
                  pl.BlockSpec((1, D_MODEL), lambda i, j: (0, 0)),
                  pl.BlockSpec((D_MODEL, tn), lambda i, j: (0, j))],
        out_specs=[pl.BlockSpec((tm, tn), lambda i, j: (i % nt, i // nt)),
                   pl.BlockSpec((tm, tn), lambda i, j: (i, jnp.maximum(j - 1, 0)))],
        out_shape=[jax.ShapeDtypeStruct((SEQ, BATCH * S5_WIDTH), BF16),
                   jax.ShapeDtypeStruct((TOKENS, n_att), BF16)],
        scratch_shapes=[pltpu.VMEM((tm, D_MODEL), BF16)],
        compiler_params=_cparams(("parallel", "arbitrary")),
        name="odd_in_proj",
    )(x, g.reshape(1, D_MODEL), w)
    return u.reshape(SEQ * BATCH, S5_WIDTH), qkv


def _rope_half(blk, ct, st):
    return blk * ct + pltpu.roll(blk, 64, 1) * st


def _qproj_kernel(c_ref, g_ref, w_ref, ct_ref, st_ref, o_ref, xn_ref, *, scale):
    @pl.when(pl.program_id(1) == 0)
    def _():
        xn_ref[...] = _rms(c_ref[...].astype(F32), g_ref[...]).astype(BF16)

    a = jnp.dot(xn_ref[...], w_ref[...], preferred_element_type=F32)
    o_ref[:, :LANE] = (a[:, :LANE] * scale).astype(BF16)
    o_ref[:, LANE:] = (_rope_half(a[:, LANE:], ct_ref[...], st_ref[...]) * scale).astype(BF16)


def mla_q_proj(proj, g, w, ct, st, *, tm=512):
    scale = (MLA_NOPE + MLA_ROPE) ** -0.5
    return pl.pallas_call(
        functools.partial(_qproj_kernel, scale=scale),
        grid=(TOKENS // tm, MLA_HEADS),
        in_specs=[pl.BlockSpec((tm, MLA_RANK), lambda i, h: (i, EV_CQ // MLA_RANK)),
                  pl.BlockSpec((1, MLA_RANK), lambda i, h: (0, 0)),
                  pl.BlockSpec((MLA_RANK, MLA_QK), lambda i, h: (0, h)),
                  pl.BlockSpec((tm, LANE), lambda i, h: (i, 0)),
                  pl.BlockSpec((tm, LANE), lambda i, h: (i, 0))],
        out_specs=pl.BlockSpec((tm, MLA_QK), lambda i, h: (i, h)),
        out_shape=jax.ShapeDtypeStruct((TOKENS, MLA_HEADS * MLA_QK), BF16),
        scratch_shapes=[pltpu.VMEM((tm, MLA_RANK), BF16)],
        compiler_params=_cparams(("parallel", "arbitrary")),
        name="mla_q_proj",
    )(proj, g.reshape(1, MLA_RANK), w, ct, st)


def _kvproj_kernel(c_ref, kr_ref, g_ref, w_ref, ct_ref, st_ref, k_ref, v_ref, xn_ref, kr_s):
    @pl.when(pl.program_id(1) == 0)
    def _():
        xn_ref[...] = _rms(c_ref[...].astype(F32), g_ref[...]).astype(BF16)
        kr_s[...] = _rope_half(kr_ref[...].astype(F32), ct_ref[...], st_ref[...]).astype(BF16)

    kv = jnp.dot(xn_ref[...], w_ref[...], preferred_element_type=F32)
    k_ref[:, :LANE] = kv[:, :LANE].astype(BF16)
    k_ref[:, LANE:] = kr_s[...]
    v_ref[...] = kv[:, LANE:].astype(BF16)


def mla_kv_proj(proj, g, w, ct, st, *, tm=512):
    return pl.pallas_call(
        _kvproj_kernel,
        grid=(TOKENS // tm, MLA_HEADS),
        in_specs=[pl.BlockSpec((tm, MLA_RANK), lambda i, h: (i, EV_CKV // MLA_RANK)),
                  pl.BlockSpec((tm, LANE), lambda i, h: (i, EV_KROPE // LANE)),
                  pl.BlockSpec((1, MLA_RANK), lambda i, h: (0, 0)),
                  pl.BlockSpec((MLA_RANK, MLA_NOPE + MLA_V), lambda i, h: (0, h)),
                  pl.BlockSpec((tm, LANE), lambda i, h: (i, 0)),
                  pl.BlockSpec((tm, LANE), lambda i, h: (i, 0))],
        out_specs=[pl.BlockSpec((tm, MLA_QK), lambda i, h: (i, h)),
                   pl.BlockSpec((tm, MLA_V), lambda i, h: (i, h))],
        out_shape=[jax.ShapeDtypeStruct((TOKENS, MLA_HEADS * MLA_QK), BF16),
                   jax.ShapeDtypeStruct((TOKENS, MLA_HEADS * MLA_V), BF16)],
        scratch_shapes=[pltpu.VMEM((tm, MLA_RANK), BF16), pltpu.VMEM((tm, LANE), BF16)],
        compiler_params=_cparams(("parallel", "arbitrary")),
        name="mla_kv_proj",
    )(proj, proj, g.reshape(1, MLA_RANK), w, ct, st)


def _causal_mask(tq, tk):
    row = lax.broadcasted_iota(jnp.int32, (tq, tk), 0)
    col = lax.broadcasted_iota(jnp.int32, (tq, tk), 1)
    return col <= row


def _lane_tile(x, width):
    return jnp.concatenate([x] * (width // LANE), axis=1)


def _with_ones(v):
    return jnp.concatenate([v, jnp.ones_like(v)], axis=1)


def _softmax_step(s, v_ext, m_ref, l_ref, acc_ref, rows):
    m_prev = m_ref[rows, :]
    m_new = jnp.maximum(m_prev, jnp.max(s, axis=1, keepdims=True))
    p = jnp.exp(s - _lane_tile(m_new, s.shape[1]))
    alpha = jnp.exp(m_prev - m_new)
    pv = jnp.dot(p.astype(BF16), v_ext, preferred_element_type=F32)
    dv = acc_ref.shape[1]
    acc_ref[rows, :] = alpha * acc_ref[rows, :] + pv[:, :dv]
    l_ref[rows, :] = alpha * l_ref[rows, :] + pv[:, dv:]
    m_ref[rows, :] = m_new


def _mla_attn_kernel(q_ref, k_ref, v_ref, o_ref, m_ref, l_ref, acc_ref, *, tq, chains):
    qi = pl.program_id(2)
    m_ref[...] = jnp.full_like(m_ref, NEG_BIG)
    l_ref[...] = jnp.zeros_like(l_ref)
    acc_ref[...] = jnp.zeros_like(acc_ref)
    rc = tq // chains

    def body(j, carry):
        off = pl.multiple_of(j * tq, tq)
        kb = k_ref[pl.ds(off, tq), :]
        vb = _with_ones(v_ref[pl.ds(off, tq), :])
        for c in range(chains):
            rows = slice(c * rc, (c + 1) * rc)
            _softmax_step(_nt_dot(q_ref[rows, :], kb), vb, m_ref, l_ref, acc_ref, rows)
        return carry

    lax.fori_loop(0, qi, body, 0)
    off = pl.multiple_of(qi * tq, tq)
    for c in range(chains):
        rows = slice(c * rc, (c + 1) * rc)
        width = (c + 1) * rc
        kb = k_ref[pl.ds(off, width), :]
        vb = _with_ones(v_ref[pl.ds(off, width), :])
        s = _nt_dot(q_ref[rows, :], kb)
        row = lax.broadcasted_iota(jnp.int32, s.shape, 0) + c * rc
        col = lax.broadcasted_iota(jnp.int32, s.shape, 1)
        _softmax_step(jnp.where(col <= row, s, NEG_BIG), vb, m_ref, l_ref, acc_ref, rows)
    o_ref[...] = (acc_ref[...] / l_ref[...]).astype(BF16)


def mla_attention(q, k, v, *, tq=512, chains=2):
    nq = SEQ // tq
    return pl.pallas_call(
        functools.partial(_mla_attn_kernel, tq=tq, chains=chains),
        grid=(BATCH, MLA_HEADS, nq),
        in_specs=[pl.BlockSpec((tq, MLA_QK), lambda b, h, i: (b * nq + i, h)),
                  pl.BlockSpec((SEQ, MLA_QK), lambda b, h, i: (b, h)),
                  pl.BlockSpec((SEQ, MLA_V), lambda b, h, i: (b, h))],
        out_specs=pl.BlockSpec((tq, MLA_V), lambda b, h, i: (b * nq + i, h)),
        out_shape=jax.ShapeDtypeStruct((TOKENS, MLA_HEADS * MLA_V), BF16),
        scratch_shapes=[pltpu.VMEM((tq, LANE), F32), pltpu.VMEM((tq, LANE), F32),
                        pltpu.VMEM((tq, MLA_V), F32)],
        compiler_params=_cparams(("parallel", "parallel", "arbitrary")),
        name="mla_attention",
    )(q, k, v)


def _split3(x):
    hi = x.astype(BF16)
    r1 = x - hi.astype(F32)
    mid = r1.astype(BF16)
    lo = (r1 - mid.astype(F32)).astype(BF16)
    return hi, mid, lo


def _gla_kernel(q_ref, k_ref, v_ref, go_ref, glr_ref, wg_ref, bg_ref, gn_ref, o_ref, st_ref,
                *, chunk):
    @pl.when(pl.program_id(2) == 0)
    def _():
        st_ref[...] = jnp.zeros_like(st_ref)

    z = jnp.dot(glr_ref[...], wg_ref[0], preferred_element_type=F32) + bg_ref[0]
    log_a = -(jnp.maximum(-z, 0.0) + jnp.log1p(jnp.exp(-jnp.abs(z)))) / GLA_GATE_NORM
    tri = _causal_mask(chunk, chunk).astype(BF16)
    hi, mid, lo = _split3(log_a)
    cum = (jnp.dot(tri, hi, preferred_element_type=F32)
           + jnp.dot(tri, mid, preferred_element_type=F32)
           + jnp.dot(tri, lo, preferred_element_type=F32))
    last = cum[chunk - 1:chunk, :]

    q = q_ref[...].astype(F32) * GLA_DK ** -0.5
    k = k_ref[...].astype(F32)
    v = v_ref[...]
    st = st_ref[...]
    q_in = (q * jnp.exp(cum)).astype(BF16)
    q_hat = (q * jnp.exp(cum - last)).astype(BF16)
    k_hat = (k * jnp.exp(last - cum)).astype(BF16)

    o = _nt_dot(q_in, st.astype(BF16))
    attn = jnp.where(_causal_mask(chunk, chunk), _nt_dot(q_hat, k_hat), 0.0)
    o = o + jnp.dot(attn.astype(BF16), v, preferred_element_type=F32)
    vt = v.astype(F32).T.astype(BF16)
    st_ref[...] = st * jnp.exp(last) + jnp.dot(vt, k_hat, preferred_element_type=F32)

    o = _rms(o, gn_ref[...])
    g = go_ref[...].astype(F32)
    o_ref[...] = (o * (g * jax.nn.sigmoid(g))).astype(BF16)


def gla_mixer(proj, wg, bg, gn, *, chunk=GLA_CHUNK):
    nc = SEQ // chunk
    row = lambda b, h, c: b * nc + c
    return pl.pallas_call(
        functools.partial(_gla_kernel, chunk=chunk),
        grid=(BATCH, GLA_HEADS, nc),
        in_specs=[pl.BlockSpec((chunk, GLA_DK), lambda b, h, c: (row(b, h, c), EV_GQ // GLA_DK + h)),
                  pl.BlockSpec((chunk, GLA_DK), lambda b, h, c: (row(b, h, c), EV_GK // GLA_DK + h)),
                  pl.BlockSpec((chunk, GLA_DV), lambda b, h, c: (row(b, h, c), EV_GV // GLA_DV + h)),
                  pl.BlockSpec((chunk, GLA_DV), lambda b, h, c: (row(b, h, c), EV_GOUT // GLA_DV + h)),
                  pl.BlockSpec((chunk, LANE), lambda b, h, c: (row(b, h, c), EV_GLR // LANE)),
                  pl.BlockSpec((1, LANE, GLA_DK), lambda b, h, c: (h, 0, 0)),
                  pl.BlockSpec((1, 1, GLA_DK), lambda b, h, c: (h, 0, 0)),
                  pl.BlockSpec((1, GLA_DV), lambda b, h, c: (0, 0))],
        out_specs=pl.BlockSpec((chunk, GLA_DV), lambda b, h, c: (row(b, h, c), h)),
        out_shape=jax.ShapeDtypeStruct((TOKENS, GLA_HEADS * GLA_DV), BF16),
        scratch_shapes=[pltpu.VMEM((GLA_DV, GLA_DK), F32)],
        compiler_params=_cparams(("parallel", "parallel", "arbitrary")),
        name="gla_mixer",
    )(proj, proj, proj, proj, proj, wg, bg, gn.reshape(1, GLA_DV))


def _out_proj_kernel(r_ref, a1_ref, a2_ref, w1_ref, w2_ref, o_ref):
    o_ref[...] = (r_ref[...]
                  + jnp.dot(a1_ref[...], w1_ref[...], preferred_element_type=F32)
                  + jnp.dot(a2_ref[...], w2_ref[...], preferred_element_type=F32))


def out_proj_residual(r, a1, a2, w, *, tm=512, tn=512):
    half = a1.shape[1]
    n = w.shape[1]
    return pl.pallas_call(
        _out_proj_kernel,
        grid=(TOKENS // tm, n // tn),
        in_specs=[pl.BlockSpec((tm, tn), lambda i, j: (i, j)),
                  pl.BlockSpec((tm, half), lambda i, j: (i, 0)),
                  pl.BlockSpec((tm, half), lambda i, j: (i, 0)),
                  pl.BlockSpec((half, tn), lambda i, j: (0, j)),
                  pl.BlockSpec((half, tn), lambda i, j: (1, j))],
        out_specs=pl.BlockSpec((tm, tn), lambda i, j: (i, j)),
        out_shape=jax.ShapeDtypeStruct((TOKENS, n), F32),
        compiler_params=_cparams(("parallel", "arbitrary")),
        name="out_proj_residual",
    )(r, a1, a2, w, w)


def _ffn_kernel(h_ref, g_ref, wg_ref, wu_ref, wd_ref, fg_ref, o_ref, xn_ref, *, final_norm):
    f = pl.program_id(1)

    @pl.when(f == 0)
    def _():
        x = h_ref[...]
        xn_ref[...] = _rms(x, g_ref[...]).astype(BF16)
        o_ref[...] = x

    xn = xn_ref[...]
    a = jnp.dot(xn, wg_ref[0], preferred_element_type=F32)
    b = jnp.dot(xn, wu_ref[0], preferred_element_type=F32)
    act = (a * jax.nn.sigmoid(a) * b).astype(BF16)
    o_ref[...] += jnp.dot(act, wd_ref[0], preferred_element_type=F32)

    if final_norm:
        @pl.when(f == pl.num_programs(1) - 1)
        def _():
            o_ref[...] = _rms(o_ref[...], fg_ref[...])


def ffn_residual(h, g, wg, wu, wd, fg, layer, *, final_norm, tm=512, tf=512):
    d = h.shape[1]
    hidden = wg.shape[2]
    return pl.pallas_call(
        functools.partial(_ffn_kernel, final_norm=final_norm),
        grid=(TOKENS // tm, hidden // tf),
        in_specs=[pl.BlockSpec((tm, d), lambda i, f: (i, 0)),
                  pl.BlockSpec((1, d), lambda i, f: (0, 0)),
                  pl.BlockSpec((1, d, tf), lambda i, f: (layer, 0, f)),
                  pl.BlockSpec((1, d, tf), lambda i, f: (layer, 0, f)),
                  pl.BlockSpec((1, tf, d), lambda i, f: (layer, f, 0)),
                  pl.BlockSpec((1, d), lambda i, f: (0, 0))],
        out_specs=pl.BlockSpec((tm, d), lambda i, f: (i, 0)),
        out_shape=jax.ShapeDtypeStruct((TOKENS, d), F32),
        scratch_shapes=[pltpu.VMEM((tm, d), BF16)],
        compiler_params=_cparams(("parallel", "arbitrary")),
        name="ffn_residual",
    )(h, g.reshape(1, d), wg, wu, wd, fg.reshape(1, d))


def _s5_kernel(u_ref, wb_ref, a_ref, wc_ref, d_ref, z_ref, x_ref, st_ref, *, ts):
    @pl.when(pl.program_id(1) == 0)
    def _():
        st_ref[...] = jnp.zeros_like(st_ref)

    u = u_ref[...]
    x_ref[...] = jnp.dot(u, wb_ref[0], preferred_element_type=F32)
    ar = a_ref[0, 0:1, :]
    ai = a_ref[0, 1:2, :]

    def step(t, carry):
        xr, xi = carry
        rows = pl.ds(pl.multiple_of(t * BATCH, BATCH), BATCH)
        nr = ar * xr - ai * xi + x_ref[rows, :S5_SL]
        ni = ar * xi + ai * xr + x_ref[rows, S5_SL:]
        x_ref[rows, :S5_SL] = nr
        x_ref[rows, S5_SL:] = ni
        return nr, ni

    xr, xi = lax.fori_loop(0, ts, step, (st_ref[:, :S5_SL], st_ref[:, S5_SL:]), unroll=8)
    st_ref[:, :S5_SL] = xr
    st_ref[:, S5_SL:] = xi

    y = (jnp.dot(x_ref[...].astype(BF16), wc_ref[0], preferred_element_type=F32)
         + d_ref[0] * u.astype(F32))
    z_ref[...] = jax.nn.gelu(y, approximate=True).astype(BF16)


def s5_scan(u_tm, ops, *, ts=256):
    wb, a, wc, d = ops
    rows = ts * BATCH
    j3 = lambda j, t: (j, 0, 0)
    return pl.pallas_call(
        functools.partial(_s5_kernel, ts=ts),
        grid=(S5_LANE_BLOCKS, SEQ // ts),
        in_specs=[pl.BlockSpec((rows, LANE), lambda j, t: (t, j)),
                  pl.BlockSpec((1, LANE, 2 * S5_SL), j3),
                  pl.BlockSpec((1, 2, S5_SL), j3),
                  pl.BlockSpec((1, 2 * S5_SL, LANE), j3),
                  pl.BlockSpec((1, 1, LANE), j3)],
        out_specs=pl.BlockSpec((rows, LANE), lambda j, t: (t, j)),
        out_shape=jax.ShapeDtypeStruct((SEQ * BATCH, S5_WIDTH), BF16),
        scratch_shapes=[pltpu.VMEM((rows, 2 * S5_SL), F32), pltpu.VMEM((BATCH, 2 * S5_SL), F32)],
        compiler_params=_cparams(("parallel", "arbitrary")),
        name="s5_scan",
    )(u_tm, wb, a, wc, d)


def _s5_operators(a_re, a_im, log_dt, b_re, b_im, c_re, c_im, d_skip):
    dt = jnp.exp(log_dt)[:, None]
    lr, li = a_re, a_im
    mag = jnp.exp(lr * dt)
    ar, ai = mag * jnp.cos(li * dt), mag * jnp.sin(li * dt)
    den = lr * lr + li * li
    zr, zi = ar - 1.0, ai
    fr = (zr * lr + zi * li) / den
    fi = (zi * lr - zr * li) / den
    bbr = fr[..., None] * b_re - fi[..., None] * b_im
    bbi = fr[..., None] * b_im + fi[..., None] * b_re
    nb, gb = S5_LANE_BLOCKS, S5_GB
    eye = jnp.eye(gb, dtype=F32)

    def in_map(bb):
        return jnp.einsum('jgnp,gh->jgphn', bb.reshape(nb, gb, S5_N, S5_P), eye).reshape(
            nb, LANE, S5_SL)

    def out_map(cc):
        return jnp.einsum('jgpn,gh->jhngp', cc.reshape(nb, gb, S5_P, S5_N), eye).reshape(
            nb, S5_SL, LANE)

    wb = jnp.concatenate([in_map(bbr), in_map(bbi)], axis=2).astype(BF16)
    wc = jnp.concatenate([out_map(c_re), out_map(-c_im)], axis=1).astype(BF16)
    a = jnp.stack([ar.reshape(nb, S5_SL), ai.reshape(nb, S5_SL)], axis=1)
    return wb, a, wc, d_skip.reshape(nb, 1, LANE)


def _glu_kernel(z_ref, zc_ref, w_ref, b_ref, o_ref):
    gate = jnp.dot(z_ref[...], w_ref[...], preferred_element_type=F32) + b_ref[...]
    o_ref[...] = (zc_ref[...].astype(F32) * jax.nn.sigmoid(gate)).astype(BF16)


def s5_glu(z_tm, w, b, *, tm=1024, tn=512):
    n = w.shape[1]
    nt = SEQ // tm
    nj = n // tn
    z2 = z_tm.reshape(SEQ, BATCH * n)
    return pl.pallas_call(
        _glu_kernel,
        grid=(BATCH, nt, nj),
        in_specs=[pl.BlockSpec((tm, n), lambda b, i, j: (i, b)),
                  pl.BlockSpec((tm, tn), lambda b, i, j: (i, b * nj + j)),
                  pl.BlockSpec((n, tn), lambda b, i, j: (0, j)),
                  pl.BlockSpec((1, tn), lambda b, i, j: (0, j))],
        out_specs=pl.BlockSpec((tm, tn), lambda b, i, j: (b * nt + i, j)),
        out_shape=jax.ShapeDtypeStruct((TOKENS, n), BF16),
        compiler_params=_cparams(("parallel", "parallel", "arbitrary")),
        name="s5_glu",
    )(z2, z2, w, b.reshape(1, n))


def _diff_attn_kernel(scal_ref, q_ref, k_ref, v_ref, pq_ref, pk_ref, gn_ref, o_ref,
                      m0, l0, acc0, m1, l1, acc1, *, tq, out_scale):
    h = pl.program_id(1)
    qi = pl.program_id(2)
    slope = scal_ref[h]
    lam = scal_ref[DIFF_HEADS]
    for m_ref, l_ref, acc_ref in ((m0, l0, acc0), (m1, l1, acc1)):
        m_ref[...] = jnp.full_like(m_ref, NEG_BIG)
        l_ref[...] = jnp.zeros_like(l_ref)
        acc_ref[...] = jnp.zeros_like(acc_ref)

    lane = lax.broadcasted_iota(jnp.int32, (tq, 2 * DIFF_DQK), 1)
    q = q_ref[...].astype(F32) * DIFF_DQK ** -0.5
    qa = jnp.where(lane < DIFF_DQK, q, 0.0).astype(BF16)
    qb = jnp.where(lane >= DIFF_DQK, q, 0.0).astype(BF16)
    pq = _lane_tile(jnp.broadcast_to(pq_ref[...], (tq, LANE)), tq)
    rows = slice(0, tq)

    def scores(off):
        kb = k_ref[pl.ds(off, tq), :]
        pk = pk_ref[0, :, pl.ds(off, tq)]
        bias = slope * jnp.abs(pq - pk)
        return _nt_dot(qa, kb) - bias, _nt_dot(qb, kb) - bias

    def body(j, carry):
        off = pl.multiple_of(j * tq, tq)
        sa, sb = scores(off)
        vb = _with_ones(v_ref[pl.ds(off, tq), :])
        _softmax_step(sa, vb, m0, l0, acc0, rows)
        _softmax_step(sb, vb, m1, l1, acc1, rows)
        return carry

    lax.fori_loop(0, qi, body, 0)
    off = pl.multiple_of(qi * tq, tq)
    sa, sb = scores(off)
    mask = _causal_mask(tq, tq)
    vb = _with_ones(v_ref[pl.ds(off, tq), :])
    _softmax_step(jnp.where(mask, sa, NEG_BIG), vb, m0, l0, acc0, rows)
    _softmax_step(jnp.where(mask, sb, NEG_BIG), vb, m1, l1, acc1, rows)
    o = acc0[...] / l0[...] - lam * (acc1[...] / l1[...])
    o_ref[...] = (_rms(o, gn_ref[...]) * out_scale).astype(BF16)


def diff_attention(scal, proj, posq, posk, gn, *, out_scale, tq=512):
    nq = SEQ // tq
    hb = lambda base: (lambda b, h, i, s: (b, base // DIFF_DV + h))
    return pl.pallas_call(
        functools.partial(_diff_attn_kernel, tq=tq, out_scale=out_scale),
        grid_spec=pltpu.PrefetchScalarGridSpec(
            num_scalar_prefetch=1,
            grid=(BATCH, DIFF_HEADS, nq),
            in_specs=[pl.BlockSpec((tq, DIFF_DV), lambda b, h, i, s: (b * nq + i, OD_Q // DIFF_DV + h)),
                      pl.BlockSpec((SEQ, DIFF_DV), hb(OD_K)),
                      pl.BlockSpec((SEQ, DIFF_DV), hb(OD_V)),
                      pl.BlockSpec((tq, 1), lambda b, h, i, s: (b * nq + i, 0)),
                      pl.BlockSpec((1, 1, SEQ), lambda b, h, i, s: (b, 0, 0)),
                      pl.BlockSpec((1, DIFF_DV), lambda b, h, i, s: (0, 0))],
            out_specs=pl.BlockSpec((tq, DIFF_DV), lambda b, h, i, s: (b * nq + i, h)),
            scratch_shapes=[pltpu.VMEM((tq, LANE), F32), pltpu.VMEM((tq, LANE), F32),
                            pltpu.VMEM((tq, DIFF_DV), F32)] * 2),
        out_shape=jax.ShapeDtypeStruct((TOKENS, DIFF_HEADS * DIFF_DV), BF16),
        compiler_params=_cparams(("parallel", "parallel", "arbitrary")),
        name="diff_attention",
    )(scal, proj, proj, proj, posq, posk, gn.reshape(1, DIFF_DV))


OD_Q = 0
OD_K = 1024
OD_V = 2048


def _pack_even_w_in(w):
    cq, ckv, kr, gq, gk, gv, glr, gout = jnp.split(
        w, [512, 1024, 1088, 1600, 2112, 3136, 3152], axis=1)
    half = MLA_ROPE // 2
    kr_dup = jnp.concatenate([kr, kr[:, half:], kr[:, :half]], axis=1)
    glr_pad = jnp.pad(glr, ((0, 0), (0, EV_COLS - EV_GLR - GLA_GATE_RANK)))
    return jnp.concatenate([cq, ckv, gq, gk, gv, gout, kr_dup, glr_pad], axis=1).astype(BF16)


def _pack_w_uq(w):
    w = w.reshape(MLA_RANK, MLA_HEADS, MLA_NOPE + MLA_ROPE)
    half = MLA_ROPE // 2
    t1 = w[..., MLA_NOPE:MLA_NOPE + half]
    t2 = w[..., MLA_NOPE + half:]
    w = jnp.concatenate([w[..., :MLA_NOPE], t1, t2, t2, t1], axis=-1)
    return w.reshape(MLA_RANK, MLA_HEADS * MLA_QK).astype(BF16)


def _rope_tables(positions):
    half = MLA_ROPE // 2
    inv_freq = ROPE_THETA ** (-jnp.arange(half, dtype=F32) / half)
    ang = positions.astype(F32).reshape(TOKENS, 1) * inv_freq
    cos, sin = jnp.cos(ang), jnp.sin(ang)
    zeros = jnp.zeros((TOKENS, 2 * half), F32)
    return (jnp.concatenate([cos, cos, zeros], axis=1),
            jnp.concatenate([-sin, sin, zeros], axis=1))


def _even_mixer(h, g_mix, positions, w_in, q_norm, w_uq, kv_norm, w_ukv, w_gate_up, b_gate,
                g_norm, w_out):
    proj = norm_matmul(h, g_mix, _pack_even_w_in(w_in), tm=1024, tn=768)
    ct, st = _rope_tables(positions)
    q = mla_q_proj(proj, q_norm, _pack_w_uq(w_uq), ct, st)
    k, v = mla_kv_proj(proj, kv_norm, w_ukv.astype(BF16), ct, st)
    o_mla = mla_attention(q, k, v)
    wg = w_gate_up.reshape(GLA_GATE_RANK, GLA_HEADS, GLA_DK).transpose(1, 0, 2)
    wg = jnp.pad(wg, ((0, 0), (0, LANE - GLA_GATE_RANK), (0, 0))).astype(BF16)
    bg = b_gate.reshape(GLA_HEADS, 1, GLA_DK)
    o_gla = gla_mixer(proj, wg, bg, g_norm)
    return out_proj_residual(h, o_mla, o_gla, w_out.astype(BF16))


def _odd_mixer(h, g_mix, positions, layer, w_in, a_re, a_im, log_dt, b_re, b_im, c_re, c_im,
               d_skip, w_glu, b_glu, lq1, lk1, lq2, lk2, d_norm, w_out):
    lambda_init = 0.8 - 0.6 * math.exp(-0.3 * layer)
    u_tm, proj = odd_in_proj(h, g_mix, w_in.astype(BF16))
    z_tm = s5_scan(u_tm, _s5_operators(a_re, a_im, log_dt, b_re, b_im, c_re, c_im, d_skip))
    o_s5 = s5_glu(z_tm, w_glu.astype(BF16), b_glu)
    lam = (jnp.exp(jnp.sum(lq1 * lk1)) - jnp.exp(jnp.sum(lq2 * lk2)) + lambda_init)
    slopes = jnp.exp2(-8.0 * jnp.arange(1, DIFF_HEADS + 1, dtype=F32) / DIFF_HEADS)
    scal = jnp.concatenate([slopes, lam.reshape(1)]).astype(F32)
    posf = positions.astype(F32)
    o_diff = diff_attention(scal, proj, posf.reshape(TOKENS, 1), posf.reshape(BATCH, 1, SEQ),
                            d_norm, out_scale=1.0 - lambda_init)
    return out_proj_residual(h, o_s5, o_diff, w_out.astype(BF16))


def kernel(x, positions, norm_mix, norm_ffn, final_norm, ffn_w_gate, ffn_w_up, ffn_w_down, ag_w_in, mla_q_norm, mla_w_uq, mla_kv_norm, mla_w_ukv, gla_w_gate_up, gla_b_gate, gla_norm, ag_w_out, cd_w_in, s5_a_re, s5_a_im, s5_log_dt, s5_b_re, s5_b_im, s5_c_re, s5_c_im, s5_d, s5_w_glu, s5_b_glu, diff_lambda_q1, diff_lambda_k1, diff_lambda_q2, diff_lambda_k2, diff_norm, cd_w_out):
    h = x.reshape(TOKENS, D_MODEL)
    w_gate, w_up, w_down = (w.astype(BF16) for w in (ffn_w_gate, ffn_w_up, ffn_w_down))
    for layer in range(DEPTH):
        i = layer // 2
        if layer % 2 == 0:
            h = _even_mixer(h, norm_mix[layer], positions, ag_w_in[i], mla_q_norm[i],
                            mla_w_uq[i], mla_kv_norm[i], mla_w_ukv[i], gla_w_gate_up[i],
                            gla_b_gate[i], gla_norm[i], ag_w_out[i])
        else:
            h = _odd_mixer(h, norm_mix[layer], positions, layer, cd_w_in[i], s5_a_re[i],
                           s5_a_im[i], s5_log_dt[i], s5_b_re[i], s5_b_im[i], s5_c_re[i],
                           s5_c_im[i], s5_d[i], s5_w_glu[i], s5_b_glu[i], diff_lambda_q1[i],
                           diff_lambda_k1[i], diff_lambda_q2[i], diff_lambda_k2[i],
                           diff_norm[i], cd_w_out[i])
        h = ffn_residual(h, norm_ffn[layer], w_gate, w_up, w_down, final_norm, layer,
                         final_norm=(layer == DEPTH - 1))
    return h.reshape(BATCH, SEQ, D_MODEL)
```

```python
import functools
import math

import jax
import jax.numpy as jnp
from jax import lax
from jax.experimental import pallas as pl
from jax.experimental.pallas import tpu as pltpu

F32 = jnp.float32
BF16 = jnp.bfloat16

D_MODEL = 2048
BATCH = 8
SEQ = 2048
DEPTH = 2
TOKENS = BATCH * SEQ
MIX_HALF = D_MODEL // 2
RMS_EPS = 1e-6

MLA_NOPE = 128
MLA_ROPE = 64
MLA_V = 128
MLA_HEADS = 8
MLA_RANK = 512
ROPE_THETA = 10000.0
MLA_QK = 256

GLA_HEADS = 4
GLA_DK = 128
GLA_DV = 256
GLA_GATE_RANK = 16
GLA_GATE_NORM = 16.0
GLA_CHUNK = 128

S5_WIDTH = MIX_HALF
S5_P = 16
S5_GROUPS = 64
S5_N = 64
S5_GB = 8
S5_LANE_BLOCKS = S5_GROUPS // S5_GB
S5_SL = S5_GB * S5_N

DIFF_DQK = 64
DIFF_DV = 128
DIFF_HEADS = 8

FFN_HIDDEN = 5632

LANE = 128
NEG_BIG = -1e30
LOG2_E = math.log2(math.e)

EV_CQ = 0
EV_CKV = 512
EV_GQ = 1024
EV_GK = 1536
EV_GV = 2048
EV_GOUT = 3072
EV_KROPE = 4096
EV_GLR = 4224
EV_COLS = 4608


def _cparams(sem, vmem_mb=48):
    return pltpu.CompilerParams(dimension_semantics=sem,
                                vmem_limit_bytes=vmem_mb * 1024 * 1024)


def _rms(x, g):
    ms = jnp.mean(x * x, axis=-1, keepdims=True)
    return x * lax.rsqrt(ms + RMS_EPS) * g


def _nt_dot(a, b):
    return lax.dot_general(a, b, (((1,), (1,)), ((), ())), preferred_element_type=F32)


def _norm_matmul_kernel(x_ref, g_ref, w_ref, o_ref, xn_ref):
    @pl.when(pl.program_id(1) == 0)
    def _():
        xn_ref[...] = _rms(x_ref[...].astype(F32), g_ref[...]).astype(BF16)

    o_ref[...] = jnp.dot(xn_ref[...], w_ref[...],
                         preferred_element_type=F32).astype(o_ref.dtype)


def norm_matmul(x, g, w, *, tm=512, tn=512, out_dtype=BF16):
    t, k = x.shape
    n = w.shape[1]
    return pl.pallas_call(
        _norm_matmul_kernel,
        grid=(t // tm, n // tn),
        in_specs=[pl.BlockSpec((tm, k), lambda i, j: (i, 0)),
                  pl.BlockSpec((1, k), lambda i, j: (0, 0)),
                  pl.BlockSpec((k, tn), lambda i, j: (0, j))],
        out_specs=pl.BlockSpec((tm, tn), lambda i, j: (i, j)),
        out_shape=jax.ShapeDtypeStruct((t, n), out_dtype),
        scratch_shapes=[pltpu.VMEM((tm, k), BF16)],
        compiler_params=_cparams(("parallel", "arbitrary")),
        name="norm_matmul",
    )(x, g.reshape(1, k), w)


def _odd_in_proj_kernel(x_ref, g_ref, w_ref, u_ref, qkv_ref, xn_ref):
    j = pl.program_id(1)

    @pl.when(j == 0)
    def _():
        xn_ref[...] = _rms(x_ref[...], g_ref[...]).astype(BF16)

    y = jnp.dot(xn_ref[...], w_ref[...], preferred_element_type=F32).astype(BF16)

    @pl.when(j == 0)
    def _():
        u_ref[...] = y

    @pl.when(j > 0)
    def _():
        qkv_ref[...] = y


def odd_in_proj(x, g, w, *, tm=1024):
    tn = S5_WIDTH
    nt = SEQ // tm
    n_att = w.shape[1] - S5_WIDTH
    u, qkv = pl.pallas_call(
        _odd_in_proj_kernel,
        grid=(TOKENS // tm, w.shape[1] // tn),
        in_specs=[pl.BlockSpec((tm, D_MODEL), lambda i, j: (i, 0)),
                  pl.BlockSpec((1, D_MODEL), lambda i, j: (0, 0)),
                  pl.BlockSpec((D_MODEL, tn), lambda i, j: (0, j))],
        out_specs=[pl.BlockSpec((tm, tn), lambda i, j: (i % nt, i // nt)),
                   pl.BlockSpec((tm, tn), lambda i, j: (i, jnp.maximum(j - 1, 0)))],
        out_shape=[jax.ShapeDtypeStruct((SEQ, BATCH * S5_WIDTH), BF16),
                   jax.ShapeDtypeStruct((TOKENS, n_att), BF16)],
        scratch_shapes=[pltpu.VMEM((tm, D_MODEL), BF16)],
        compiler_params=_cparams(("parallel", "arbitrary")),
        name="odd_in_proj",
    )(x, g.reshape(1, D_MODEL), w)
    return u.reshape(SEQ * BATCH, S5_WIDTH), qkv


def _rope_half(blk, ct, st):
    return blk * ct + pltpu.roll(blk, 64, 1) * st


def _mla_proj_kernel(cq_ref, ckv_ref, kr_ref, gq_ref, gkv_ref, wq_ref, wkv_ref, ct_ref, st_ref,
                     q_ref, k_ref, v_ref, *, scale):
    ct, st = ct_ref[...], st_ref[...]
    xq = _rms(cq_ref[...].astype(F32), gq_ref[...]).astype(BF16)
    xkv = _rms(ckv_ref[...].astype(F32), gkv_ref[...]).astype(BF16)
    kr = _rope_half(kr_ref[...].astype(F32), ct, st).astype(BF16)
    for h in range(MLA_HEADS):
        lo, mid, hi = h * MLA_QK, h * MLA_QK + LANE, (h + 1) * MLA_QK
        a = jnp.dot(xq, wq_ref[:, lo:hi], preferred_element_type=F32)
        q_ref[:, lo:mid] = (a[:, :LANE] * scale).astype(BF16)
        q_ref[:, mid:hi] = (_rope_half(a[:, LANE:], ct, st) * scale).astype(BF16)
        kv = jnp.dot(xkv, wkv_ref[:, lo:hi], preferred_element_type=F32)
        k_ref[:, lo:mid] = kv[:, :LANE].astype(BF16)
        k_ref[:, mid:hi] = kr
        v_ref[:, h * MLA_V:(h + 1) * MLA_V] = kv[:, LANE:].astype(BF16)


def mla_proj(proj, gq, gkv, wq, wkv, ct, st, *, tm=512):
    scale = (MLA_NOPE + MLA_ROPE) ** -0.5 * LOG2_E
    row = lambda i: (i, 0)
    fixed = lambda i: (0, 0)
    return pl.pallas_call(
        functools.partial(_mla_proj_kernel, scale=scale),
        grid=(TOKENS // tm,),
        in_specs=[pl.BlockSpec((tm, MLA_RANK), lambda i: (i, EV_CQ // MLA_RANK)),
                  pl.BlockSpec((tm, MLA_RANK), lambda i: (i, EV_CKV // MLA_RANK)),
                  pl.BlockSpec((tm, LANE), lambda i: (i, EV_KROPE // LANE)),
                  pl.BlockSpec((1, MLA_RANK), fixed),
                  pl.BlockSpec((1, MLA_RANK), fixed),
                  pl.BlockSpec((MLA_RANK, MLA_HEADS * MLA_QK), fixed),
                  pl.BlockSpec((MLA_RANK, MLA_HEADS * (MLA_NOPE + MLA_V)), fixed),
                  pl.BlockSpec((tm, LANE), row),
                  pl.BlockSpec((tm, LANE), row)],
        out_specs=[pl.BlockSpec((tm, MLA_HEADS * MLA_QK), row),
                   pl.BlockSpec((tm, MLA_HEADS * MLA_QK), row),
                   pl.BlockSpec((tm, MLA_HEADS * MLA_V), row)],
        out_shape=[jax.ShapeDtypeStruct((TOKENS, MLA_HEADS * MLA_QK), BF16),
                   jax.ShapeDtypeStruct((TOKENS, MLA_HEADS * MLA_QK), BF16),
                   jax.ShapeDtypeStruct((TOKENS, MLA_HEADS * MLA_V), BF16)],
        compiler_params=_cparams(("parallel",)),
        name="mla_proj",
    )(proj, proj, proj, gq.reshape(1, MLA_RANK), gkv.reshape(1, MLA_RANK), wq, wkv, ct, st)


def _causal_mask(tq, tk):
    row = lax.broadcasted_iota(jnp.int32, (tq, tk), 0)
    col = lax.broadcasted_iota(jnp.int32, (tq, tk), 1)
    return col <= row


def _lane_tile(x, width):
    return jnp.concatenate([x] * (width // LANE), axis=1)


def _with_ones(v):
    return jnp.concatenate([v, jnp.ones_like(v)], axis=1)


def _softmax_step(s, v_ext, m_ref, l_ref, acc_ref, rows):
    m_prev = m_ref[rows, :]
    m_new = jnp.maximum(m_prev, jnp.max(s, axis=1, keepdims=True))
    p = jnp.exp2(s - _lane_tile(m_new, s.shape[1]))
    alpha = jnp.exp2(m_prev - m_new)
    pv = jnp.dot(p.astype(BF16), v_ext, preferred_element_type=F32)
    dv = acc_ref.shape[1]
    acc_ref[rows, :] = alpha * acc_ref[rows, :] + pv[:, :dv]
    l_ref[rows, :] = alpha * l_ref[rows, :] + pv[:, dv:]
    m_ref[rows, :] = m_new


def _mla_attn_kernel(q_ref, k_ref, v_ref, o_ref, m_ref, l_ref, acc_ref, *, tq, chains):
    qi = pl.program_id(2)
    m_ref[...] = jnp.full_like(m_ref, NEG_BIG)
    l_ref[...] = jnp.zeros_like(l_ref)
    acc_ref[...] = jnp.zeros_like(acc_ref)
    rc = tq // chains

    def body(j, carry):
        off = pl.multiple_of(j * tq, tq)
        kb = k_ref[pl.ds(off, tq), :]
        vb = _with_ones(v_ref[pl.ds(off, tq), :])
        for c in range(chains):
            rows = slice(c * rc, (c + 1) * rc)
            _softmax_step(_nt_dot(q_ref[rows, :], kb), vb, m_ref, l_ref, acc_ref, rows)
        return carry

    lax.fori_loop(0, qi, body, 0)
    off = pl.multiple_of(qi * tq, tq)
    for c in range(chains):
        rows = slice(c * rc, (c + 1) * rc)
        width = (c + 1) * rc
        kb = k_ref[pl.ds(off, width), :]
        vb = _with_ones(v_ref[pl.ds(off, width), :])
        s = _nt_dot(q_ref[rows, :], kb)
        row = lax.broadcasted_iota(jnp.int32, s.shape, 0) + c * rc
        col = lax.broadcasted_iota(jnp.int32, s.shape, 1)
        _softmax_step(jnp.where(col <= row, s, NEG_BIG), vb, m_ref, l_ref, acc_ref, rows)
    o_ref[...] = (acc_ref[...] / l_ref[...]).astype(BF16)


def mla_attention(q, k, v, *, tq=1024, chains=2):
    nq = SEQ // tq
    return pl.pallas_call(
        functools.partial(_mla_attn_kernel, tq=tq, chains=chains),
        grid=(BATCH, MLA_HEADS, nq),
        in_specs=[pl.BlockSpec((tq, MLA_QK), lambda b, h, i: (b * nq + i, h)),
                  pl.BlockSpec((SEQ, MLA_QK), lambda b, h, i: (b, h)),
                  pl.BlockSpec((SEQ, MLA_V), lambda b, h, i: (b, h))],
        out_specs=pl.BlockSpec((tq, MLA_V), lambda b, h, i: (b * nq + i, h)),
        out_shape=jax.ShapeDtypeStruct((TOKENS, MLA_HEADS * MLA_V), BF16),
        scratch_shapes=[pltpu.VMEM((tq, LANE), F32), pltpu.VMEM((tq, LANE), F32),
                        pltpu.VMEM((tq, MLA_V), F32)],
        compiler_params=_cparams(("parallel", "parallel", "arbitrary")),
        name="mla_attention",
    )(q, k, v)


def _split3(x):
    hi = x.astype(BF16)
    r1 = x - hi.astype(F32)
    mid = r1.astype(BF16)
    lo = (r1 - mid.astype(F32)).astype(BF16)
    return hi, mid, lo


def _gla_kernel(q_ref, k_ref, v_ref, go_ref, glr_ref, wg_ref, bg_ref, gn_ref, o_ref, st_ref,
                *, chunk):
    @pl.when(pl.program_id(1) == 0)
    def _():
        st_ref[...] = jnp.zeros_like(st_ref)

    z = jnp.dot(glr_ref[...], wg_ref[...], preferred_element_type=F32) + bg_ref[...]
    log_a = -(jnp.maximum(-z, 0.0) + jnp.log1p(jnp.exp(-jnp.abs(z)))) / GLA_GATE_NORM
    mask = _causal_mask(chunk, chunk)
    tri = mask.astype(BF16)
    hi, mid, lo = _split3(log_a)
    cum_all = (jnp.dot(tri, hi, preferred_element_type=F32)
               + jnp.dot(tri, mid, preferred_element_type=F32)
               + jnp.dot(tri, lo, preferred_element_type=F32))

    for h in range(GLA_HEADS):
        ks = slice(h * GLA_DK, (h + 1) * GLA_DK)
        vs = slice(h * GLA_DV, (h + 1) * GLA_DV)
        cum = cum_all[:, ks]
        last = cum[chunk - 1:chunk, :]
        q = q_ref[:, ks].astype(F32) * GLA_DK ** -0.5
        k = k_ref[:, ks].astype(F32)
        v = v_ref[:, vs]
        st = st_ref[h]
        q_in = (q * jnp.exp(cum)).astype(BF16)
        q_hat = (q * jnp.exp(cum - last)).astype(BF16)
        k_hat = (k * jnp.exp(last - cum)).astype(BF16)

        o = _nt_dot(q_in, st.astype(BF16))
        attn = jnp.where(mask, _nt_dot(q_hat, k_hat), 0.0)
        o = o + jnp.dot(attn.astype(BF16), v, preferred_element_type=F32)
        vt = v.astype(F32).T.astype(BF16)
        st_ref[h] = st * jnp.exp(last) + jnp.dot(vt, k_hat, preferred_element_type=F32)

        o = _rms(o, gn_ref[...])
        g = go_ref[:, vs].astype(F32)
        o_ref[:, vs] = (o * (g * jax.nn.sigmoid(g))).astype(BF16)


def gla_mixer(proj, wg, bg, gn, *, chunk=GLA_CHUNK):
    nc = SEQ // chunk
    kw, vw = GLA_HEADS * GLA_DK, GLA_HEADS * GLA_DV
    row = lambda b, c: b * nc + c
    return pl.pallas_call(
        functools.partial(_gla_kernel, chunk=chunk),
        grid=(BATCH, nc),
        in_specs=[pl.BlockSpec((chunk, kw), lambda b, c: (row(b, c), EV_GQ // kw)),
                  pl.BlockSpec((chunk, kw), lambda b, c: (row(b, c), EV_GK // kw)),
                  pl.BlockSpec((chunk, vw), lambda b, c: (row(b, c), EV_GV // vw)),
                  pl.BlockSpec((chunk, vw), lambda b, c: (row(b, c), EV_GOUT // vw)),
                  pl.BlockSpec((chunk, LANE), lambda b, c: (row(b, c), EV_GLR // LANE)),
                  pl.BlockSpec((LANE, kw), lambda b, c: (0, 0)),
                  pl.BlockSpec((1, kw), lambda b, c: (0, 0)),
                  pl.BlockSpec((1, GLA_DV), lambda b, c: (0, 0))],
        out_specs=pl.BlockSpec((chunk, vw), lambda b, c: (row(b, c), 0)),
        out_shape=jax.ShapeDtypeStruct((TOKENS, vw), BF16),
        scratch_shapes=[pltpu.VMEM((GLA_HEADS, GLA_DV, GLA_DK), F32)],
        compiler_params=_cparams(("parallel", "arbitrary")),
        name="gla_mixer",
    )(proj, proj, proj, proj, proj, wg, bg, gn.reshape(1, GLA_DV))


def _out_proj_kernel(r_ref, a1_ref, a2_ref, w1_ref, w2_ref, o_ref):
    o_ref[...] = (r_ref[...]
                  + jnp.dot(a1_ref[...], w1_ref[...], preferred_element_type=F32)
                  + jnp.dot(a2_ref[...], w2_ref[...], preferred_element_type=F32))


def out_proj_residual(r, a1, a2, w, *, tm=512):
    half = a1.shape[1]
    n = w.shape[1]
    return pl.pallas_call(
        _out_proj_kernel,
        grid=(TOKENS // tm,),
        in_specs=[pl.BlockSpec((tm, n), lambda i: (i, 0)),
                  pl.BlockSpec((tm, half), lambda i: (i, 0)),
                  pl.BlockSpec((tm, half), lambda i: (i, 0)),
                  pl.BlockSpec((half, n), lambda i: (0, 0)),
                  pl.BlockSpec((half, n), lambda i: (1, 0))],
        out_specs=pl.BlockSpec((tm, n), lambda i: (i, 0)),
        out_shape=jax.ShapeDtypeStruct((TOKENS, n), F32),
        compiler_params=_cparams(("parallel",)),
        name="out_proj_residual",
    )(r, a1, a2, w, w)


def _ffn_kernel(h_ref, g_ref, wg_ref, wu_ref, wd_ref, fg_ref, o_ref, xn_ref, *, final_norm):
    f = pl.program_id(1)

    @pl.when(f == 0)
    def _():
        x = h_ref[...]
        xn_ref[...] = _rms(x, g_ref[...]).astype(BF16)
        o_ref[...] = x

    xn = xn_ref[...]
    a = jnp.dot(xn, wg_ref[0], preferred_element_type=F32)
    b = jnp.dot(xn, wu_ref[0], preferred_element_type=F32)
    act = (a * jax.nn.sigmoid(a) * b).astype(BF16)
    o_ref[...] += jnp.dot(act, wd_ref[0], preferred_element_type=F32)

    if final_norm:
        @pl.when(f == pl.num_programs(1) - 1)
        def _():
            o_ref[...] = _rms(o_ref[...], fg_ref[...])


def ffn_residual(h, g, wg, wu, wd, fg, layer, *, final_norm, tm=512, tf=512):
    d = h.shape[1]
    hidden = wg.shape[2]
    return pl.pallas_call(
        functools.partial(_ffn_kernel, final_norm=final_norm),
        grid=(TOKENS // tm, hidden // tf),
        in_specs=[pl.BlockSpec((tm, d), lambda i, f: (i, 0)),
                  pl.BlockSpec((1, d), lambda i, f: (0, 0)),
                  pl.BlockSpec((1, d, tf), lambda i, f: (layer, 0, f)),
                  pl.BlockSpec((1, d, tf), lambda i, f: (layer, 0, f)),
                  pl.BlockSpec((1, tf, d), lambda i, f: (layer, f, 0)),
                  pl.BlockSpec((1, d), lambda i, f: (0, 0))],
        out_specs=pl.BlockSpec((tm, d), lambda i, f: (i, 0)),
        out_shape=jax.ShapeDtypeStruct((TOKENS, d), F32),
        scratch_shapes=[pltpu.VMEM((tm, d), BF16)],
        compiler_params=_cparams(("parallel", "arbitrary")),
        name="ffn_residual",
    )(h, g.reshape(1, d), wg, wu, wd, fg.reshape(1, d))


def _s5_kernel(u_ref, wb_ref, a_ref, wc_ref, d_ref, z_ref, x_ref, st_ref, *, ts):
    @pl.when(pl.program_id(1) == 0)
    def _():
        st_ref[...] = jnp.zeros_like(st_ref)

    u = u_ref[...]
    x_ref[...] = jnp.dot(u, wb_ref[0], preferred_element_type=F32)
    ar = a_ref[0, 0:1, :]
    ai = a_ref[0, 1:2, :]

    def step(t, carry):
        xr, xi = carry
        rows = pl.ds(pl.multiple_of(t * BATCH, BATCH), BATCH)
        nr = ar * xr - ai * xi + x_ref[rows, :S5_SL]
        ni = ar * xi + ai * xr + x_ref[rows, S5_SL:]
        x_ref[rows, :S5_SL] = nr
        x_ref[rows, S5_SL:] = ni
        return nr, ni

    xr, xi = lax.fori_loop(0, ts, step, (st_ref[:, :S5_SL], st_ref[:, S5_SL:]), unroll=8)
    st_ref[:, :S5_SL] = xr
    st_ref[:, S5_SL:] = xi

    y = (jnp.dot(x_ref[...].astype(BF16), wc_ref[0], preferred_element_type=F32)
         + d_ref[0] * u.astype(F32))
    z_ref[...] = jax.nn.gelu(y, approximate=True).astype(BF16)


def s5_scan(u_tm, ops, *, ts=256):
    wb, a, wc, d = ops
    rows = ts * BATCH
    j3 = lambda j, t: (j, 0, 0)
    return pl.pallas_call(
        functools.partial(_s5_kernel, ts=ts),
        grid=(S5_LANE_BLOCKS, SEQ // ts),
        in_specs=[pl.BlockSpec((rows, LANE), lambda j, t: (t, j)),
                  pl.BlockSpec((1, LANE, 2 * S5_SL), j3),
                  pl.BlockSpec((1, 2, S5_SL), j3),
                  pl.BlockSpec((1, 2 * S5_SL, LANE), j3),
                  pl.BlockSpec((1, 1, LANE), j3)],
        out_specs=pl.BlockSpec((rows, LANE), lambda j, t: (t, j)),
        out_shape=jax.ShapeDtypeStruct((SEQ * BATCH, S5_WIDTH), BF16),
        scratch_shapes=[pltpu.VMEM((rows, 2 * S5_SL), F32), pltpu.VMEM((BATCH, 2 * S5_SL), F32)],
        compiler_params=_cparams(("parallel", "arbitrary")),
        name="s5_scan",
    )(u_tm, wb, a, wc, d)


def _s5_operators(a_re, a_im, log_dt, b_re, b_im, c_re, c_im, d_skip):
    dt = jnp.exp(log_dt)[:, None]
    lr, li = a_re, a_im
    mag = jnp.exp(lr * dt)
    ar, ai = mag * jnp.cos(li * dt), mag * jnp.sin(li * dt)
    den = lr * lr + li * li
    zr, zi = ar - 1.0, ai
    fr = (zr * lr + zi * li) / den
    fi = (zi * lr - zr * li) / den
    bbr = fr[..., None] * b_re - fi[..., None] * b_im
    bbi = fr[..., None] * b_im + fi[..., None] * b_re
    nb, gb = S5_LANE_BLOCKS, S5_GB
    eye = jnp.eye(gb, dtype=F32)

    def in_map(bb):
        return jnp.einsum('jgnp,gh->jgphn', bb.reshape(nb, gb, S5_N, S5_P), eye).reshape(
            nb, LANE, S5_SL)

    def out_map(cc):
        return jnp.einsum('jgpn,gh->jhngp', cc.reshape(nb, gb, S5_P, S5_N), eye).reshape(
            nb, S5_SL, LANE)

    wb = jnp.concatenate([in_map(bbr), in_map(bbi)], axis=2).astype(BF16)
    wc = jnp.concatenate([out_map(c_re), out_map(-c_im)], axis=1).astype(BF16)
    a = jnp.stack([ar.reshape(nb, S5_SL), ai.reshape(nb, S5_SL)], axis=1)
    return wb, a, wc, d_skip.reshape(nb, 1, LANE)


def _glu_kernel(z_ref, zc_ref, w_ref, b_ref, o_ref):
    gate = jnp.dot(z_ref[...], w_ref[...], preferred_element_type=F32) + b_ref[...]
    o_ref[...] = (zc_ref[...].astype(F32) * jax.nn.sigmoid(gate)).astype(BF16)


def s5_glu(z_tm, w, b, *, tm=1024, tn=512):
    n = w.shape[1]
    nt = SEQ // tm
    nj = n // tn
    z2 = z_tm.reshape(SEQ, BATCH * n)
    return pl.pallas_call(
        _glu_kernel,
        grid=(BATCH, nt, nj),
        in_specs=[pl.BlockSpec((tm, n), lambda b, i, j: (i, b)),
                  pl.BlockSpec((tm, tn), lambda b, i, j: (i, b * nj + j)),
                  pl.BlockSpec((n, tn), lambda b, i, j: (0, j)),
                  pl.BlockSpec((1, tn), lambda b, i, j: (0, j))],
        out_specs=pl.BlockSpec((tm, tn), lambda b, i, j: (b * nt + i, j)),
        out_shape=jax.ShapeDtypeStruct((TOKENS, n), BF16),
        compiler_params=_cparams(("parallel", "parallel", "arbitrary")),
        name="s5_glu",
    )(z2, z2, w, b.reshape(1, n))


def _diff_attn_kernel(scal_ref, q_ref, k_ref, v_ref, pq_ref, pk_ref, gn_ref, o_ref,
                      m0, l0, acc0, m1, l1, acc1, *, tq, out_scale):
    h = pl.program_id(1)
    qi = pl.program_id(2)
    slope = scal_ref[h]
    lam = scal_ref[DIFF_HEADS]
    for m_ref, l_ref, acc_ref in ((m0, l0, acc0), (m1, l1, acc1)):
        m_ref[...] = jnp.full_like(m_ref, NEG_BIG)
        l_ref[...] = jnp.zeros_like(l_ref)
        acc_ref[...] = jnp.zeros_like(acc_ref)

    lane = lax.broadcasted_iota(jnp.int32, (tq, 2 * DIFF_DQK), 1)
    q = q_ref[...].astype(F32) * (DIFF_DQK ** -0.5 * LOG2_E)
    qa = jnp.where(lane < DIFF_DQK, q, 0.0).astype(BF16)
    qb = jnp.where(lane >= DIFF_DQK, q, 0.0).astype(BF16)
    pq = jnp.broadcast_to(pq_ref[...], (tq, LANE))
    maps = ((qa, m0, l0, acc0), (qb, m1, l1, acc1))

    def alibi(rows, off, width):
        pk = pk_ref[0, :, pl.ds(off, width)]
        return slope * jnp.abs(_lane_tile(pq[rows, :], width) - pk)

    def body(j, carry):
        off = pl.multiple_of(j * tq, tq)
        rows = slice(0, tq)
        kb = k_ref[pl.ds(off, tq), :]
        vb = _with_ones(v_ref[pl.ds(off, tq), :])
        bias = alibi(rows, off, tq)
        for qm, m_ref, l_ref, acc_ref in maps:
            _softmax_step(_nt_dot(qm, kb) - bias, vb, m_ref, l_ref, acc_ref, rows)
        return carry

    lax.fori_loop(0, qi, body, 0)
    off = pl.multiple_of(qi * tq, tq)
    half = tq // 2
    for c in range(2):
        rows = slice(c * half, (c + 1) * half)
        width = (c + 1) * half
        kb = k_ref[pl.ds(off, width), :]
        vb = _with_ones(v_ref[pl.ds(off, width), :])
        bias = alibi(rows, off, width)
        row = lax.broadcasted_iota(jnp.int32, (half, width), 0) + c * half
        col = lax.broadcasted_iota(jnp.int32, (half, width), 1)
        for qm, m_ref, l_ref, acc_ref in maps:
            s = jnp.where(col <= row, _nt_dot(qm[rows, :], kb) - bias, NEG_BIG)
            _softmax_step(s, vb, m_ref, l_ref, acc_ref, rows)
    o = acc0[...] / l0[...] - lam * (acc1[...] / l1[...])
    o_ref[...] = (_rms(o, gn_ref[...]) * out_scale).astype(BF16)


def diff_attention(scal, proj, posq, posk, gn, *, out_scale, tq=1024):
    nq = SEQ // tq
    hb = lambda base: (lambda b, h, i, s: (b, base // DIFF_DV + h))
    return pl.pallas_call(
        functools.partial(_diff_attn_kernel, tq=tq, out_scale=out_scale),
        grid_spec=pltpu.PrefetchScalarGridSpec(
            num_scalar_prefetch=1,
            grid=(BATCH, DIFF_HEADS, nq),
            in_specs=[pl.BlockSpec((tq, DIFF_DV), lambda b, h, i, s: (b * nq + i, OD_Q // DIFF_DV + h)),
                      pl.BlockSpec((SEQ, DIFF_DV), hb(OD_K)),
                      pl.BlockSpec((SEQ, DIFF_DV), hb(OD_V)),
                      pl.BlockSpec((tq, 1), lambda b, h, i, s: (b * nq + i, 0)),
                      pl.BlockSpec((1, 1, SEQ), lambda b, h, i, s: (b, 0, 0)),
                      pl.BlockSpec((1, DIFF_DV), lambda b, h, i, s: (0, 0))],
            out_specs=pl.BlockSpec((tq, DIFF_DV), lambda b, h, i, s: (b * nq + i, h)),
            scratch_shapes=[pltpu.VMEM((tq, LANE), F32), pltpu.VMEM((tq, LANE), F32),
                            pltpu.VMEM((tq, DIFF_DV), F32)] * 2),
        out_shape=jax.ShapeDtypeStruct((TOKENS, DIFF_HEADS * DIFF_DV), BF16),
        compiler_params=_cparams(("parallel", "parallel", "arbitrary")),
        name="diff_attention",
    )(scal, proj, proj, proj, posq, posk, gn.reshape(1, DIFF_DV))


OD_Q = 0
OD_K = 1024
OD_V = 2048


def _pack_even_w_in(w):
    cq, ckv, kr, gq, gk, gv, glr, gout = jnp.split(
        w, [512, 1024, 1088, 1600, 2112, 3136, 3152], axis=1)
    half = MLA_ROPE // 2
    kr_dup = jnp.concatenate([kr, kr[:, half:], kr[:, :half]], axis=1)
    glr_pad = jnp.pad(glr, ((0, 0), (0, EV_COLS - EV_GLR - GLA_GATE_RANK)))
    return jnp.concatenate([cq, ckv, gq, gk, gv, gout, kr_dup, glr_pad], axis=1).astype(BF16)


def _pack_w_uq(w):
    w = w.reshape(MLA_RANK, MLA_HEADS, MLA_NOPE + MLA_ROPE)
    half = MLA_ROPE // 2
    t1 = w[..., MLA_NOPE:MLA_NOPE + half]
    t2 = w[..., MLA_NOPE + half:]
    w = jnp.concatenate([w[..., :MLA_NOPE], t1, t2, t2, t1], axis=-1)
    return w.reshape(MLA_RANK, MLA_HEADS * MLA_QK).astype(BF16)


def _rope_tables(positions):
    half = MLA_ROPE // 2
    inv_freq = ROPE_THETA ** (-jnp.arange(half, dtype=F32) / half)
    ang = positions.astype(F32).reshape(TOKENS, 1) * inv_freq
    cos, sin = jnp.cos(ang), jnp.sin(ang)
    zeros = jnp.zeros((TOKENS, 2 * half), F32)
    return (jnp.concatenate([cos, cos, zeros], axis=1),
            jnp.concatenate([-sin, sin, zeros], axis=1))


def _even_mixer(h, g_mix, positions, w_in, q_norm, w_uq, kv_norm, w_ukv, w_gate_up, b_gate,
                g_norm, w_out):
    proj = norm_matmul(h, g_mix, _pack_even_w_in(w_in), tm=1024, tn=768)
    ct, st = _rope_tables(positions)
    q, k, v = mla_proj(proj, q_norm, kv_norm, _pack_w_uq(w_uq), w_ukv.astype(BF16), ct, st)
    o_mla = mla_attention(q, k, v)
    wg = jnp.pad(w_gate_up, ((0, LANE - GLA_GATE_RANK), (0, 0))).astype(BF16)
    o_gla = gla_mixer(proj, wg, b_gate.reshape(1, GLA_HEADS * GLA_DK), g_norm)
    return out_proj_residual(h, o_mla, o_gla, w_out.astype(BF16))


def _odd_mixer(h, g_mix, positions, layer, w_in, a_re, a_im, log_dt, b_re, b_im, c_re, c_im,
               d_skip, w_glu, b_glu, lq1, lk1, lq2, lk2, d_norm, w_out):
    lambda_init = 0.8 - 0.6 * math.exp(-0.3 * layer)
    u_tm, proj = odd_in_proj(h, g_mix, w_in.astype(BF16))
    z_tm = s5_scan(u_tm, _s5_operators(a_re, a_im, log_dt, b_re, b_im, c_re, c_im, d_skip))
    o_s5 = s5_glu(z_tm, w_glu.astype(BF16), b_glu)
    lam = (jnp.exp(jnp.sum(lq1 * lk1)) - jnp.exp(jnp.sum(lq2 * lk2)) + lambda_init)
    slopes = jnp.exp2(-8.0 * jnp.arange(1, DIFF_HEADS + 1, dtype=F32) / DIFF_HEADS)
    scal = jnp.concatenate([slopes * LOG2_E, lam.reshape(1)]).astype(F32)
    posf = positions.astype(F32)
    o_diff = diff_attention(scal, proj, posf.reshape(TOKENS, 1), posf.reshape(BATCH, 1, SEQ),
                            d_norm, out_scale=1.0 - lambda_init)
    return out_proj_residual(h, o_s5, o_diff, w_out.astype(BF16))


def kernel(x, positions, norm_mix, norm_ffn, final_norm, ffn_w_gate, ffn_w_up, ffn_w_down, ag_w_in, mla_q_norm, mla_w_uq, mla_kv_norm, mla_w_ukv, gla_w_gate_up, gla_b_gate, gla_norm, ag_w_out, cd_w_in, s5_a_re, s5_a_im, s5_log_dt, s5_b_re, s5_b_im, s5_c_re, s5_c_im, s5_d, s5_w_glu, s5_b_glu, diff_lambda_q1, diff_lambda_k1, diff_lambda_q2, diff_lambda_k2, diff_norm, cd_w_out):
    h = x.reshape(TOKENS, D_MODEL)
    w_gate, w_up, w_down = (w.astype(BF16) for w in (ffn_w_gate, ffn_w_up, ffn_w_down))
    for layer in range(DEPTH):
        i = layer // 2
        if layer % 2 == 0:
            h = _even_mixer(h, norm_mix[layer], positions, ag_w_in[i], mla_q_norm[i],
                            mla_w_uq[i], mla_kv_norm[i], mla_w_ukv[i], gla_w_gate_up[i],
                            gla_b_gate[i], gla_norm[i], ag_w_out[i])
        else:
            h = _odd_mixer(h, norm_mix[layer], positions, layer, cd_w_in[i], s5_a_re[i],
                           s5_a_im[i], s5_log_dt[i], s5_b_re[i], s5_b_im[i], s5_c_re[i],
                           s5_c_im[i], s5_d[i], s5_w_glu[i], s5_b_glu[i], diff_lambda_q1[i],
                           diff_lambda_k1[i], diff_lambda_q2[i], diff_lambda_k2[i],
                           diff_norm[i], cd_w_out[i])
        h = ffn_residual(h, norm_ffn[layer], w_gate, w_up, w_down, final_norm, layer,
                         final_norm=(layer == DEPTH - 1))
    return h.reshape(BATCH, SEQ, D_MODEL)
```

```python
import functools
import math

import jax
import jax.numpy as jnp
from jax import lax
from jax.experimental import pallas as pl
from jax.experimental.pallas import tpu as pltpu

F32 = jnp.float32
BF16 = jnp.bfloat16

D_MODEL = 2048
BATCH = 8
SEQ = 2048
DEPTH = 2
TOKENS = BATCH * SEQ
MIX_HALF = D_MODEL // 2
RMS_EPS = 1e-6

MLA_NOPE = 128
MLA_ROPE = 64
MLA_V = 128
MLA_HEADS = 8
MLA_RANK = 512
ROPE_THETA = 10000.0
MLA_QK = 256

GLA_HEADS = 4
GLA_DK = 128
GLA_DV = 256
GLA_GATE_RANK = 16
GLA_GATE_NORM = 16.0
GLA_CHUNK = 128

S5_WIDTH = MIX_HALF
S5_P = 16
S5_GROUPS = 64
S5_N = 64
S5_GB = 8
S5_LANE_BLOCKS = S5_GROUPS // S5_GB
S5_SL = S5_GB * S5_N

DIFF_DQK = 64
DIFF_DV = 128
DIFF_HEADS = 8

FFN_HIDDEN = 5632

LANE = 128
NEG_BIG = -1e30
LOG2_E = math.log2(math.e)

EV_CQ = 0
EV_CKV = 512
EV_GQ = 1024
EV_GK = 1536
EV_GV = 2048
EV_GOUT = 3072
EV_KROPE = 4096
EV_GLR = 4224
EV_COLS = 4608


def _cparams(sem, vmem_mb=48):
    return pltpu.CompilerParams(dimension_semantics=sem,
                                vmem_limit_bytes=vmem_mb * 1024 * 1024)


def _rms(x, g):
    ms = jnp.mean(x * x, axis=-1, keepdims=True)
    return x * lax.rsqrt(ms + RMS_EPS) * g


def _nt_dot(a, b):
    return lax.dot_general(a, b, (((1,), (1,)), ((), ())), preferred_element_type=F32)


def _norm_matmul_kernel(x_ref, g_ref, w_ref, o_ref, xn_ref):
    @pl.when(pl.program_id(1) == 0)
    def _():
        xn_ref[...] = _rms(x_ref[...].astype(F32), g_ref[...]).astype(BF16)

    o_ref[...] = jnp.dot(xn_ref[...], w_ref[...],
                         preferred_element_type=F32).astype(o_ref.dtype)


def norm_matmul(x, g, w, *, tm=512, tn=512, out_dtype=BF16):
    t, k = x.shape
    n = w.shape[1]
    return pl.pallas_call(
        _norm_matmul_kernel,
        grid=(t // tm, n // tn),
        in_specs=[pl.BlockSpec((tm, k), lambda i, j: (i, 0)),
                  pl.BlockSpec((1, k), lambda i, j: (0, 0)),
                  pl.BlockSpec((k, tn), lambda i, j: (0, j))],
        out_specs=pl.BlockSpec((tm, tn), lambda i, j: (i, j)),
        out_shape=jax.ShapeDtypeStruct((t, n), out_dtype),
        scratch_shapes=[pltpu.VMEM((tm, k), BF16)],
        compiler_params=_cparams(("parallel", "arbitrary")),
        name="norm_matmul",
    )(x, g.reshape(1, k), w)


def _odd_in_proj_kernel(x_ref, g_ref, w_ref, u_ref, qkv_ref, xn_ref):
    j = pl.program_id(1)

    @pl.when(j == 0)
    def _():
        xn_ref[...] = _rms(x_ref[...], g_ref[...]).astype(BF16)

    y = jnp.dot(xn_ref[...], w_ref[...], preferred_element_type=F32).astype(BF16)

    @pl.when(j == 0)
    def _():
        u_ref[...] = y

    @pl.when(j > 0)
    def _():
        qkv_ref[...] = y


def odd_in_proj(x, g, w, *, tm=1024):
    tn = S5_WIDTH
    nt = SEQ // tm
    n_att = w.shape[1] - S5_WIDTH
    u, qkv = pl.pallas_call(
        _odd_in_proj_kernel,
        grid=(TOKENS // tm, w.shape[1] // tn),
        in_specs=[pl.BlockSpec((tm, D_MODEL), lambda i, j: (i, 0)),
                  pl.BlockSpec((1, D_MODEL), lambda i, j: (0, 0)),
                  pl.BlockSpec((D_MODEL, tn), lambda i, j: (0, j))],
        out_specs=[pl.BlockSpec((tm, tn), lambda i, j: (i % nt, i // nt)),
                   pl.BlockSpec((tm, tn), lambda i, j: (i, jnp.maximum(j - 1, 0)))],
        out_shape=[jax.ShapeDtypeStruct((SEQ, BATCH * S5_WIDTH), BF16),
                   jax.ShapeDtypeStruct((TOKENS, n_att), BF16)],
        scratch_shapes=[pltpu.VMEM((tm, D_MODEL), BF16)],
        compiler_params=_cparams(("parallel", "arbitrary")),
        name="odd_in_proj",
    )(x, g.reshape(1, D_MODEL), w)
    return u.reshape(SEQ * BATCH, S5_WIDTH), qkv


def _rope_half(blk, ct, st):
    return blk * ct + pltpu.roll(blk, 64, 1) * st


def _mla_proj_kernel(cq_ref, ckv_ref, kr_ref, gq_ref, gkv_ref, wq_ref, wkv_ref, ct_ref, st_ref,
                     q_ref, k_ref, v_ref, *, scale):
    ct, st = ct_ref[...], st_ref[...]
    xq = _rms(cq_ref[...].astype(F32), gq_ref[...]).astype(BF16)
    xkv = _rms(ckv_ref[...].astype(F32), gkv_ref[...]).astype(BF16)
    kr = _rope_half(kr_ref[...].astype(F32), ct, st).astype(BF16)
    for h in range(MLA_HEADS):
        lo, mid, hi = h * MLA_QK, h * MLA_QK + LANE, (h + 1) * MLA_QK
        a = jnp.dot(xq, wq_ref[:, lo:hi], preferred_element_type=F32)
        q_ref[:, lo:mid] = (a[:, :LANE] * scale).astype(BF16)
        q_ref[:, mid:hi] = (_rope_half(a[:, LANE:], ct, st) * scale).astype(BF16)
        kv = jnp.dot(xkv, wkv_ref[:, lo:hi], preferred_element_type=F32)
        k_ref[:, lo:mid] = kv[:, :LANE].astype(BF16)
        k_ref[:, mid:hi] = kr
        v_ref[:, h * MLA_V:(h + 1) * MLA_V] = kv[:, LANE:].astype(BF16)


def mla_proj(proj, gq, gkv, wq, wkv, ct, st, *, tm=512):
    scale = (MLA_NOPE + MLA_ROPE) ** -0.5 * LOG2_E
    row = lambda i: (i, 0)
    fixed = lambda i: (0, 0)
    return pl.pallas_call(
        functools.partial(_mla_proj_kernel, scale=scale),
        grid=(TOKENS // tm,),
        in_specs=[pl.BlockSpec((tm, MLA_RANK), lambda i: (i, EV_CQ // MLA_RANK)),
                  pl.BlockSpec((tm, MLA_RANK), lambda i: (i, EV_CKV // MLA_RANK)),
                  pl.BlockSpec((tm, LANE), lambda i: (i, EV_KROPE // LANE)),
                  pl.BlockSpec((1, MLA_RANK), fixed),
                  pl.BlockSpec((1, MLA_RANK), fixed),
                  pl.BlockSpec((MLA_RANK, MLA_HEADS * MLA_QK), fixed),
                  pl.BlockSpec((MLA_RANK, MLA_HEADS * (MLA_NOPE + MLA_V)), fixed),
                  pl.BlockSpec((tm, LANE), row),
                  pl.BlockSpec((tm, LANE), row)],
        out_specs=[pl.BlockSpec((tm, MLA_HEADS * MLA_QK), row),
                   pl.BlockSpec((tm, MLA_HEADS * MLA_QK), row),
                   pl.BlockSpec((tm, MLA_HEADS * MLA_V), row)],
        out_shape=[jax.ShapeDtypeStruct((TOKENS, MLA_HEADS * MLA_QK), BF16),
                   jax.ShapeDtypeStruct((TOKENS, MLA_HEADS * MLA_QK), BF16),
                   jax.ShapeDtypeStruct((TOKENS, MLA_HEADS * MLA_V), BF16)],
        compiler_params=_cparams(("parallel",)),
        name="mla_proj",
    )(proj, proj, proj, gq.reshape(1, MLA_RANK), gkv.reshape(1, MLA_RANK), wq, wkv, ct, st)


def _causal_mask(tq, tk):
    row = lax.broadcasted_iota(jnp.int32, (tq, tk), 0)
    col = lax.broadcasted_iota(jnp.int32, (tq, tk), 1)
    return col <= row


def _lane_tile(x, width):
    return jnp.concatenate([x] * (width // LANE), axis=1)


def _with_ones(v):
    return jnp.concatenate([v, jnp.ones_like(v)], axis=1)


def _softmax_step(s, v_ext, m_ref, l_ref, acc_ref, rows):
    m_prev = m_ref[rows, :]
    m_new = jnp.maximum(m_prev, jnp.max(s, axis=1, keepdims=True))
    p = jnp.exp2(s - _lane_tile(m_new, s.shape[1]))
    alpha = jnp.exp2(m_prev - m_new)
    pv = jnp.dot(p.astype(BF16), v_ext, preferred_element_type=F32)
    dv = acc_ref.shape[1]
    acc_ref[rows, :] = alpha * acc_ref[rows, :] + pv[:, :dv]
    l_ref[rows, :] = alpha * l_ref[rows, :] + pv[:, dv:]
    m_ref[rows, :] = m_new


def _causal_rows(q, k_ref, v_ref, c, rc, bias_d=None, bias_p=None):
    lo = c * rc
    s_d = _nt_dot(q, k_ref[lo:lo + rc, :])
    if bias_d is not None:
        s_d = s_d - bias_d
    s_d = jnp.where(_causal_mask(rc, rc), s_d, NEG_BIG)
    m = jnp.max(s_d, axis=1, keepdims=True)
    if c:
        s_p = _nt_dot(q, k_ref[:lo, :])
        if bias_p is not None:
            s_p = s_p - bias_p
        m = jnp.maximum(m, jnp.max(s_p, axis=1, keepdims=True))
    m = jnp.broadcast_to(m, (rc, LANE))
    pv = jnp.dot(jnp.exp2(s_d - _lane_tile(m, rc)).astype(BF16),
                 _with_ones(v_ref[lo:lo + rc, :]), preferred_element_type=F32)
    if c:
        pv = pv + jnp.dot(jnp.exp2(s_p - _lane_tile(m, lo)).astype(BF16),
                          _with_ones(v_ref[:lo, :]), preferred_element_type=F32)
    return pv


def _mla_attn_kernel(q_ref, k_ref, v_ref, o_ref, *, rc):
    for c in range(SEQ // rc):
        rows = slice(c * rc, (c + 1) * rc)
        pv = _causal_rows(q_ref[rows, :], k_ref, v_ref, c, rc)
        o_ref[rows, :] = (pv[:, :MLA_V] / pv[:, MLA_V:]).astype(BF16)


def mla_attention(q, k, v, *, rc=512):
    return pl.pallas_call(
        functools.partial(_mla_attn_kernel, rc=rc),
        grid=(BATCH, MLA_HEADS),
        in_specs=[pl.BlockSpec((SEQ, MLA_QK), lambda b, h: (b, h)),
                  pl.BlockSpec((SEQ, MLA_QK), lambda b, h: (b, h)),
                  pl.BlockSpec((SEQ, MLA_V), lambda b, h: (b, h))],
        out_specs=pl.BlockSpec((SEQ, MLA_V), lambda b, h: (b, h)),
        out_shape=jax.ShapeDtypeStruct((TOKENS, MLA_HEADS * MLA_V), BF16),
        compiler_params=_cparams(("parallel", "parallel")),
        name="mla_attention",
    )(q, k, v)


def _split3(x):
    hi = x.astype(BF16)
    r1 = x - hi.astype(F32)
    mid = r1.astype(BF16)
    lo = (r1 - mid.astype(F32)).astype(BF16)
    return hi, mid, lo


def _gla_kernel(q_ref, k_ref, v_ref, go_ref, glr_ref, wg_ref, bg_ref, gn_ref, o_ref, st_ref,
                *, chunk):
    @pl.when(pl.program_id(1) == 0)
    def _():
        st_ref[...] = jnp.zeros_like(st_ref)

    z = jnp.dot(glr_ref[...], wg_ref[...], preferred_element_type=F32) + bg_ref[...]
    log_a = -(jnp.maximum(-z, 0.0) + jnp.log1p(jnp.exp(-jnp.abs(z)))) / GLA_GATE_NORM
    mask = _causal_mask(chunk, chunk)
    tri = mask.astype(BF16)
    hi, mid, lo = _split3(log_a)
    cum_all = (jnp.dot(tri, hi, preferred_element_type=F32)
               + jnp.dot(tri, mid, preferred_element_type=F32)
               + jnp.dot(tri, lo, preferred_element_type=F32))

    for h in range(GLA_HEADS):
        ks = slice(h * GLA_DK, (h + 1) * GLA_DK)
        vs = slice(h * GLA_DV, (h + 1) * GLA_DV)
        cum = cum_all[:, ks]
        last = cum[chunk - 1:chunk, :]
        q = q_ref[:, ks].astype(F32) * GLA_DK ** -0.5
        k = k_ref[:, ks].astype(F32)
        v = v_ref[:, vs]
        st = st_ref[h]
        q_in = (q * jnp.exp(cum)).astype(BF16)
        q_hat = (q * jnp.exp(cum - last)).astype(BF16)
        k_hat = (k * jnp.exp(last - cum)).astype(BF16)

        o = _nt_dot(q_in, st.astype(BF16))
        attn = jnp.where(mask, _nt_dot(q_hat, k_hat), 0.0)
        o = o + jnp.dot(attn.astype(BF16), v, preferred_element_type=F32)
        vt = v.astype(F32).T.astype(BF16)
        st_ref[h] = st * jnp.exp(last) + jnp.dot(vt, k_hat, preferred_element_type=F32)

        o = _rms(o, gn_ref[...])
        g = go_ref[:, vs].astype(F32)
        o_ref[:, vs] = (o * (g * jax.nn.sigmoid(g))).astype(BF16)


def gla_mixer(proj, wg, bg, gn, *, chunk=GLA_CHUNK):
    nc = SEQ // chunk
    kw, vw = GLA_HEADS * GLA_DK, GLA_HEADS * GLA_DV
    row = lambda b, c: b * nc + c
    return pl.pallas_call(
        functools.partial(_gla_kernel, chunk=chunk),
        grid=(BATCH, nc),
        in_specs=[pl.BlockSpec((chunk, kw), lambda b, c: (row(b, c), EV_GQ // kw)),
                  pl.BlockSpec((chunk, kw), lambda b, c: (row(b, c), EV_GK // kw)),
                  pl.BlockSpec((chunk, vw), lambda b, c: (row(b, c), EV_GV // vw)),
                  pl.BlockSpec((chunk, vw), lambda b, c: (row(b, c), EV_GOUT // vw)),
                  pl.BlockSpec((chunk, LANE), lambda b, c: (row(b, c), EV_GLR // LANE)),
                  pl.BlockSpec((LANE, kw), lambda b, c: (0, 0)),
                  pl.BlockSpec((1, kw), lambda b, c: (0, 0)),
                  pl.BlockSpec((1, GLA_DV), lambda b, c: (0, 0))],
        out_specs=pl.BlockSpec((chunk, vw), lambda b, c: (row(b, c), 0)),
        out_shape=jax.ShapeDtypeStruct((TOKENS, vw), BF16),
        scratch_shapes=[pltpu.VMEM((GLA_HEADS, GLA_DV, GLA_DK), F32)],
        compiler_params=_cparams(("parallel", "arbitrary")),
        name="gla_mixer",
    )(proj, proj, proj, proj, proj, wg, bg, gn.reshape(1, GLA_DV))


def _out_proj_kernel(r_ref, a1_ref, a2_ref, w1_ref, w2_ref, o_ref):
    o_ref[...] = (r_ref[...]
                  + jnp.dot(a1_ref[...], w1_ref[...], preferred_element_type=F32)
                  + jnp.dot(a2_ref[...], w2_ref[...], preferred_element_type=F32))


def out_proj_residual(r, a1, a2, w, *, tm=512):
    half = a1.shape[1]
    n = w.shape[1]
    return pl.pallas_call(
        _out_proj_kernel,
        grid=(TOKENS // tm,),
        in_specs=[pl.BlockSpec((tm, n), lambda i: (i, 0)),
                  pl.BlockSpec((tm, half), lambda i: (i, 0)),
                  pl.BlockSpec((tm, half), lambda i: (i, 0)),
                  pl.BlockSpec((half, n), lambda i: (0, 0)),
                  pl.BlockSpec((half, n), lambda i: (1, 0))],
        out_specs=pl.BlockSpec((tm, n), lambda i: (i, 0)),
        out_shape=jax.ShapeDtypeStruct((TOKENS, n), F32),
        compiler_params=_cparams(("parallel",)),
        name="out_proj_residual",
    )(r, a1, a2, w, w)


def _ffn_kernel(h_ref, g_ref, wg_ref, wu_ref, wd_ref, fg_ref, o_ref, xn_ref, *, final_norm):
    f = pl.program_id(1)

    @pl.when(f == 0)
    def _():
        x = h_ref[...]
        xn_ref[...] = _rms(x, g_ref[...]).astype(BF16)
        o_ref[...] = x

    xn = xn_ref[...]
    a = jnp.dot(xn, wg_ref[0], preferred_element_type=F32)
    b = jnp.dot(xn, wu_ref[0], preferred_element_type=F32)
    act = (a * jax.nn.sigmoid(a) * b).astype(BF16)
    o_ref[...] += jnp.dot(act, wd_ref[0], preferred_element_type=F32)

    if final_norm:
        @pl.when(f == pl.num_programs(1) - 1)
        def _():
            o_ref[...] = _rms(o_ref[...], fg_ref[...])


def ffn_residual(h, g, wg, wu, wd, fg, layer, *, final_norm, tm=512, tf=512):
    d = h.shape[1]
    hidden = wg.shape[2]
    return pl.pallas_call(
        functools.partial(_ffn_kernel, final_norm=final_norm),
        grid=(TOKENS // tm, hidden // tf),
        in_specs=[pl.BlockSpec((tm, d), lambda i, f: (i, 0)),
                  pl.BlockSpec((1, d), lambda i, f: (0, 0)),
                  pl.BlockSpec((1, d, tf), lambda i, f: (layer, 0, f)),
                  pl.BlockSpec((1, d, tf), lambda i, f: (layer, 0, f)),
                  pl.BlockSpec((1, tf, d), lambda i, f: (layer, f, 0)),
                  pl.BlockSpec((1, d), lambda i, f: (0, 0))],
        out_specs=pl.BlockSpec((tm, d), lambda i, f: (i, 0)),
        out_shape=jax.ShapeDtypeStruct((TOKENS, d), F32),
        scratch_shapes=[pltpu.VMEM((tm, d), BF16)],
        compiler_params=_cparams(("parallel", "arbitrary")),
        name="ffn_residual",
    )(h, g.reshape(1, d), wg, wu, wd, fg.reshape(1, d))


def _s5_kernel(u_ref, wb_ref, a_ref, wc_ref, d_ref, z_ref, bu_ref, xb_ref, st_ref, *, ts, nl):
    @pl.when(pl.program_id(1) == 0)
    def _():
        st_ref[...] = jnp.zeros_like(st_ref)

    blocks = range(nl)
    for l in blocks:
        bu_ref[l] = jnp.dot(u_ref[:, l * LANE:(l + 1) * LANE], wb_ref[l],
                            preferred_element_type=F32)
    coef = [(a_ref[l, 0:1, :], a_ref[l, 1:2, :]) for l in blocks]

    def two_steps(i, carry):
        r0 = pl.ds(pl.multiple_of(i * 2 * BATCH, 2 * BATCH), BATCH)
        r1 = pl.ds(pl.multiple_of(i * 2 * BATCH + BATCH, BATCH), BATCH)
        both = pl.ds(pl.multiple_of(i * 2 * BATCH, 2 * BATCH), 2 * BATCH)
        out = []
        for l in blocks:
            (xr, xi), (ar, ai) = carry[l], coef[l]
            xr1 = ar * xr - ai * xi + bu_ref[l, r0, :S5_SL]
            xi1 = ar * xi + ai * xr + bu_ref[l, r0, S5_SL:]
            xr2 = ar * xr1 - ai * xi1 + bu_ref[l, r1, :S5_SL]
            xi2 = ar * xi1 + ai * xr1 + bu_ref[l, r1, S5_SL:]
            xb_ref[l, both, :S5_SL] = jnp.concatenate([xr1, xr2], axis=0).astype(BF16)
            xb_ref[l, both, S5_SL:] = jnp.concatenate([xi1, xi2], axis=0).astype(BF16)
            out.append((xr2, xi2))
        return tuple(out)

    init = tuple((st_ref[l, :, :S5_SL], st_ref[l, :, S5_SL:]) for l in blocks)
    final = lax.fori_loop(0, ts // 2, two_steps, init, unroll=4)
    for l in blocks:
        st_ref[l, :, :S5_SL] = final[l][0]
        st_ref[l, :, S5_SL:] = final[l][1]
        cols = slice(l * LANE, (l + 1) * LANE)
        y = (jnp.dot(xb_ref[l], wc_ref[l], preferred_element_type=F32)
             + d_ref[l] * u_ref[:, cols].astype(F32))
        z_ref[:, cols] = jax.nn.gelu(y, approximate=True).astype(BF16)


def s5_scan(u_tm, ops, *, ts=256, nl=2):
    wb, a, wc, d = ops
    rows = ts * BATCH
    j3 = lambda j, t: (j, 0, 0)
    return pl.pallas_call(
        functools.partial(_s5_kernel, ts=ts, nl=nl),
        grid=(S5_LANE_BLOCKS // nl, SEQ // ts),
        in_specs=[pl.BlockSpec((rows, nl * LANE), lambda j, t: (t, j)),
                  pl.BlockSpec((nl, LANE, 2 * S5_SL), j3),
                  pl.BlockSpec((nl, 2, S5_SL), j3),
                  pl.BlockSpec((nl, 2 * S5_SL, LANE), j3),
                  pl.BlockSpec((nl, 1, LANE), j3)],
        out_specs=pl.BlockSpec((rows, nl * LANE), lambda j, t: (t, j)),
        out_shape=jax.ShapeDtypeStruct((SEQ * BATCH, S5_WIDTH), BF16),
        scratch_shapes=[pltpu.VMEM((nl, rows, 2 * S5_SL), F32),
                        pltpu.VMEM((nl, rows, 2 * S5_SL), BF16),
                        pltpu.VMEM((nl, BATCH, 2 * S5_SL), F32)],
        compiler_params=_cparams(("parallel", "arbitrary")),
        name="s5_scan",
    )(u_tm, wb, a, wc, d)


def _s5_operators(a_re, a_im, log_dt, b_re, b_im, c_re, c_im, d_skip):
    dt = jnp.exp(log_dt)[:, None]
    lr, li = a_re, a_im
    mag = jnp.exp(lr * dt)
    ar, ai = mag * jnp.cos(li * dt), mag * jnp.sin(li * dt)
    den = lr * lr + li * li
    zr, zi = ar - 1.0, ai
    fr = (zr * lr + zi * li) / den
    fi = (zi * lr - zr * li) / den
    bbr = fr[..., None] * b_re - fi[..., None] * b_im
    bbi = fr[..., None] * b_im + fi[..., None] * b_re
    nb, gb = S5_LANE_BLOCKS, S5_GB
    eye = jnp.eye(gb, dtype=F32)

    def in_map(bb):
        return jnp.einsum('jgnp,gh->jgphn', bb.reshape(nb, gb, S5_N, S5_P), eye).reshape(
            nb, LANE, S5_SL)

    def out_map(cc):
        return jnp.einsum('jgpn,gh->jhngp', cc.reshape(nb, gb, S5_P, S5_N), eye).reshape(
            nb, S5_SL, LANE)

    wb = jnp.concatenate([in_map(bbr), in_map(bbi)], axis=2).astype(BF16)
    wc = jnp.concatenate([out_map(c_re), out_map(-c_im)], axis=1).astype(BF16)
    a = jnp.stack([ar.reshape(nb, S5_SL), ai.reshape(nb, S5_SL)], axis=1)
    return wb, a, wc, d_skip.reshape(nb, 1, LANE)


def _glu_kernel(z_ref, zc_ref, w_ref, b_ref, o_ref):
    gate = jnp.dot(z_ref[...], w_ref[...], preferred_element_type=F32) + b_ref[...]
    o_ref[...] = (zc_ref[...].astype(F32) * jax.nn.sigmoid(gate)).astype(BF16)


def s5_glu(z_tm, w, b, *, tm=1024, tn=512):
    n = w.shape[1]
    nt = SEQ // tm
    nj = n // tn
    z2 = z_tm.reshape(SEQ, BATCH * n)
    return pl.pallas_call(
        _glu_kernel,
        grid=(BATCH, nt, nj),
        in_specs=[pl.BlockSpec((tm, n), lambda b, i, j: (i, b)),
                  pl.BlockSpec((tm, tn), lambda b, i, j: (i, b * nj + j)),
                  pl.BlockSpec((n, tn), lambda b, i, j: (0, j)),
                  pl.BlockSpec((1, tn), lambda b, i, j: (0, j))],
        out_specs=pl.BlockSpec((tm, tn), lambda b, i, j: (b * nt + i, j)),
        out_shape=jax.ShapeDtypeStruct((TOKENS, n), BF16),
        compiler_params=_cparams(("parallel", "parallel", "arbitrary")),
        name="s5_glu",
    )(z2, z2, w, b.reshape(1, n))


def _diff_attn_kernel(scal_ref, q_ref, k_ref, v_ref, pq_ref, pk_ref, gn_ref, o_ref,
                      *, rc, out_scale):
    h = pl.program_id(1)
    slope = scal_ref[h]
    lam = scal_ref[DIFF_HEADS]
    lane = lax.broadcasted_iota(jnp.int32, (rc, 2 * DIFF_DQK), 1)
    for c in range(SEQ // rc):
        lo = c * rc
        rows = slice(lo, lo + rc)
        q = q_ref[rows, :].astype(F32) * (DIFF_DQK ** -0.5 * LOG2_E)
        qa = jnp.where(lane < DIFF_DQK, q, 0.0).astype(BF16)
        qb = jnp.where(lane >= DIFF_DQK, q, 0.0).astype(BF16)
        pq = jnp.broadcast_to(pq_ref[rows, :], (rc, LANE))
        bias_d = slope * jnp.abs(_lane_tile(pq, rc) - pk_ref[0, :, lo:lo + rc])
        bias_p = slope * jnp.abs(_lane_tile(pq, lo) - pk_ref[0, :, :lo]) if c else None
        pva = _causal_rows(qa, k_ref, v_ref, c, rc, bias_d, bias_p)
        pvb = _causal_rows(qb, k_ref, v_ref, c, rc, bias_d, bias_p)
        o = (pva[:, :DIFF_DV] / pva[:, DIFF_DV:]
             - lam * (pvb[:, :DIFF_DV] / pvb[:, DIFF_DV:]))
        o_ref[rows, :] = (_rms(o, gn_ref[...]) * out_scale).astype(BF16)


def diff_attention(scal, proj, posq, posk, gn, *, out_scale, rc=512):
    hb = lambda base: (lambda b, h, s: (b, base // DIFF_DV + h))
    return pl.pallas_call(
        functools.partial(_diff_attn_kernel, rc=rc, out_scale=out_scale),
        grid_spec=pltpu.PrefetchScalarGridSpec(
            num_scalar_prefetch=1,
            grid=(BATCH, DIFF_HEADS),
            in_specs=[pl.BlockSpec((SEQ, DIFF_DV), hb(OD_Q)),
                      pl.BlockSpec((SEQ, DIFF_DV), hb(OD_K)),
                      pl.BlockSpec((SEQ, DIFF_DV), hb(OD_V)),
                      pl.BlockSpec((SEQ, 1), lambda b, h, s: (b, 0)),
                      pl.BlockSpec((1, 1, SEQ), lambda b, h, s: (b, 0, 0)),
                      pl.BlockSpec((1, DIFF_DV), lambda b, h, s: (0, 0))],
            out_specs=pl.BlockSpec((SEQ, DIFF_DV), lambda b, h, s: (b, h))),
        out_shape=jax.ShapeDtypeStruct((TOKENS, DIFF_HEADS * DIFF_DV), BF16),
        compiler_params=_cparams(("parallel", "parallel")),
        name="diff_attention",
    )(scal, proj, proj, proj, posq, posk, gn.reshape(1, DIFF_DV))


OD_Q = 0
OD_K = 1024
OD_V = 2048


def _pack_even_w_in(w):
    cq, ckv, kr, gq, gk, gv, glr, gout = jnp.split(
        w, [512, 1024, 1088, 1600, 2112, 3136, 3152], axis=1)
    half = MLA_ROPE // 2
    kr_dup = jnp.concatenate([kr, kr[:, half:], kr[:, :half]], axis=1)
    glr_pad = jnp.pad(glr, ((0, 0), (0, EV_COLS - EV_GLR - GLA_GATE_RANK)))
    return jnp.concatenate([cq, ckv, gq, gk, gv, gout, kr_dup, glr_pad], axis=1).astype(BF16)


def _pack_w_uq(w):
    w = w.reshape(MLA_RANK, MLA_HEADS, MLA_NOPE + MLA_ROPE)
    half = MLA_ROPE // 2
    t1 = w[..., MLA_NOPE:MLA_NOPE + half]
    t2 = w[..., MLA_NOPE + half:]
    w = jnp.concatenate([w[..., :MLA_NOPE], t1, t2, t2, t1], axis=-1)
    return w.reshape(MLA_RANK, MLA_HEADS * MLA_QK).astype(BF16)


def _rope_tables(positions):
    half = MLA_ROPE // 2
    inv_freq = ROPE_THETA ** (-jnp.arange(half, dtype=F32) / half)
    ang = positions.astype(F32).reshape(TOKENS, 1) * inv_freq
    cos, sin = jnp.cos(ang), jnp.sin(ang)
    zeros = jnp.zeros((TOKENS, 2 * half), F32)
    return (jnp.concatenate([cos, cos, zeros], axis=1),
            jnp.concatenate([-sin, sin, zeros], axis=1))


def _even_mixer(h, g_mix, positions, w_in, q_norm, w_uq, kv_norm, w_ukv, w_gate_up, b_gate,
                g_norm, w_out):
    proj = norm_matmul(h, g_mix, _pack_even_w_in(w_in), tm=1024, tn=1536)
    ct, st = _rope_tables(positions)
    q, k, v = mla_proj(proj, q_norm, kv_norm, _pack_w_uq(w_uq), w_ukv.astype(BF16), ct, st)
    o_mla = mla_attention(q, k, v)
    wg = jnp.pad(w_gate_up, ((0, LANE - GLA_GATE_RANK), (0, 0))).astype(BF16)
    o_gla = gla_mixer(proj, wg, b_gate.reshape(1, GLA_HEADS * GLA_DK), g_norm)
    return out_proj_residual(h, o_mla, o_gla, w_out.astype(BF16))


def _odd_mixer(h, g_mix, positions, layer, w_in, a_re, a_im, log_dt, b_re, b_im, c_re, c_im,
               d_skip, w_glu, b_glu, lq1, lk1, lq2, lk2, d_norm, w_out):
    lambda_init = 0.8 - 0.6 * math.exp(-0.3 * layer)
    u_tm, proj = odd_in_proj(h, g_mix, w_in.astype(BF16))
    z_tm = s5_scan(u_tm, _s5_operators(a_re, a_im, log_dt, b_re, b_im, c_re, c_im, d_skip))
    o_s5 = s5_glu(z_tm, w_glu.astype(BF16), b_glu)
    lam = (jnp.exp(jnp.sum(lq1 * lk1)) - jnp.exp(jnp.sum(lq2 * lk2)) + lambda_init)
    slopes = jnp.exp2(-8.0 * jnp.arange(1, DIFF_HEADS + 1, dtype=F32) / DIFF_HEADS)
    scal = jnp.concatenate([slopes * LOG2_E, lam.reshape(1)]).astype(F32)
    posf = positions.astype(F32)
    o_diff = diff_attention(scal, proj, posf.reshape(TOKENS, 1), posf.reshape(BATCH, 1, SEQ),
                            d_norm, out_scale=1.0 - lambda_init)
    return out_proj_residual(h, o_s5, o_diff, w_out.astype(BF16))


def kernel(x, positions, norm_mix, norm_ffn, final_norm, ffn_w_gate, ffn_w_up, ffn_w_down, ag_w_in, mla_q_norm, mla_w_uq, mla_kv_norm, mla_w_ukv, gla_w_gate_up, gla_b_gate, gla_norm, ag_w_out, cd_w_in, s5_a_re, s5_a_im, s5_log_dt, s5_b_re, s5_b_im, s5_c_re, s5_c_im, s5_d, s5_w_glu, s5_b_glu, diff_lambda_q1, diff_lambda_k1, diff_lambda_q2, diff_lambda_k2, diff_norm, cd_w_out):
    h = x.reshape(TOKENS, D_MODEL)
    w_gate, w_up, w_down = (w.astype(BF16) for w in (ffn_w_gate, ffn_w_up, ffn_w_down))
    for layer in range(DEPTH):
        i = layer // 2
        if layer % 2 == 0:
            h = _even_mixer(h, norm_mix[layer], positions, ag_w_in[i], mla_q_norm[i],
                            mla_w_uq[i], mla_kv_norm[i], mla_w_ukv[i], gla_w_gate_up[i],
                            gla_b_gate[i], gla_norm[i], ag_w_out[i])
        else:
            h = _odd_mixer(h, norm_mix[layer], positions, layer, cd_w_in[i], s5_a_re[i],
                           s5_a_im[i], s5_log_dt[i], s5_b_re[i], s5_b_im[i], s5_c_re[i],
                           s5_c_im[i], s5_d[i], s5_w_glu[i], s5_b_glu[i], diff_lambda_q1[i],
                           diff_lambda_k1[i], diff_lambda_q2[i], diff_lambda_k2[i],
                           diff_norm[i], cd_w_out[i])
        h = ffn_residual(h, norm_ffn[layer], w_gate, w_up, w_down, final_norm, layer,
                         final_norm=(layer == DEPTH - 1))
    return h.reshape(BATCH, SEQ, D_MODEL)
```

```python
import functools
import math
from typing import Callable, NamedTuple

import jax
import jax.numpy as jnp
from jax import lax
from jax.experimental import pallas as pl
from jax.experimental.pallas import tpu as pltpu

F32 = jnp.float32
BF16 = jnp.bfloat16

D_MODEL = 2048
BATCH = 8
SEQ = 2048
DEPTH = 2
TOKENS = BATCH * SEQ
MIX_HALF = D_MODEL // 2
RMS_EPS = 1e-6

MLA_NOPE = 128
MLA_ROPE = 64
MLA_V = 128
MLA_HEADS = 8
MLA_RANK = 512
ROPE_THETA = 10000.0
MLA_QK = 256

GLA_HEADS = 4
GLA_DK = 128
GLA_DV = 256
GLA_GATE_RANK = 16
GLA_GATE_NORM = 16.0
GLA_CHUNK = 128

S5_WIDTH = MIX_HALF
S5_P = 16
S5_GROUPS = 64
S5_N = 64
S5_GB = 8
S5_LANE_BLOCKS = S5_GROUPS // S5_GB
S5_SL = S5_GB * S5_N

DIFF_DQK = 64
DIFF_DV = 128
DIFF_HEADS = 8

FFN_HIDDEN = 5632

LANE = 128
NEG_BIG = -1e30
LOG2_E = math.log2(math.e)

EV_CQ = 0
EV_CKV = 512
EV_GQ = 1024
EV_GK = 1536
EV_GV = 2048
EV_GOUT = 3072
EV_KROPE = 4096
EV_GLR = 4224
EV_COLS = 4608


def _cparams(sem, vmem_mb=48):
    return pltpu.CompilerParams(dimension_semantics=sem,
                                vmem_limit_bytes=vmem_mb * 1024 * 1024)


class _CastJob(NamedTuple):
    w: jax.Array
    layer: int
    block: tuple
    index: Callable


def _call_with_casts(body, casts, args, *, in_specs, out_specs, out_shape, **kwargs):
    n_in, n_out, n_cast = len(in_specs), len(out_specs), len(casts)

    def kernel(*refs):
        ins, rest = refs[:n_in], refs[n_in:]
        cast_in, rest = rest[:n_cast], rest[n_cast:]
        outs, rest = rest[:n_out], rest[n_out:]
        cast_out, scratch = rest[:n_cast], rest[n_cast:]
        for src, dst in zip(cast_in, cast_out):
            dst[...] = src[0].astype(BF16)
        body(*ins, *outs, *scratch)

    cast_in_specs = [pl.BlockSpec((1,) + c.block, lambda *g, c=c: (c.layer,) + tuple(c.index(*g)))
                     for c in casts]
    cast_out_specs = [pl.BlockSpec(c.block, lambda *g, c=c: tuple(c.index(*g))) for c in casts]
    cast_shapes = [jax.ShapeDtypeStruct(c.w.shape[1:], BF16) for c in casts]
    res = pl.pallas_call(kernel, in_specs=list(in_specs) + cast_in_specs,
                         out_specs=list(out_specs) + cast_out_specs,
                         out_shape=list(out_shape) + cast_shapes, **kwargs)(
                             *args, *[c.w for c in casts])
    return res[:n_out], res[n_out:]


def _rms(x, g):
    ms = jnp.mean(x * x, axis=-1, keepdims=True)
    return x * lax.rsqrt(ms + RMS_EPS) * g


def _nt_dot(a, b):
    return lax.dot_general(a, b, (((1,), (1,)), ((), ())), preferred_element_type=F32)


def _norm_matmul_kernel(x_ref, g_ref, w_ref, o_ref, xn_ref):
    @pl.when(pl.program_id(1) == 0)
    def _():
        xn_ref[...] = _rms(x_ref[...].astype(F32), g_ref[...]).astype(BF16)

    o_ref[...] = jnp.dot(xn_ref[...], w_ref[...],
                         preferred_element_type=F32).astype(o_ref.dtype)


def norm_matmul(x, g, w, *, tm=512, tn=512, out_dtype=BF16):
    t, k = x.shape
    n = w.shape[1]
    return pl.pallas_call(
        _norm_matmul_kernel,
        grid=(t // tm, n // tn),
        in_specs=[pl.BlockSpec((tm, k), lambda i, j: (i, 0)),
                  pl.BlockSpec((1, k), lambda i, j: (0, 0)),
                  pl.BlockSpec((k, tn), lambda i, j: (0, j))],
        out_specs=pl.BlockSpec((tm, tn), lambda i, j: (i, j)),
        out_shape=jax.ShapeDtypeStruct((t, n), out_dtype),
        scratch_shapes=[pltpu.VMEM((tm, k), BF16)],
        compiler_params=_cparams(("parallel", "arbitrary")),
        name="norm_matmul",
    )(x, g.reshape(1, k), w)


def _odd_in_proj_kernel(x_ref, g_ref, w_ref, u_ref, qkv_ref, xn_ref):
    j = pl.program_id(1)

    @pl.when(j == 0)
    def _():
        xn_ref[...] = _rms(x_ref[...], g_ref[...]).astype(BF16)

    y = jnp.dot(xn_ref[...], w_ref[...], preferred_element_type=F32).astype(BF16)

    @pl.when(j == 0)
    def _():
        u_ref[...] = y

    @pl.when(j > 0)
    def _():
        qkv_ref[...] = y


def odd_in_proj(x, g, w, *, tm=1024):
    tn = S5_WIDTH
    nt = SEQ // tm
    n_att = w.shape[1] - S5_WIDTH
    u, qkv = pl.pallas_call(
        _odd_in_proj_kernel,
        grid=(TOKENS // tm, w.shape[1] // tn),
        in_specs=[pl.BlockSpec((tm, D_MODEL), lambda i, j: (i, 0)),
                  pl.BlockSpec((1, D_MODEL), lambda i, j: (0, 0)),
                  pl.BlockSpec((D_MODEL, tn), lambda i, j: (0, j))],
        out_specs=[pl.BlockSpec((tm, tn), lambda i, j: (i % nt, i // nt)),
                   pl.BlockSpec((tm, tn), lambda i, j: (i, jnp.maximum(j - 1, 0)))],
        out_shape=[jax.ShapeDtypeStruct((SEQ, BATCH * S5_WIDTH), BF16),
                   jax.ShapeDtypeStruct((TOKENS, n_att), BF16)],
        scratch_shapes=[pltpu.VMEM((tm, D_MODEL), BF16)],
        compiler_params=_cparams(("parallel", "arbitrary")),
        name="odd_in_proj",
    )(x, g.reshape(1, D_MODEL), w)
    return u.reshape(SEQ * BATCH, S5_WIDTH), qkv


def _rope_half(blk, ct, st):
    return blk * ct + pltpu.roll(blk, 64, 1) * st


def _mla_proj_kernel(cq_ref, ckv_ref, kr_ref, gq_ref, gkv_ref, wq_ref, wkv_ref, ct_ref, st_ref,
                     q_ref, k_ref, v_ref, *, scale):
    ct, st = ct_ref[...], st_ref[...]
    xq = _rms(cq_ref[...].astype(F32), gq_ref[...]).astype(BF16)
    xkv = _rms(ckv_ref[...].astype(F32), gkv_ref[...]).astype(BF16)
    kr = _rope_half(kr_ref[...].astype(F32), ct, st).astype(BF16)
    for h in range(MLA_HEADS):
        lo, mid, hi = h * MLA_QK, h * MLA_QK + LANE, (h + 1) * MLA_QK
        a = jnp.dot(xq, wq_ref[:, lo:hi], preferred_element_type=F32)
        q_ref[:, lo:mid] = (a[:, :LANE] * scale).astype(BF16)
        q_ref[:, mid:hi] = (_rope_half(a[:, LANE:], ct, st) * scale).astype(BF16)
        kv = jnp.dot(xkv, wkv_ref[:, lo:hi], preferred_element_type=F32)
        k_ref[:, lo:mid] = kv[:, :LANE].astype(BF16)
        k_ref[:, mid:hi] = kr
        v_ref[:, h * MLA_V:(h + 1) * MLA_V] = kv[:, LANE:].astype(BF16)


def mla_proj(proj, gq, gkv, wq, wkv, ct, st, *, tm=512):
    scale = (MLA_NOPE + MLA_ROPE) ** -0.5 * LOG2_E
    row = lambda i: (i, 0)
    fixed = lambda i: (0, 0)
    return pl.pallas_call(
        functools.partial(_mla_proj_kernel, scale=scale),
        grid=(TOKENS // tm,),
        in_specs=[pl.BlockSpec((tm, MLA_RANK), lambda i: (i, EV_CQ // MLA_RANK)),
                  pl.BlockSpec((tm, MLA_RANK), lambda i: (i, EV_CKV // MLA_RANK)),
                  pl.BlockSpec((tm, LANE), lambda i: (i, EV_KROPE // LANE)),
                  pl.BlockSpec((1, MLA_RANK), fixed),
                  pl.BlockSpec((1, MLA_RANK), fixed),
                  pl.BlockSpec((MLA_RANK, MLA_HEADS * MLA_QK), fixed),
                  pl.BlockSpec((MLA_RANK, MLA_HEADS * (MLA_NOPE + MLA_V)), fixed),
                  pl.BlockSpec((tm, LANE), row),
                  pl.BlockSpec((tm, LANE), row)],
        out_specs=[pl.BlockSpec((tm, MLA_HEADS * MLA_QK), row),
                   pl.BlockSpec((tm, MLA_HEADS * MLA_QK), row),
                   pl.BlockSpec((tm, MLA_HEADS * MLA_V), row)],
        out_shape=[jax.ShapeDtypeStruct((TOKENS, MLA_HEADS * MLA_QK), BF16),
                   jax.ShapeDtypeStruct((TOKENS, MLA_HEADS * MLA_QK), BF16),
                   jax.ShapeDtypeStruct((TOKENS, MLA_HEADS * MLA_V), BF16)],
        compiler_params=_cparams(("parallel",)),
        name="mla_proj",
    )(proj, proj, proj, gq.reshape(1, MLA_RANK), gkv.reshape(1, MLA_RANK), wq, wkv, ct, st)


def _causal_mask(tq, tk):
    row = lax.broadcasted_iota(jnp.int32, (tq, tk), 0)
    col = lax.broadcasted_iota(jnp.int32, (tq, tk), 1)
    return col <= row


def _lane_tile(x, width):
    return jnp.concatenate([x] * (width // LANE), axis=1)


def _with_ones(v):
    return jnp.concatenate([v, jnp.ones_like(v)], axis=1)


def _softmax_step(s, v_ext, m_ref, l_ref, acc_ref, rows):
    m_prev = m_ref[rows, :]
    m_new = jnp.maximum(m_prev, jnp.max(s, axis=1, keepdims=True))
    p = jnp.exp2(s - _lane_tile(m_new, s.shape[1]))
    alpha = jnp.exp2(m_prev - m_new)
    pv = jnp.dot(p.astype(BF16), v_ext, preferred_element_type=F32)
    dv = acc_ref.shape[1]
    acc_ref[rows, :] = alpha * acc_ref[rows, :] + pv[:, :dv]
    l_ref[rows, :] = alpha * l_ref[rows, :] + pv[:, dv:]
    m_ref[rows, :] = m_new


def _causal_rows(q, k_ref, v_ref, c, rc, bias_d=None, bias_p=None):
    lo = c * rc
    s_d = _nt_dot(q, k_ref[lo:lo + rc, :])
    if bias_d is not None:
        s_d = s_d - bias_d
    s_d = jnp.where(_causal_mask(rc, rc), s_d, NEG_BIG)
    m = jnp.max(s_d, axis=1, keepdims=True)
    if c:
        s_p = _nt_dot(q, k_ref[:lo, :])
        if bias_p is not None:
            s_p = s_p - bias_p
        m = jnp.maximum(m, jnp.max(s_p, axis=1, keepdims=True))
    m = jnp.broadcast_to(m, (rc, LANE))
    pv = jnp.dot(jnp.exp2(s_d - _lane_tile(m, rc)).astype(BF16),
                 _with_ones(v_ref[lo:lo + rc, :]), preferred_element_type=F32)
    if c:
        pv = pv + jnp.dot(jnp.exp2(s_p - _lane_tile(m, lo)).astype(BF16),
                          _with_ones(v_ref[:lo, :]), preferred_element_type=F32)
    return pv


def _mla_attn_kernel(q_ref, k_ref, v_ref, o_ref, *, rc):
    for c in range(SEQ // rc):
        rows = slice(c * rc, (c + 1) * rc)
        pv = _causal_rows(q_ref[rows, :], k_ref, v_ref, c, rc)
        o_ref[rows, :] = (pv[:, :MLA_V] / pv[:, MLA_V:]).astype(BF16)


def mla_attention(q, k, v, ffn_weights, *, rc=512):
    layer, gate, up, down = ffn_weights
    split = 4
    where = lambda b, h: ((b * MLA_HEADS + h) // split, (b * MLA_HEADS + h) % split)
    rows = BATCH * MLA_HEADS // split
    up_blk = (gate.shape[1] // rows, gate.shape[2] // split)
    down_blk = (down.shape[1] // rows, down.shape[2] // split)
    casts = [_CastJob(gate, layer, up_blk, where), _CastJob(up, layer, up_blk, where),
             _CastJob(down, layer, down_blk, where)]
    (out,), cast = _call_with_casts(
        functools.partial(_mla_attn_kernel, rc=rc), casts, (q, k, v),
        grid=(BATCH, MLA_HEADS),
        in_specs=[pl.BlockSpec((SEQ, MLA_QK), lambda b, h: (b, h)),
                  pl.BlockSpec((SEQ, MLA_QK), lambda b, h: (b, h)),
                  pl.BlockSpec((SEQ, MLA_V), lambda b, h: (b, h))],
        out_specs=[pl.BlockSpec((SEQ, MLA_V), lambda b, h: (b, h))],
        out_shape=[jax.ShapeDtypeStruct((TOKENS, MLA_HEADS * MLA_V), BF16)],
        compiler_params=_cparams(("parallel", "parallel")),
        name="mla_attention")
    return out, cast


def _split3(x):
    hi = x.astype(BF16)
    r1 = x - hi.astype(F32)
    mid = r1.astype(BF16)
    lo = (r1 - mid.astype(F32)).astype(BF16)
    return hi, mid, lo


def _gla_kernel(q_ref, k_ref, v_ref, go_ref, glr_ref, wg_ref, bg_ref, gn_ref, o_ref, st_ref,
                *, chunk):
    @pl.when(pl.program_id(1) == 0)
    def _():
        st_ref[...] = jnp.zeros_like(st_ref)

    mask = _causal_mask(chunk, chunk)
    tri = mask.astype(BF16)
    for b in range(q_ref.shape[0]):
        z = jnp.dot(glr_ref[b], wg_ref[...], preferred_element_type=F32) + bg_ref[...]
        log_a = -(jnp.maximum(-z, 0.0) + jnp.log1p(jnp.exp(-jnp.abs(z)))) / GLA_GATE_NORM
        hi, mid, lo = _split3(log_a)
        cum_all = (jnp.dot(tri, hi, preferred_element_type=F32)
                   + jnp.dot(tri, mid, preferred_element_type=F32)
                   + jnp.dot(tri, lo, preferred_element_type=F32))

        for h in range(GLA_HEADS):
            ks = slice(h * GLA_DK, (h + 1) * GLA_DK)
            vs = slice(h * GLA_DV, (h + 1) * GLA_DV)
            cum = cum_all[:, ks]
            last = cum[chunk - 1:chunk, :]
            q = q_ref[b, :, ks].astype(F32) * GLA_DK ** -0.5
            k = k_ref[b, :, ks].astype(F32)
            v = v_ref[b, :, vs]
            st = st_ref[b, h]
            q_in = (q * jnp.exp(cum)).astype(BF16)
            q_hat = (q * jnp.exp(cum - last)).astype(BF16)
            k_hat = (k * jnp.exp(last - cum)).astype(BF16)

            o = _nt_dot(q_in, st.astype(BF16))
            attn = jnp.where(mask, _nt_dot(q_hat, k_hat), 0.0)
            o = o + jnp.dot(attn.astype(BF16), v, preferred_element_type=F32)
            vt = v.astype(F32).T.astype(BF16)
            st_ref[b, h] = st * jnp.exp(last) + jnp.dot(vt, k_hat, preferred_element_type=F32)

            o = _rms(o, gn_ref[...])
            g = go_ref[b, :, vs].astype(F32)
            o_ref[b, :, vs] = (o * (g * jax.nn.sigmoid(g))).astype(BF16)


def gla_mixer(proj, wg, bg, gn, *, chunk=GLA_CHUNK, nb=2):
    kw, vw = GLA_HEADS * GLA_DK, GLA_HEADS * GLA_DV
    col = lambda base, width: (lambda b, c: (b, c, base // width))
    fixed = lambda b, c: (0, 0)
    return pl.pallas_call(
        functools.partial(_gla_kernel, chunk=chunk),
        grid=(BATCH // nb, SEQ // chunk),
        in_specs=[pl.BlockSpec((nb, chunk, kw), col(EV_GQ, kw)),
                  pl.BlockSpec((nb, chunk, kw), col(EV_GK, kw)),
                  pl.BlockSpec((nb, chunk, vw), col(EV_GV, vw)),
                  pl.BlockSpec((nb, chunk, vw), col(EV_GOUT, vw)),
                  pl.BlockSpec((nb, chunk, LANE), col(EV_GLR, LANE)),
                  pl.BlockSpec((LANE, kw), fixed),
                  pl.BlockSpec((1, kw), fixed),
                  pl.BlockSpec((1, GLA_DV), fixed)],
        out_specs=pl.BlockSpec((nb, chunk, vw), lambda b, c: (b, c, 0)),
        out_shape=jax.ShapeDtypeStruct((BATCH, SEQ, vw), BF16),
        scratch_shapes=[pltpu.VMEM((nb, GLA_HEADS, GLA_DV, GLA_DK), F32)],
        compiler_params=_cparams(("parallel", "arbitrary")),
        name="gla_mixer",
    )(proj, proj, proj, proj, proj, wg, bg, gn.reshape(1, GLA_DV))


def _out_proj_kernel(r_ref, a1_ref, a2_ref, w1_ref, w2_ref, o_ref):
    o_ref[...] = (r_ref[...]
                  + jnp.dot(a1_ref[...], w1_ref[...], preferred_element_type=F32)
                  + jnp.dot(a2_ref[...], w2_ref[...], preferred_element_type=F32))


def out_proj_residual(r, a1, a2, w, *, tm=512):
    half = a1.shape[1]
    n = w.shape[1]
    return pl.pallas_call(
        _out_proj_kernel,
        grid=(TOKENS // tm,),
        in_specs=[pl.BlockSpec((tm, n), lambda i: (i, 0)),
                  pl.BlockSpec((tm, half), lambda i: (i, 0)),
                  pl.BlockSpec((tm, half), lambda i: (i, 0)),
                  pl.BlockSpec((half, n), lambda i: (0, 0)),
                  pl.BlockSpec((half, n), lambda i: (1, 0))],
        out_specs=pl.BlockSpec((tm, n), lambda i: (i, 0)),
        out_shape=jax.ShapeDtypeStruct((TOKENS, n), F32),
        compiler_params=_cparams(("parallel",)),
        name="out_proj_residual",
    )(r, a1, a2, w, w)


def _ffn_kernel(h_ref, g_ref, wg_ref, wu_ref, wd_ref, fg_ref, o_ref, xn_ref, *, final_norm):
    f = pl.program_id(1)

    @pl.when(f == 0)
    def _():
        x = h_ref[...]
        xn_ref[...] = _rms(x, g_ref[...]).astype(BF16)
        o_ref[...] = x

    xn = xn_ref[...]
    a = jnp.dot(xn, wg_ref[...], preferred_element_type=F32)
    b = jnp.dot(xn, wu_ref[...], preferred_element_type=F32)
    act = (a * jax.nn.sigmoid(a) * b).astype(BF16)
    o_ref[...] += jnp.dot(act, wd_ref[...], preferred_element_type=F32)

    if final_norm:
        @pl.when(f == pl.num_programs(1) - 1)
        def _():
            o_ref[...] = _rms(o_ref[...], fg_ref[...])


def ffn_residual(h, g, wg, wu, wd, fg, *, final_norm, next_weights=None, tm=1024, tf=256):
    d = h.shape[1]
    hidden = wg.shape[1]
    n_rows, n_f = TOKENS // tm, hidden // tf
    casts = []
    if next_weights is not None:
        layer, ngate, nup, ndown = next_weights
        up_blk = (d // n_rows, tf)
        casts = [_CastJob(ngate, layer, up_blk, lambda i, f: (i, f)),
                 _CastJob(nup, layer, up_blk, lambda i, f: (i, f)),
                 _CastJob(ndown, layer, (tf, d // n_rows), lambda i, f: (f, i))]
    (out,), cast = _call_with_casts(
        functools.partial(_ffn_kernel, final_norm=final_norm), casts,
        (h, g.reshape(1, d), wg, wu, wd, fg.reshape(1, d)),
        grid=(n_rows, n_f),
        in_specs=[pl.BlockSpec((tm, d), lambda i, f: (i, 0)),
                  pl.BlockSpec((1, d), lambda i, f: (0, 0)),
                  pl.BlockSpec((d, tf), lambda i, f: (0, f)),
                  pl.BlockSpec((d, tf), lambda i, f: (0, f)),
                  pl.BlockSpec((tf, d), lambda i, f: (f, 0)),
                  pl.BlockSpec((1, d), lambda i, f: (0, 0))],
        out_specs=[pl.BlockSpec((tm, d), lambda i, f: (i, 0))],
        out_shape=[jax.ShapeDtypeStruct((TOKENS, d), F32)],
        scratch_shapes=[pltpu.VMEM((tm, d), BF16)],
        compiler_params=_cparams(("parallel", "arbitrary")),
        name="ffn_residual")
    return out, cast


def _s5_kernel(u_ref, wb_ref, a_ref, wc_ref, d_ref, z_ref, bu_ref, xb_ref, st_ref, *, ts, nl):
    @pl.when(pl.program_id(1) == 0)
    def _():
        st_ref[...] = jnp.zeros_like(st_ref)

    blocks = range(nl)
    for l in blocks:
        bu_ref[l] = jnp.dot(u_ref[:, l * LANE:(l + 1) * LANE], wb_ref[l],
                            preferred_element_type=F32)
    coef = [(a_ref[l, 0:1, :], a_ref[l, 1:2, :]) for l in blocks]

    def two_steps(i, carry):
        r0 = pl.ds(pl.multiple_of(i * 2 * BATCH, 2 * BATCH), BATCH)
        r1 = pl.ds(pl.multiple_of(i * 2 * BATCH + BATCH, BATCH), BATCH)
        both = pl.ds(pl.multiple_of(i * 2 * BATCH, 2 * BATCH), 2 * BATCH)
        out = []
        for l in blocks:
            (xr, xi), (ar, ai) = carry[l], coef[l]
            xr1 = ar * xr - ai * xi + bu_ref[l, r0, :S5_SL]
            xi1 = ar * xi + ai * xr + bu_ref[l, r0, S5_SL:]
            xr2 = ar * xr1 - ai * xi1 + bu_ref[l, r1, :S5_SL]
            xi2 = ar * xi1 + ai * xr1 + bu_ref[l, r1, S5_SL:]
            xb_ref[l, both, :S5_SL] = jnp.concatenate([xr1, xr2], axis=0).astype(BF16)
            xb_ref[l, both, S5_SL:] = jnp.concatenate([xi1, xi2], axis=0).astype(BF16)
            out.append((xr2, xi2))
        return tuple(out)

    init = tuple((st_ref[l, :, :S5_SL], st_ref[l, :, S5_SL:]) for l in blocks)
    final = lax.fori_loop(0, ts // 2, two_steps, init, unroll=4)
    for l in blocks:
        st_ref[l, :, :S5_SL] = final[l][0]
        st_ref[l, :, S5_SL:] = final[l][1]
        cols = slice(l * LANE, (l + 1) * LANE)
        y = (jnp.dot(xb_ref[l], wc_ref[l], preferred_element_type=F32)
             + d_ref[l] * u_ref[:, cols].astype(F32))
        z_ref[:, cols] = jax.nn.gelu(y, approximate=True).astype(BF16)


def s5_scan(u_tm, ops, *, ts=256, nl=2):
    wb, a, wc, d = ops
    rows = ts * BATCH
    j3 = lambda j, t: (j, 0, 0)
    return pl.pallas_call(
        functools.partial(_s5_kernel, ts=ts, nl=nl),
        grid=(S5_LANE_BLOCKS // nl, SEQ // ts),
        in_specs=[pl.BlockSpec((rows, nl * LANE), lambda j, t: (t, j)),
                  pl.BlockSpec((nl, LANE, 2 * S5_SL), j3),
                  pl.BlockSpec((nl, 2, S5_SL), j3),
                  pl.BlockSpec((nl, 2 * S5_SL, LANE), j3),
                  pl.BlockSpec((nl, 1, LANE), j3)],
        out_specs=pl.BlockSpec((rows, nl * LANE), lambda j, t: (t, j)),
        out_shape=jax.ShapeDtypeStruct((SEQ * BATCH, S5_WIDTH), BF16),
        scratch_shapes=[pltpu.VMEM((nl, rows, 2 * S5_SL), F32),
                        pltpu.VMEM((nl, rows, 2 * S5_SL), BF16),
                        pltpu.VMEM((nl, BATCH, 2 * S5_SL), F32)],
        compiler_params=_cparams(("parallel", "arbitrary")),
        name="s5_scan",
    )(u_tm, wb, a, wc, d)


def _s5_operators(a_re, a_im, log_dt, b_re, b_im, c_re, c_im, d_skip):
    dt = jnp.exp(log_dt)[:, None]
    lr, li = a_re, a_im
    mag = jnp.exp(lr * dt)
    ar, ai = mag * jnp.cos(li * dt), mag * jnp.sin(li * dt)
    den = lr * lr + li * li
    zr, zi = ar - 1.0, ai
    fr = (zr * lr + zi * li) / den
    fi = (zi * lr - zr * li) / den
    bbr = fr[..., None] * b_re - fi[..., None] * b_im
    bbi = fr[..., None] * b_im + fi[..., None] * b_re
    nb, gb = S5_LANE_BLOCKS, S5_GB
    eye = jnp.eye(gb, dtype=F32)

    def in_map(bb):
        return jnp.einsum('jgnp,gh->jgphn', bb.reshape(nb, gb, S5_N, S5_P), eye).reshape(
            nb, LANE, S5_SL)

    def out_map(cc):
        return jnp.einsum('jgpn,gh->jhngp', cc.reshape(nb, gb, S5_P, S5_N), eye).reshape(
            nb, S5_SL, LANE)

    wb = jnp.concatenate([in_map(bbr), in_map(bbi)], axis=2).astype(BF16)
    wc = jnp.concatenate([out_map(c_re), out_map(-c_im)], axis=1).astype(BF16)
    a = jnp.stack([ar.reshape(nb, S5_SL), ai.reshape(nb, S5_SL)], axis=1)
    return wb, a, wc, d_skip.reshape(nb, 1, LANE)


def _glu_kernel(z_ref, zc_ref, w_ref, b_ref, o_ref):
    gate = jnp.dot(z_ref[...], w_ref[...], preferred_element_type=F32) + b_ref[...]
    o_ref[...] = (zc_ref[...].astype(F32) * jax.nn.sigmoid(gate)).astype(BF16)


def s5_glu(z_tm, w, b, *, tm=1024, tn=512):
    n = w.shape[1]
    nt = SEQ // tm
    nj = n // tn
    z2 = z_tm.reshape(SEQ, BATCH * n)
    return pl.pallas_call(
        _glu_kernel,
        grid=(BATCH, nt, nj),
        in_specs=[pl.BlockSpec((tm, n), lambda b, i, j: (i, b)),
                  pl.BlockSpec((tm, tn), lambda b, i, j: (i, b * nj + j)),
                  pl.BlockSpec((n, tn), lambda b, i, j: (0, j)),
                  pl.BlockSpec((1, tn), lambda b, i, j: (0, j))],
        out_specs=pl.BlockSpec((tm, tn), lambda b, i, j: (b * nt + i, j)),
        out_shape=jax.ShapeDtypeStruct((TOKENS, n), BF16),
        compiler_params=_cparams(("parallel", "parallel", "arbitrary")),
        name="s5_glu",
    )(z2, z2, w, b.reshape(1, n))


def _diff_attn_kernel(scal_ref, q_ref, k_ref, v_ref, pq_ref, pk_ref, gn_ref, o_ref,
                      *, rc, out_scale):
    h = pl.program_id(1)
    slope = scal_ref[h]
    lam = scal_ref[DIFF_HEADS]
    lane = lax.broadcasted_iota(jnp.int32, (rc, 2 * DIFF_DQK), 1)
    for c in range(SEQ // rc):
        lo = c * rc
        rows = slice(lo, lo + rc)
        q = q_ref[rows, :].astype(F32) * (DIFF_DQK ** -0.5 * LOG2_E)
        qa = jnp.where(lane < DIFF_DQK, q, 0.0).astype(BF16)
        qb = jnp.where(lane >= DIFF_DQK, q, 0.0).astype(BF16)
        pq = jnp.broadcast_to(pq_ref[rows, :], (rc, LANE))
        bias_d = slope * jnp.abs(_lane_tile(pq, rc) - pk_ref[0, :, lo:lo + rc])
        bias_p = slope * jnp.abs(_lane_tile(pq, lo) - pk_ref[0, :, :lo]) if c else None
        pva = _causal_rows(qa, k_ref, v_ref, c, rc, bias_d, bias_p)
        pvb = _causal_rows(qb, k_ref, v_ref, c, rc, bias_d, bias_p)
        o = (pva[:, :DIFF_DV] / pva[:, DIFF_DV:]
             - lam * (pvb[:, :DIFF_DV] / pvb[:, DIFF_DV:]))
        o_ref[rows, :] = (_rms(o, gn_ref[...]) * out_scale).astype(BF16)


def diff_attention(scal, proj, posq, posk, gn, *, out_scale, rc=512):
    hb = lambda base: (lambda b, h, s: (b, base // DIFF_DV + h))
    return pl.pallas_call(
        functools.partial(_diff_attn_kernel, rc=rc, out_scale=out_scale),
        grid_spec=pltpu.PrefetchScalarGridSpec(
            num_scalar_prefetch=1,
            grid=(BATCH, DIFF_HEADS),
            in_specs=[pl.BlockSpec((SEQ, DIFF_DV), hb(OD_Q)),
                      pl.BlockSpec((SEQ, DIFF_DV), hb(OD_K)),
                      pl.BlockSpec((SEQ, DIFF_DV), hb(OD_V)),
                      pl.BlockSpec((SEQ, 1), lambda b, h, s: (b, 0)),
                      pl.BlockSpec((1, 1, SEQ), lambda b, h, s: (b, 0, 0)),
                      pl.BlockSpec((1, DIFF_DV), lambda b, h, s: (0, 0))],
            out_specs=pl.BlockSpec((SEQ, DIFF_DV), lambda b, h, s: (b, h))),
        out_shape=jax.ShapeDtypeStruct((TOKENS, DIFF_HEADS * DIFF_DV), BF16),
        compiler_params=_cparams(("parallel", "parallel")),
        name="diff_attention",
    )(scal, proj, proj, proj, posq, posk, gn.reshape(1, DIFF_DV))


OD_Q = 0
OD_K = 1024
OD_V = 2048


def _pack_even_w_in(w):
    cq, ckv, kr, gq, gk, gv, glr, gout = jnp.split(
        w, [512, 1024, 1088, 1600, 2112, 3136, 3152], axis=1)
    half = MLA_ROPE // 2
    kr_dup = jnp.concatenate([kr, kr[:, half:], kr[:, :half]], axis=1)
    glr_pad = jnp.pad(glr, ((0, 0), (0, EV_COLS - EV_GLR - GLA_GATE_RANK)))
    return jnp.concatenate([cq, ckv, gq, gk, gv, gout, kr_dup, glr_pad], axis=1).astype(BF16)


def _pack_w_uq(w):
    w = w.reshape(MLA_RANK, MLA_HEADS, MLA_NOPE + MLA_ROPE)
    half = MLA_ROPE // 2
    t1 = w[..., MLA_NOPE:MLA_NOPE + half]
    t2 = w[..., MLA_NOPE + half:]
    w = jnp.concatenate([w[..., :MLA_NOPE], t1, t2, t2, t1], axis=-1)
    return w.reshape(MLA_RANK, MLA_HEADS * MLA_QK).astype(BF16)


def _rope_tables(positions):
    half = MLA_ROPE // 2
    inv_freq = ROPE_THETA ** (-jnp.arange(half, dtype=F32) / half)
    ang = positions.astype(F32).reshape(TOKENS, 1) * inv_freq
    cos, sin = jnp.cos(ang), jnp.sin(ang)
    zeros = jnp.zeros((TOKENS, 2 * half), F32)
    return (jnp.concatenate([cos, cos, zeros], axis=1),
            jnp.concatenate([-sin, sin, zeros], axis=1))


def _even_mixer(h, g_mix, positions, w_in, q_norm, w_uq, kv_norm, w_ukv, w_gate_up, b_gate,
                g_norm, w_out, ffn_weights):
    proj = norm_matmul(h, g_mix, _pack_even_w_in(w_in), tm=1024, tn=1536)
    ct, st = _rope_tables(positions)
    q, k, v = mla_proj(proj, q_norm, kv_norm, _pack_w_uq(w_uq), w_ukv.astype(BF16), ct, st)
    o_mla, ffn_bf16 = mla_attention(q, k, v, ffn_weights)
    wg = jnp.pad(w_gate_up, ((0, LANE - GLA_GATE_RANK), (0, 0))).astype(BF16)
    o_gla = gla_mixer(proj.reshape(BATCH, SEQ, EV_COLS), wg,
                      b_gate.reshape(1, GLA_HEADS * GLA_DK), g_norm).reshape(TOKENS, -1)
    return out_proj_residual(h, o_mla, o_gla, w_out.astype(BF16)), ffn_bf16


def _odd_mixer(h, g_mix, positions, layer, w_in, a_re, a_im, log_dt, b_re, b_im, c_re, c_im,
               d_skip, w_glu, b_glu, lq1, lk1, lq2, lk2, d_norm, w_out):
    lambda_init = 0.8 - 0.6 * math.exp(-0.3 * layer)
    u_tm, proj = odd_in_proj(h, g_mix, w_in.astype(BF16))
    z_tm = s5_scan(u_tm, _s5_operators(a_re, a_im, log_dt, b_re, b_im, c_re, c_im, d_skip))
    o_s5 = s5_glu(z_tm, w_glu.astype(BF16), b_glu)
    lam = (jnp.exp(jnp.sum(lq1 * lk1)) - jnp.exp(jnp.sum(lq2 * lk2)) + lambda_init)
    slopes = jnp.exp2(-8.0 * jnp.arange(1, DIFF_HEADS + 1, dtype=F32) / DIFF_HEADS)
    scal = jnp.concatenate([slopes * LOG2_E, lam.reshape(1)]).astype(F32)
    posf = positions.astype(F32)
    o_diff = diff_attention(scal, proj, posf.reshape(TOKENS, 1), posf.reshape(BATCH, 1, SEQ),
                            d_norm, out_scale=1.0 - lambda_init)
    return out_proj_residual(h, o_s5, o_diff, w_out.astype(BF16))


def kernel(x, positions, norm_mix, norm_ffn, final_norm, ffn_w_gate, ffn_w_up, ffn_w_down, ag_w_in, mla_q_norm, mla_w_uq, mla_kv_norm, mla_w_ukv, gla_w_gate_up, gla_b_gate, gla_norm, ag_w_out, cd_w_in, s5_a_re, s5_a_im, s5_log_dt, s5_b_re, s5_b_im, s5_c_re, s5_c_im, s5_d, s5_w_glu, s5_b_glu, diff_lambda_q1, diff_lambda_k1, diff_lambda_q2, diff_lambda_k2, diff_norm, cd_w_out):
    h = x.reshape(TOKENS, D_MODEL)
    ffn_f32 = lambda layer: (layer, ffn_w_gate, ffn_w_up, ffn_w_down)
    ffn_bf16 = None
    for layer in range(DEPTH):
        i = layer // 2
        if layer % 2 == 0:
            h, ffn_bf16 = _even_mixer(h, norm_mix[layer], positions, ag_w_in[i], mla_q_norm[i],
                                      mla_w_uq[i], mla_kv_norm[i], mla_w_ukv[i],
                                      gla_w_gate_up[i], gla_b_gate[i], gla_norm[i], ag_w_out[i],
                                      ffn_f32(layer))
        else:
            h = _odd_mixer(h, norm_mix[layer], positions, layer, cd_w_in[i], s5_a_re[i],
                           s5_a_im[i], s5_log_dt[i], s5_b_re[i], s5_b_im[i], s5_c_re[i],
                           s5_c_im[i], s5_d[i], s5_w_glu[i], s5_b_glu[i], diff_lambda_q1[i],
                           diff_lambda_k1[i], diff_lambda_q2[i], diff_lambda_k2[i],
                           diff_norm[i], cd_w_out[i])
        hosts_next = layer % 2 == 0 and layer + 1 < DEPTH
        h, next_bf16 = ffn_residual(h, norm_ffn[layer], *ffn_bf16, final_norm,
                                    final_norm=(layer == DEPTH - 1),
                                    next_weights=ffn_f32(layer + 1) if hosts_next else None)
        ffn_bf16 = next_bf16
    return h.reshape(BATCH, SEQ, D_MODEL)
```

```python
import functools
import math
from typing import Callable, NamedTuple

import jax
import jax.numpy as jnp
from jax import lax
from jax.experimental import pallas as pl
from jax.experimental.pallas import tpu as pltpu

F32 = jnp.float32
BF16 = jnp.bfloat16

D_MODEL = 2048
BATCH = 8
SEQ = 2048
DEPTH = 2
TOKENS = BATCH * SEQ
MIX_HALF = D_MODEL // 2
RMS_EPS = 1e-6

MLA_NOPE = 128
MLA_ROPE = 64
MLA_V = 128
MLA_HEADS = 8
MLA_RANK = 512
ROPE_THETA = 10000.0
MLA_QK = 256

GLA_HEADS = 4
GLA_DK = 128
GLA_DV = 256
GLA_GATE_RANK = 16
GLA_GATE_NORM = 16.0
GLA_CHUNK = 128

S5_WIDTH = MIX_HALF
S5_P = 16
S5_GROUPS = 64
S5_N = 64
S5_GB = 8
S5_LANE_BLOCKS = S5_GROUPS // S5_GB
S5_SL = S5_GB * S5_N

DIFF_DQK = 64
DIFF_DV = 128
DIFF_HEADS = 8

FFN_HIDDEN = 5632
FFN_VMEM_MB = 58

LANE = 128
NEG_BIG = -1e30
LOG2_E = math.log2(math.e)

EV_CQ = 0
EV_CKV = 512
EV_GQ = 1024
EV_GK = 1536
EV_GV = 2048
EV_GOUT = 3072
EV_KROPE = 4096
EV_GLR = 4224
EV_COLS = 4608


def _cparams(sem, vmem_mb=48):
    return pltpu.CompilerParams(dimension_semantics=sem,
                                vmem_limit_bytes=vmem_mb * 1024 * 1024)


class _CastJob(NamedTuple):
    w: jax.Array
    layer: int
    block: tuple
    index: Callable


def _call_with_casts(body, casts, args, *, in_specs, out_specs, out_shape, **kwargs):
    n_in, n_out, n_cast = len(in_specs), len(out_specs), len(casts)

    def kernel(*refs):
        ins, rest = refs[:n_in], refs[n_in:]
        cast_in, rest = rest[:n_cast], rest[n_cast:]
        outs, rest = rest[:n_out], rest[n_out:]
        cast_out, scratch = rest[:n_cast], rest[n_cast:]
        for src, dst in zip(cast_in, cast_out):
            dst[...] = src[0].astype(BF16)
        body(*ins, *outs, *scratch)

    cast_in_specs = [pl.BlockSpec((1,) + c.block, lambda *g, c=c: (c.layer,) + tuple(c.index(*g)))
                     for c in casts]
    cast_out_specs = [pl.BlockSpec(c.block, lambda *g, c=c: tuple(c.index(*g))) for c in casts]
    cast_shapes = [jax.ShapeDtypeStruct(c.w.shape[1:], BF16) for c in casts]
    res = pl.pallas_call(kernel, in_specs=list(in_specs) + cast_in_specs,
                         out_specs=list(out_specs) + cast_out_specs,
                         out_shape=list(out_shape) + cast_shapes, **kwargs)(
                             *args, *[c.w for c in casts])
    return res[:n_out], res[n_out:]


def _rms(x, g):
    ms = jnp.mean(x * x, axis=-1, keepdims=True)
    return x * lax.rsqrt(ms + RMS_EPS) * g


def _nt_dot(a, b):
    return lax.dot_general(a, b, (((1,), (1,)), ((), ())), preferred_element_type=F32)


def _norm_matmul_kernel(x_ref, g_ref, w_ref, o_ref, xn_ref):
    @pl.when(pl.program_id(1) == 0)
    def _():
        xn_ref[...] = _rms(x_ref[...].astype(F32), g_ref[...]).astype(BF16)

    o_ref[...] = jnp.dot(xn_ref[...], w_ref[...],
                         preferred_element_type=F32).astype(o_ref.dtype)


def norm_matmul(x, g, w, *, tm=512, tn=512, out_dtype=BF16):
    t, k = x.shape
    n = w.shape[1]
    return pl.pallas_call(
        _norm_matmul_kernel,
        grid=(t // tm, n // tn),
        in_specs=[pl.BlockSpec((tm, k), lambda i, j: (i, 0)),
                  pl.BlockSpec((1, k), lambda i, j: (0, 0)),
                  pl.BlockSpec((k, tn), lambda i, j: (0, j))],
        out_specs=pl.BlockSpec((tm, tn), lambda i, j: (i, j)),
        out_shape=jax.ShapeDtypeStruct((t, n), out_dtype),
        scratch_shapes=[pltpu.VMEM((tm, k), BF16)],
        compiler_params=_cparams(("parallel", "arbitrary")),
        name="norm_matmul",
    )(x, g.reshape(1, k), w)


def _rope_half(blk, ct, st):
    return blk * ct + pltpu.roll(blk, 64, 1) * st


def _mla_proj_kernel(cq_ref, ckv_ref, kr_ref, gq_ref, gkv_ref, wq_ref, wkv_ref, ct_ref, st_ref,
                     q_ref, k_ref, v_ref, *, scale):
    ct, st = ct_ref[...], st_ref[...]
    xq = _rms(cq_ref[...].astype(F32), gq_ref[...]).astype(BF16)
    xkv = _rms(ckv_ref[...].astype(F32), gkv_ref[...]).astype(BF16)
    kr = _rope_half(kr_ref[...].astype(F32), ct, st).astype(BF16)
    for h in range(MLA_HEADS):
        lo, mid, hi = h * MLA_QK, h * MLA_QK + LANE, (h + 1) * MLA_QK
        a = jnp.dot(xq, wq_ref[:, lo:hi], preferred_element_type=F32)
        q_ref[:, lo:mid] = (a[:, :LANE] * scale).astype(BF16)
        q_ref[:, mid:hi] = (_rope_half(a[:, LANE:], ct, st) * scale).astype(BF16)
        kv = jnp.dot(xkv, wkv_ref[:, lo:hi], preferred_element_type=F32)
        k_ref[:, lo:mid] = kv[:, :LANE].astype(BF16)
        k_ref[:, mid:hi] = kr
        v_ref[:, h * MLA_V:(h + 1) * MLA_V] = kv[:, LANE:].astype(BF16)


def mla_proj(proj, gq, gkv, wq, wkv, ct, st, *, tm=1024):
    scale = (MLA_NOPE + MLA_ROPE) ** -0.5 * LOG2_E
    row = lambda i: (i, 0)
    fixed = lambda i: (0, 0)
    return pl.pallas_call(
        functools.partial(_mla_proj_kernel, scale=scale),
        grid=(TOKENS // tm,),
        in_specs=[pl.BlockSpec((tm, MLA_RANK), lambda i: (i, EV_CQ // MLA_RANK)),
                  pl.BlockSpec((tm, MLA_RANK), lambda i: (i, EV_CKV // MLA_RANK)),
                  pl.BlockSpec((tm, LANE), lambda i: (i, EV_KROPE // LANE)),
                  pl.BlockSpec((1, MLA_RANK), fixed),
                  pl.BlockSpec((1, MLA_RANK), fixed),
                  pl.BlockSpec((MLA_RANK, MLA_HEADS * MLA_QK), fixed),
                  pl.BlockSpec((MLA_RANK, MLA_HEADS * (MLA_NOPE + MLA_V)), fixed),
                  pl.BlockSpec((tm, LANE), row),
                  pl.BlockSpec((tm, LANE), row)],
        out_specs=[pl.BlockSpec((tm, MLA_HEADS * MLA_QK), row),
                   pl.BlockSpec((tm, MLA_HEADS * MLA_QK), row),
                   pl.BlockSpec((tm, MLA_HEADS * MLA_V), row)],
        out_shape=[jax.ShapeDtypeStruct((TOKENS, MLA_HEADS * MLA_QK), BF16),
                   jax.ShapeDtypeStruct((TOKENS, MLA_HEADS * MLA_QK), BF16),
                   jax.ShapeDtypeStruct((TOKENS, MLA_HEADS * MLA_V), BF16)],
        compiler_params=_cparams(("parallel",)),
        name="mla_proj",
    )(proj, proj, proj, gq.reshape(1, MLA_RANK), gkv.reshape(1, MLA_RANK), wq, wkv, ct, st)


def _causal_mask(tq, tk):
    row = lax.broadcasted_iota(jnp.int32, (tq, tk), 0)
    col = lax.broadcasted_iota(jnp.int32, (tq, tk), 1)
    return col <= row


def _lane_tile(x, width):
    return jnp.concatenate([x] * (width // LANE), axis=1)


def _with_ones(v):
    return jnp.concatenate([v, jnp.ones_like(v)], axis=1)


def _softmax_step(s, v_ext, m_ref, l_ref, acc_ref, rows):
    m_prev = m_ref[rows, :]
    m_new = jnp.maximum(m_prev, jnp.max(s, axis=1, keepdims=True))
    p = jnp.exp2(s - _lane_tile(m_new, s.shape[1]))
    alpha = jnp.exp2(m_prev - m_new)
    pv = jnp.dot(p.astype(BF16), v_ext, preferred_element_type=F32)
    dv = acc_ref.shape[1]
    acc_ref[rows, :] = alpha * acc_ref[rows, :] + pv[:, :dv]
    l_ref[rows, :] = alpha * l_ref[rows, :] + pv[:, dv:]
    m_ref[rows, :] = m_new


def _causal_rows(q, k_ref, v_ref, c, rc, bias_d=None, bias_p=None):
    lo = c * rc
    s_d = _nt_dot(q, k_ref[lo:lo + rc, :])
    if bias_d is not None:
        s_d = s_d - bias_d
    s_d = jnp.where(_causal_mask(rc, rc), s_d, NEG_BIG)
    m = jnp.max(s_d, axis=1, keepdims=True)
    if c:
        s_p = _nt_dot(q, k_ref[:lo, :])
        if bias_p is not None:
            s_p = s_p - bias_p
        m = jnp.maximum(m, jnp.max(s_p, axis=1, keepdims=True))
    m = jnp.broadcast_to(m, (rc, LANE))
    pv = jnp.dot(jnp.exp2(s_d - _lane_tile(m, rc)).astype(BF16),
                 _with_ones(v_ref[lo:lo + rc, :]), preferred_element_type=F32)
    if c:
        pv = pv + jnp.dot(jnp.exp2(s_p - _lane_tile(m, lo)).astype(BF16),
                          _with_ones(v_ref[:lo, :]), preferred_element_type=F32)
    return pv


def _mla_attn_kernel(q_ref, k_ref, v_ref, o_ref, *, rc):
    for c in range(SEQ // rc):
        rows = slice(c * rc, (c + 1) * rc)
        pv = _causal_rows(q_ref[rows, :], k_ref, v_ref, c, rc)
        o_ref[rows, :] = (pv[:, :MLA_V] / pv[:, MLA_V:]).astype(BF16)


def mla_attention(q, k, v, ffn_weights, *, rc=512):
    layer, gate, up, down = ffn_weights
    split = 4
    where = lambda b, h: ((b * MLA_HEADS + h) // split, (b * MLA_HEADS + h) % split)
    rows = BATCH * MLA_HEADS // split
    up_blk = (gate.shape[1] // rows, gate.shape[2] // split)
    down_blk = (down.shape[1] // rows, down.shape[2] // split)
    casts = [_CastJob(gate, layer, up_blk, where), _CastJob(up, layer, up_blk, where),
             _CastJob(down, layer, down_blk, where)]
    (out,), cast = _call_with_casts(
        functools.partial(_mla_attn_kernel, rc=rc), casts, (q, k, v),
        grid=(BATCH, MLA_HEADS),
        in_specs=[pl.BlockSpec((SEQ, MLA_QK), lambda b, h: (b, h)),
                  pl.BlockSpec((SEQ, MLA_QK), lambda b, h: (b, h)),
                  pl.BlockSpec((SEQ, MLA_V), lambda b, h: (b, h))],
        out_specs=[pl.BlockSpec((SEQ, MLA_V), lambda b, h: (b, h))],
        out_shape=[jax.ShapeDtypeStruct((TOKENS, MLA_HEADS * MLA_V), BF16)],
        compiler_params=_cparams(("parallel", "parallel")),
        name="mla_attention")
    return out, cast


def _split3(x):
    hi = x.astype(BF16)
    r1 = x - hi.astype(F32)
    mid = r1.astype(BF16)
    lo = (r1 - mid.astype(F32)).astype(BF16)
    return hi, mid, lo


def _gla_kernel(q_ref, k_ref, v_ref, go_ref, glr_ref, wg_ref, bg_ref, gn_ref, o_ref, st_ref,
                *, chunk):
    @pl.when(pl.program_id(1) == 0)
    def _():
        st_ref[...] = jnp.zeros_like(st_ref)

    mask = _causal_mask(chunk, chunk)
    tri = mask.astype(BF16)
    for b in range(q_ref.shape[0]):
        z = jnp.dot(glr_ref[b], wg_ref[...], preferred_element_type=F32) + bg_ref[...]
        log_a = -(jnp.maximum(-z, 0.0) + jnp.log1p(jnp.exp(-jnp.abs(z)))) / GLA_GATE_NORM
        hi, mid, lo = _split3(log_a)
        cum_all = (jnp.dot(tri, hi, preferred_element_type=F32)
                   + jnp.dot(tri, mid, preferred_element_type=F32)
                   + jnp.dot(tri, lo, preferred_element_type=F32))

        for h in range(GLA_HEADS):
            ks = slice(h * GLA_DK, (h + 1) * GLA_DK)
            vs = slice(h * GLA_DV, (h + 1) * GLA_DV)
            cum = cum_all[:, ks]
            last = cum[chunk - 1:chunk, :]
            q = q_ref[b, :, ks].astype(F32) * GLA_DK ** -0.5
            k = k_ref[b, :, ks].astype(F32)
            v = v_ref[b, :, vs]
            st = st_ref[b, h]
            q_in = (q * jnp.exp(cum)).astype(BF16)
            q_hat = (q * jnp.exp(cum - last)).astype(BF16)
            k_hat = (k * jnp.exp(last - cum)).astype(BF16)

            o = _nt_dot(q_in, st.astype(BF16))
            attn = jnp.where(mask, _nt_dot(q_hat, k_hat), 0.0)
            o = o + jnp.dot(attn.astype(BF16), v, preferred_element_type=F32)
            vt = v.astype(F32).T.astype(BF16)
            st_ref[b, h] = st * jnp.exp(last) + jnp.dot(vt, k_hat, preferred_element_type=F32)

            o = _rms(o, gn_ref[...])
            g = go_ref[b, :, vs].astype(F32)
            o_ref[b, :, vs] = (o * (g * jax.nn.sigmoid(g))).astype(BF16)


def gla_mixer(proj, wg, bg, gn, *, chunk=GLA_CHUNK, nb=2):
    kw, vw = GLA_HEADS * GLA_DK, GLA_HEADS * GLA_DV
    col = lambda base, width: (lambda b, c: (b, c, base // width))
    fixed = lambda b, c: (0, 0)
    return pl.pallas_call(
        functools.partial(_gla_kernel, chunk=chunk),
        grid=(BATCH // nb, SEQ // chunk),
        in_specs=[pl.BlockSpec((nb, chunk, kw), col(EV_GQ, kw)),
                  pl.BlockSpec((nb, chunk, kw), col(EV_GK, kw)),
                  pl.BlockSpec((nb, chunk, vw), col(EV_GV, vw)),
                  pl.BlockSpec((nb, chunk, vw), col(EV_GOUT, vw)),
                  pl.BlockSpec((nb, chunk, LANE), col(EV_GLR, LANE)),
                  pl.BlockSpec((LANE, kw), fixed),
                  pl.BlockSpec((1, kw), fixed),
                  pl.BlockSpec((1, GLA_DV), fixed)],
        out_specs=pl.BlockSpec((nb, chunk, vw), lambda b, c: (b, c, 0)),
        out_shape=jax.ShapeDtypeStruct((BATCH, SEQ, vw), BF16),
        scratch_shapes=[pltpu.VMEM((nb, GLA_HEADS, GLA_DV, GLA_DK), F32)],
        compiler_params=_cparams(("parallel", "arbitrary")),
        name="gla_mixer",
    )(proj, proj, proj, proj, proj, wg, bg, gn.reshape(1, GLA_DV))


def _out_proj_kernel(r_ref, a1_ref, a2_ref, w1_ref, w2_ref, o_ref):
    o_ref[...] = (r_ref[...]
                  + jnp.dot(a1_ref[...], w1_ref[...], preferred_element_type=F32)
                  + jnp.dot(a2_ref[...], w2_ref[...], preferred_element_type=F32))


def out_proj_residual(r, a1, a2, w, *, tm=512):
    half = a1.shape[1]
    n = w.shape[1]
    return pl.pallas_call(
        _out_proj_kernel,
        grid=(TOKENS // tm,),
        in_specs=[pl.BlockSpec((tm, n), lambda i: (i, 0)),
                  pl.BlockSpec((tm, half), lambda i: (i, 0)),
                  pl.BlockSpec((tm, half), lambda i: (i, 0)),
                  pl.BlockSpec((half, n), lambda i: (0, 0)),
                  pl.BlockSpec((half, n), lambda i: (1, 0))],
        out_specs=pl.BlockSpec((tm, n), lambda i: (i, 0)),
        out_shape=jax.ShapeDtypeStruct((TOKENS, n), F32),
        compiler_params=_cparams(("parallel",)),
        name="out_proj_residual",
    )(r, a1, a2, w, w)


def _ffn_kernel(h_ref, g_ref, wg_ref, wu_ref, wd_ref, fg_ref, o_ref, xn_ref, *, final_norm):
    f = pl.program_id(1)

    @pl.when(f == 0)
    def _():
        x = h_ref[...]
        xn_ref[...] = _rms(x, g_ref[...]).astype(BF16)
        o_ref[...] = x

    xn = xn_ref[...]
    a = jnp.dot(xn, wg_ref[...], preferred_element_type=F32)
    b = jnp.dot(xn, wu_ref[...], preferred_element_type=F32)
    act = (a * jax.nn.sigmoid(a) * b).astype(BF16)
    o_ref[...] += jnp.dot(act, wd_ref[...], preferred_element_type=F32)

    if final_norm:
        @pl.when(f == pl.num_programs(1) - 1)
        def _():
            o_ref[...] = _rms(o_ref[...], fg_ref[...])


def ffn_residual(h, g, wg, wu, wd, fg, *, final_norm, next_weights=None, tm=1024, tf=512):
    d = h.shape[1]
    hidden = wg.shape[1]
    n_rows, n_f = TOKENS // tm, hidden // tf
    casts = []
    if next_weights is not None:
        layer, ngate, nup, ndown = next_weights
        up_blk = (d // n_rows, tf)
        casts = [_CastJob(ngate, layer, up_blk, lambda i, f: (i, f)),
                 _CastJob(nup, layer, up_blk, lambda i, f: (i, f)),
                 _CastJob(ndown, layer, (tf, d // n_rows), lambda i, f: (f, i))]
    (out,), cast = _call_with_casts(
        functools.partial(_ffn_kernel, final_norm=final_norm), casts,
        (h, g.reshape(1, d), wg, wu, wd, fg.reshape(1, d)),
        grid=(n_rows, n_f),
        in_specs=[pl.BlockSpec((tm, d), lambda i, f: (i, 0)),
                  pl.BlockSpec((1, d), lambda i, f: (0, 0)),
                  pl.BlockSpec((d, tf), lambda i, f: (0, f)),
                  pl.BlockSpec((d, tf), lambda i, f: (0, f)),
                  pl.BlockSpec((tf, d), lambda i, f: (f, 0)),
                  pl.BlockSpec((1, d), lambda i, f: (0, 0))],
        out_specs=[pl.BlockSpec((tm, d), lambda i, f: (i, 0))],
        out_shape=[jax.ShapeDtypeStruct((TOKENS, d), F32)],
        scratch_shapes=[pltpu.VMEM((tm, d), BF16)],
        compiler_params=_cparams(("parallel", "arbitrary"), vmem_mb=FFN_VMEM_MB),
        name="ffn_residual")
    return out, cast


def _s5_kernel(u_ref, wb_ref, a_ref, wc_ref, d_ref, z_ref, ui_ref, bu_ref, xb_ref, st_ref,
               *, ts, nl):
    @pl.when(pl.program_id(1) == 0)
    def _():
        st_ref[...] = jnp.zeros_like(st_ref)

    blocks = range(nl)
    for l in blocks:
        cols = slice(l * LANE, (l + 1) * LANE)
        for b in range(BATCH):
            ui_ref[l, pl.ds(b, ts, stride=BATCH), :] = u_ref[b, :, cols].astype(F32)
        bu_ref[l] = jnp.dot(ui_ref[l].astype(BF16), wb_ref[l],
                            preferred_element_type=F32)
    coef = [(a_ref[l, 0:1, :], a_ref[l, 1:2, :]) for l in blocks]

    def two_steps(i, carry):
        r0 = pl.ds(pl.multiple_of(i * 2 * BATCH, 2 * BATCH), BATCH)
        r1 = pl.ds(pl.multiple_of(i * 2 * BATCH + BATCH, BATCH), BATCH)
        both = pl.ds(pl.multiple_of(i * 2 * BATCH, 2 * BATCH), 2 * BATCH)
        out = []
        for l in blocks:
            (xr, xi), (ar, ai) = carry[l], coef[l]
            xr1 = ar * xr - ai * xi + bu_ref[l, r0, :S5_SL]
            xi1 = ar * xi + ai * xr + bu_ref[l, r0, S5_SL:]
            xr2 = ar * xr1 - ai * xi1 + bu_ref[l, r1, :S5_SL]
            xi2 = ar * xi1 + ai * xr1 + bu_ref[l, r1, S5_SL:]
            xb_ref[l, both, :S5_SL] = jnp.concatenate([xr1, xr2], axis=0).astype(BF16)
            xb_ref[l, both, S5_SL:] = jnp.concatenate([xi1, xi2], axis=0).astype(BF16)
            out.append((xr2, xi2))
        return tuple(out)

    init = tuple((st_ref[l, :, :S5_SL], st_ref[l, :, S5_SL:]) for l in blocks)
    final = lax.fori_loop(0, ts // 2, two_steps, init, unroll=4)
    for l in blocks:
        st_ref[l, :, :S5_SL] = final[l][0]
        st_ref[l, :, S5_SL:] = final[l][1]
        y = (jnp.dot(xb_ref[l], wc_ref[l], preferred_element_type=F32)
             + d_ref[l] * ui_ref[l])
        ui_ref[l] = jax.nn.gelu(y, approximate=True)
        cols = slice(l * LANE, (l + 1) * LANE)
        for b in range(BATCH):
            z_ref[b, :, cols] = ui_ref[l, pl.ds(b, ts, stride=BATCH), :].astype(BF16)


def s5_scan(proj, ops, *, ts=256, nl=2):
    wb, a, wc, d = ops
    rows = ts * BATCH
    j3 = lambda j, t: (j, 0, 0)
    return pl.pallas_call(
        functools.partial(_s5_kernel, ts=ts, nl=nl),
        grid=(S5_LANE_BLOCKS // nl, SEQ // ts),
        in_specs=[pl.BlockSpec((BATCH, ts, nl * LANE), lambda j, t: (0, t, j)),
                  pl.BlockSpec((nl, LANE, 2 * S5_SL), j3),
                  pl.BlockSpec((nl, 2, S5_SL), j3),
                  pl.BlockSpec((nl, 2 * S5_SL, LANE), j3),
                  pl.BlockSpec((nl, 1, LANE), j3)],
        out_specs=pl.BlockSpec((BATCH, ts, nl * LANE), lambda j, t: (0, t, j)),
        out_shape=jax.ShapeDtypeStruct((BATCH, SEQ, S5_WIDTH), BF16),
        scratch_shapes=[pltpu.VMEM((nl, rows, LANE), F32),
                        pltpu.VMEM((nl, rows, 2 * S5_SL), F32),
                        pltpu.VMEM((nl, rows, 2 * S5_SL), BF16),
                        pltpu.VMEM((nl, BATCH, 2 * S5_SL), F32)],
        compiler_params=_cparams(("parallel", "arbitrary")),
        name="s5_scan",
    )(proj, wb, a, wc, d)


def _s5_operators(a_re, a_im, log_dt, b_re, b_im, c_re, c_im, d_skip):
    dt = jnp.exp(log_dt)[:, None]
    lr, li = a_re, a_im
    mag = jnp.exp(lr * dt)
    ar, ai = mag * jnp.cos(li * dt), mag * jnp.sin(li * dt)
    den = lr * lr + li * li
    zr, zi = ar - 1.0, ai
    fr = (zr * lr + zi * li) / den
    fi = (zi * lr - zr * li) / den
    bbr = fr[..., None] * b_re - fi[..., None] * b_im
    bbi = fr[..., None] * b_im + fi[..., None] * b_re
    nb, gb = S5_LANE_BLOCKS, S5_GB
    eye = jnp.eye(gb, dtype=F32)

    def in_map(bb):
        return jnp.einsum('jgnp,gh->jgphn', bb.reshape(nb, gb, S5_N, S5_P), eye).reshape(
            nb, LANE, S5_SL)

    def out_map(cc):
        return jnp.einsum('jgpn,gh->jhngp', cc.reshape(nb, gb, S5_P, S5_N), eye).reshape(
            nb, S5_SL, LANE)

    wb = jnp.concatenate([in_map(bbr), in_map(bbi)], axis=2).astype(BF16)
    wc = jnp.concatenate([out_map(c_re), out_map(-c_im)], axis=1).astype(BF16)
    a = jnp.stack([ar.reshape(nb, S5_SL), ai.reshape(nb, S5_SL)], axis=1)
    return wb, a, wc, d_skip.reshape(nb, 1, LANE)


def _glu_kernel(z_ref, zc_ref, w_ref, b_ref, o_ref):
    gate = jnp.dot(z_ref[...], w_ref[...], preferred_element_type=F32) + b_ref[...]
    o_ref[...] = (zc_ref[...].astype(F32) * jax.nn.sigmoid(gate)).astype(BF16)


def s5_glu(z, w, b, *, tm=1024, tn=512):
    n = w.shape[1]
    return pl.pallas_call(
        _glu_kernel,
        grid=(TOKENS // tm, n // tn),
        in_specs=[pl.BlockSpec((tm, n), lambda i, j: (i, 0)),
                  pl.BlockSpec((tm, tn), lambda i, j: (i, j)),
                  pl.BlockSpec((n, tn), lambda i, j: (0, j)),
                  pl.BlockSpec((1, tn), lambda i, j: (0, j))],
        out_specs=pl.BlockSpec((tm, tn), lambda i, j: (i, j)),
        out_shape=jax.ShapeDtypeStruct((TOKENS, n), BF16),
        compiler_params=_cparams(("parallel", "arbitrary")),
        name="s5_glu",
    )(z, z, w, b.reshape(1, n))


def _diff_attn_kernel(scal_ref, q_ref, k_ref, v_ref, pq_ref, pk_ref, gn_ref, o_ref,
                      *, rc, out_scale):
    h = pl.program_id(1)
    slope = scal_ref[h]
    lam = scal_ref[DIFF_HEADS]
    lane = lax.broadcasted_iota(jnp.int32, (rc, 2 * DIFF_DQK), 1)
    for c in range(SEQ // rc):
        lo = c * rc
        rows = slice(lo, lo + rc)
        q = q_ref[rows, :].astype(F32) * (DIFF_DQK ** -0.5 * LOG2_E)
        qa = jnp.where(lane < DIFF_DQK, q, 0.0).astype(BF16)
        qb = jnp.where(lane >= DIFF_DQK, q, 0.0).astype(BF16)
        pq = jnp.broadcast_to(pq_ref[rows, :], (rc, LANE))
        bias_d = slope * jnp.abs(_lane_tile(pq, rc) - pk_ref[0, :, lo:lo + rc])
        bias_p = slope * jnp.abs(_lane_tile(pq, lo) - pk_ref[0, :, :lo]) if c else None
        pva = _causal_rows(qa, k_ref, v_ref, c, rc, bias_d, bias_p)
        pvb = _causal_rows(qb, k_ref, v_ref, c, rc, bias_d, bias_p)
        o = (pva[:, :DIFF_DV] / pva[:, DIFF_DV:]
             - lam * (pvb[:, :DIFF_DV] / pvb[:, DIFF_DV:]))
        o_ref[rows, :] = (_rms(o, gn_ref[...]) * out_scale).astype(BF16)


def diff_attention(scal, proj, posq, posk, gn, *, out_scale, rc=512):
    hb = lambda base: (lambda b, h, s: (b, base // DIFF_DV + h))
    return pl.pallas_call(
        functools.partial(_diff_attn_kernel, rc=rc, out_scale=out_scale),
        grid_spec=pltpu.PrefetchScalarGridSpec(
            num_scalar_prefetch=1,
            grid=(BATCH, DIFF_HEADS),
            in_specs=[pl.BlockSpec((SEQ, DIFF_DV), hb(OD_Q)),
                      pl.BlockSpec((SEQ, DIFF_DV), hb(OD_K)),
                      pl.BlockSpec((SEQ, DIFF_DV), hb(OD_V)),
                      pl.BlockSpec((SEQ, 1), lambda b, h, s: (b, 0)),
                      pl.BlockSpec((1, 1, SEQ), lambda b, h, s: (b, 0, 0)),
                      pl.BlockSpec((1, DIFF_DV), lambda b, h, s: (0, 0))],
            out_specs=pl.BlockSpec((SEQ, DIFF_DV), lambda b, h, s: (b, h))),
        out_shape=jax.ShapeDtypeStruct((TOKENS, DIFF_HEADS * DIFF_DV), BF16),
        compiler_params=_cparams(("parallel", "parallel")),
        name="diff_attention",
    )(scal, proj, proj, proj, posq, posk, gn.reshape(1, DIFF_DV))


OD_Q = 1024
OD_K = 2048
OD_V = 3072


def _pack_even_w_in(w):
    cq, ckv, kr, gq, gk, gv, glr, gout = jnp.split(
        w, [512, 1024, 1088, 1600, 2112, 3136, 3152], axis=1)
    half = MLA_ROPE // 2
    kr_dup = jnp.concatenate([kr, kr[:, half:], kr[:, :half]], axis=1)
    glr_pad = jnp.pad(glr, ((0, 0), (0, EV_COLS - EV_GLR - GLA_GATE_RANK)))
    return jnp.concatenate([cq, ckv, gq, gk, gv, gout, kr_dup, glr_pad], axis=1).astype(BF16)


def _pack_w_uq(w):
    w = w.reshape(MLA_RANK, MLA_HEADS, MLA_NOPE + MLA_ROPE)
    half = MLA_ROPE // 2
    t1 = w[..., MLA_NOPE:MLA_NOPE + half]
    t2 = w[..., MLA_NOPE + half:]
    w = jnp.concatenate([w[..., :MLA_NOPE], t1, t2, t2, t1], axis=-1)
    return w.reshape(MLA_RANK, MLA_HEADS * MLA_QK).astype(BF16)


def _rope_tables(positions):
    half = MLA_ROPE // 2
    inv_freq = ROPE_THETA ** (-jnp.arange(half, dtype=F32) / half)
    ang = positions.astype(F32).reshape(TOKENS, 1) * inv_freq
    cos, sin = jnp.cos(ang), jnp.sin(ang)
    zeros = jnp.zeros((TOKENS, 2 * half), F32)
    return (jnp.concatenate([cos, cos, zeros], axis=1),
            jnp.concatenate([-sin, sin, zeros], axis=1))


def _even_mixer(h, g_mix, positions, w_in, q_norm, w_uq, kv_norm, w_ukv, w_gate_up, b_gate,
                g_norm, w_out, ffn_weights):
    proj = norm_matmul(h, g_mix, _pack_even_w_in(w_in), tm=1024, tn=1536)
    ct, st = _rope_tables(positions)
    q, k, v = mla_proj(proj, q_norm, kv_norm, _pack_w_uq(w_uq), w_ukv.astype(BF16), ct, st)
    o_mla, ffn_bf16 = mla_attention(q, k, v, ffn_weights)
    wg = jnp.pad(w_gate_up, ((0, LANE - GLA_GATE_RANK), (0, 0))).astype(BF16)
    o_gla = gla_mixer(proj.reshape(BATCH, SEQ, EV_COLS), wg,
                      b_gate.reshape(1, GLA_HEADS * GLA_DK), g_norm).reshape(TOKENS, -1)
    return out_proj_residual(h, o_mla, o_gla, w_out.astype(BF16)), ffn_bf16


def _odd_mixer(h, g_mix, positions, layer, w_in, a_re, a_im, log_dt, b_re, b_im, c_re, c_im,
               d_skip, w_glu, b_glu, lq1, lk1, lq2, lk2, d_norm, w_out):
    lambda_init = 0.8 - 0.6 * math.exp(-0.3 * layer)
    proj = norm_matmul(h, g_mix, w_in.astype(BF16), tm=1024, tn=1024)
    z = s5_scan(proj.reshape(BATCH, SEQ, -1),
                _s5_operators(a_re, a_im, log_dt, b_re, b_im, c_re, c_im, d_skip))
    o_s5 = s5_glu(z.reshape(TOKENS, S5_WIDTH), w_glu.astype(BF16), b_glu)
    lam = (jnp.exp(jnp.sum(lq1 * lk1)) - jnp.exp(jnp.sum(lq2 * lk2)) + lambda_init)
    slopes = jnp.exp2(-8.0 * jnp.arange(1, DIFF_HEADS + 1, dtype=F32) / DIFF_HEADS)
    scal = jnp.concatenate([slopes * LOG2_E, lam.reshape(1)]).astype(F32)
    posf = positions.astype(F32)
    o_diff = diff_attention(scal, proj, posf.reshape(TOKENS, 1), posf.reshape(BATCH, 1, SEQ),
                            d_norm, out_scale=1.0 - lambda_init)
    return out_proj_residual(h, o_s5, o_diff, w_out.astype(BF16))


def kernel(x, positions, norm_mix, norm_ffn, final_norm, ffn_w_gate, ffn_w_up, ffn_w_down, ag_w_in, mla_q_norm, mla_w_uq, mla_kv_norm, mla_w_ukv, gla_w_gate_up, gla_b_gate, gla_norm, ag_w_out, cd_w_in, s5_a_re, s5_a_im, s5_log_dt, s5_b_re, s5_b_im, s5_c_re, s5_c_im, s5_d, s5_w_glu, s5_b_glu, diff_lambda_q1, diff_lambda_k1, diff_lambda_q2, diff_lambda_k2, diff_norm, cd_w_out):
    h = x.reshape(TOKENS, D_MODEL)
    ffn_f32 = lambda layer: (layer, ffn_w_gate, ffn_w_up, ffn_w_down)
    ffn_bf16 = None
    for layer in range(DEPTH):
        i = layer // 2
        if layer % 2 == 0:
            h, ffn_bf16 = _even_mixer(h, norm_mix[layer], positions, ag_w_in[i], mla_q_norm[i],
                                      mla_w_uq[i], mla_kv_norm[i], mla_w_ukv[i],
                                      gla_w_gate_up[i], gla_b_gate[i], gla_norm[i], ag_w_out[i],
                                      ffn_f32(layer))
        else:
            h = _odd_mixer(h, norm_mix[layer], positions, layer, cd_w_in[i], s5_a_re[i],
                           s5_a_im[i], s5_log_dt[i], s5_b_re[i], s5_b_im[i], s5_c_re[i],
                           s5_c_im[i], s5_d[i], s5_w_glu[i], s5_b_glu[i], diff_lambda_q1[i],
                           diff_lambda_k1[i], diff_lambda_q2[i], diff_lambda_k2[i],
                           diff_norm[i], cd_w_out[i])
        hosts_next = layer % 2 == 0 and layer + 1 < DEPTH
        h, next_bf16 = ffn_residual(h, norm_ffn[layer], *ffn_bf16, final_norm,
                                    final_norm=(layer == DEPTH - 1),
                                    next_weights=ffn_f32(layer + 1) if hosts_next else None)
        ffn_bf16 = next_bf16
    return h.reshape(BATCH, SEQ, D_MODEL)
```

```python
import functools
import math
from typing import Callable, NamedTuple

import jax
import jax.numpy as jnp
from jax import lax
from jax.experimental import pallas as pl
from jax.experimental.pallas import tpu as pltpu

F32 = jnp.float32
BF16 = jnp.bfloat16

D_MODEL = 2048
BATCH = 8
SEQ = 2048
DEPTH = 2
TOKENS = BATCH * SEQ
MIX_HALF = D_MODEL // 2
RMS_EPS = 1e-6

MLA_NOPE = 128
MLA_ROPE = 64
MLA_V = 128
MLA_HEADS = 8
MLA_RANK = 512
ROPE_THETA = 10000.0
MLA_QK = 256

GLA_HEADS = 4
GLA_DK = 128
GLA_DV = 256
GLA_GATE_RANK = 16
GLA_GATE_NORM = 16.0
GLA_CHUNK = 128

S5_WIDTH = MIX_HALF
S5_P = 16
S5_GROUPS = 64
S5_N = 64
S5_GB = 8
S5_LANE_BLOCKS = S5_GROUPS // S5_GB
S5_SL = S5_GB * S5_N

DIFF_DQK = 64
DIFF_DV = 128
DIFF_HEADS = 8

FFN_HIDDEN = 5632
BIG_TILE_VMEM_MB = 58

LANE = 128
NEG_BIG = -1e30
LOG2_E = math.log2(math.e)

EV_CQ = 0
EV_CKV = 512
EV_GQ = 1024
EV_GK = 1536
EV_GV = 2048
EV_GOUT = 3072
EV_KROPE = 4096
EV_GLR = 4224
EV_COLS = 4608


def _cparams(sem, vmem_mb=48):
    return pltpu.CompilerParams(dimension_semantics=sem,
                                vmem_limit_bytes=vmem_mb * 1024 * 1024)


class _CastJob(NamedTuple):
    w: jax.Array
    layer: int
    block: tuple
    index: Callable


def _call_with_casts(body, casts, args, *, in_specs, out_specs, out_shape, **kwargs):
    n_in, n_out, n_cast = len(in_specs), len(out_specs), len(casts)

    def kernel(*refs):
        ins, rest = refs[:n_in], refs[n_in:]
        cast_in, rest = rest[:n_cast], rest[n_cast:]
        outs, rest = rest[:n_out], rest[n_out:]
        cast_out, scratch = rest[:n_cast], rest[n_cast:]
        for src, dst in zip(cast_in, cast_out):
            dst[...] = src[0].astype(BF16)
        body(*ins, *outs, *scratch)

    cast_in_specs = [pl.BlockSpec((1,) + c.block, lambda *g, c=c: (c.layer,) + tuple(c.index(*g)))
                     for c in casts]
    cast_out_specs = [pl.BlockSpec(c.block, lambda *g, c=c: tuple(c.index(*g))) for c in casts]
    cast_shapes = [jax.ShapeDtypeStruct(c.w.shape[1:], BF16) for c in casts]
    res = pl.pallas_call(kernel, in_specs=list(in_specs) + cast_in_specs,
                         out_specs=list(out_specs) + cast_out_specs,
                         out_shape=list(out_shape) + cast_shapes, **kwargs)(
                             *args, *[c.w for c in casts])
    return res[:n_out], res[n_out:]


def _rms(x, g):
    ms = jnp.mean(x * x, axis=-1, keepdims=True)
    return x * lax.rsqrt(ms + RMS_EPS) * g


def _nt_dot(a, b):
    return lax.dot_general(a, b, (((1,), (1,)), ((), ())), preferred_element_type=F32)


def _norm_matmul_kernel(x_ref, g_ref, w_ref, o_ref, xn_ref):
    @pl.when(pl.program_id(1) == 0)
    def _():
        xn_ref[...] = _rms(x_ref[...].astype(F32), g_ref[...]).astype(BF16)

    o_ref[...] = jnp.dot(xn_ref[...], w_ref[...],
                         preferred_element_type=F32).astype(o_ref.dtype)


def norm_matmul(x, g, w, *, later_weights=(), tm=1024, col_tiles=2):
    t, k = x.shape
    n = w.shape[1]
    tn = n // col_tiles
    steps = t // tm * col_tiles
    casts = [_CastJob(w3, layer, (w3.shape[1] // steps, w3.shape[2]),
                      lambda i, j: (i * col_tiles + j, 0)) for w3, layer in later_weights]
    (out,), cast = _call_with_casts(
        _norm_matmul_kernel, casts, (x, g.reshape(1, k), w),
        grid=(t // tm, col_tiles),
        in_specs=[pl.BlockSpec((tm, k), lambda i, j: (i, 0)),
                  pl.BlockSpec((1, k), lambda i, j: (0, 0)),
                  pl.BlockSpec((k, tn), lambda i, j: (0, j))],
        out_specs=[pl.BlockSpec((tm, tn), lambda i, j: (i, j))],
        out_shape=[jax.ShapeDtypeStruct((t, n), BF16)],
        scratch_shapes=[pltpu.VMEM((tm, k), BF16)],
        compiler_params=_cparams(("parallel", "arbitrary"), vmem_mb=BIG_TILE_VMEM_MB),
        name="norm_matmul")
    return out, cast


def _rope_half(blk, ct, st):
    return blk * ct + pltpu.roll(blk, 64, 1) * st


def _mla_proj_kernel(cq_ref, ckv_ref, kr_ref, gq_ref, gkv_ref, wq_ref, wkv_ref, ct_ref, st_ref,
                     q_ref, k_ref, v_ref, *, scale):
    ct, st = ct_ref[...], st_ref[...]
    xq = _rms(cq_ref[...].astype(F32), gq_ref[...]).astype(BF16)
    xkv = _rms(ckv_ref[...].astype(F32), gkv_ref[...]).astype(BF16)
    kr = _rope_half(kr_ref[...].astype(F32), ct, st).astype(BF16)
    for h in range(MLA_HEADS):
        lo, mid, hi = h * MLA_QK, h * MLA_QK + LANE, (h + 1) * MLA_QK
        a = jnp.dot(xq, wq_ref[:, lo:hi], preferred_element_type=F32)
        q_ref[:, lo:mid] = (a[:, :LANE] * scale).astype(BF16)
        q_ref[:, mid:hi] = (_rope_half(a[:, LANE:], ct, st) * scale).astype(BF16)
        kv = jnp.dot(xkv, wkv_ref[:, lo:hi], preferred_element_type=F32)
        k_ref[:, lo:mid] = kv[:, :LANE].astype(BF16)
        k_ref[:, mid:hi] = kr
        v_ref[:, h * MLA_V:(h + 1) * MLA_V] = kv[:, LANE:].astype(BF16)


def mla_proj(proj, gq, gkv, wq, wkv, ct, st, *, tm=1024):
    scale = (MLA_NOPE + MLA_ROPE) ** -0.5 * LOG2_E
    row = lambda i: (i, 0)
    fixed = lambda i: (0, 0)
    return pl.pallas_call(
        functools.partial(_mla_proj_kernel, scale=scale),
        grid=(TOKENS // tm,),
        in_specs=[pl.BlockSpec((tm, MLA_RANK), lambda i: (i, EV_CQ // MLA_RANK)),
                  pl.BlockSpec((tm, MLA_RANK), lambda i: (i, EV_CKV // MLA_RANK)),
                  pl.BlockSpec((tm, LANE), lambda i: (i, EV_KROPE // LANE)),
                  pl.BlockSpec((1, MLA_RANK), fixed),
                  pl.BlockSpec((1, MLA_RANK), fixed),
                  pl.BlockSpec((MLA_RANK, MLA_HEADS * MLA_QK), fixed),
                  pl.BlockSpec((MLA_RANK, MLA_HEADS * (MLA_NOPE + MLA_V)), fixed),
                  pl.BlockSpec((tm, LANE), row),
                  pl.BlockSpec((tm, LANE), row)],
        out_specs=[pl.BlockSpec((tm, MLA_HEADS * MLA_QK), row),
                   pl.BlockSpec((tm, MLA_HEADS * MLA_QK), row),
                   pl.BlockSpec((tm, MLA_HEADS * MLA_V), row)],
        out_shape=[jax.ShapeDtypeStruct((TOKENS, MLA_HEADS * MLA_QK), BF16),
                   jax.ShapeDtypeStruct((TOKENS, MLA_HEADS * MLA_QK), BF16),
                   jax.ShapeDtypeStruct((TOKENS, MLA_HEADS * MLA_V), BF16)],
        compiler_params=_cparams(("parallel",)),
        name="mla_proj",
    )(proj, proj, proj, gq.reshape(1, MLA_RANK), gkv.reshape(1, MLA_RANK), wq, wkv, ct, st)


def _causal_mask(tq, tk):
    row = lax.broadcasted_iota(jnp.int32, (tq, tk), 0)
    col = lax.broadcasted_iota(jnp.int32, (tq, tk), 1)
    return col <= row


def _lane_tile(x, width):
    return jnp.concatenate([x] * (width // LANE), axis=1)


def _with_ones(v):
    return jnp.concatenate([v, jnp.ones_like(v)], axis=1)


def _softmax_step(s, v_ext, m_ref, l_ref, acc_ref, rows):
    m_prev = m_ref[rows, :]
    m_new = jnp.maximum(m_prev, jnp.max(s, axis=1, keepdims=True))
    p = jnp.exp2(s - _lane_tile(m_new, s.shape[1]))
    alpha = jnp.exp2(m_prev - m_new)
    pv = jnp.dot(p.astype(BF16), v_ext, preferred_element_type=F32)
    dv = acc_ref.shape[1]
    acc_ref[rows, :] = alpha * acc_ref[rows, :] + pv[:, :dv]
    l_ref[rows, :] = alpha * l_ref[rows, :] + pv[:, dv:]
    m_ref[rows, :] = m_new


def _causal_rows(q, k_ref, v_ref, c, rc, bias_d=None, bias_p=None):
    lo = c * rc
    s_d = _nt_dot(q, k_ref[lo:lo + rc, :])
    if bias_d is not None:
        s_d = s_d - bias_d
    s_d = jnp.where(_causal_mask(rc, rc), s_d, NEG_BIG)
    m = jnp.max(s_d, axis=1, keepdims=True)
    if c:
        s_p = _nt_dot(q, k_ref[:lo, :])
        if bias_p is not None:
            s_p = s_p - bias_p
        m = jnp.maximum(m, jnp.max(s_p, axis=1, keepdims=True))
    m = jnp.broadcast_to(m, (rc, LANE))
    pv = jnp.dot(jnp.exp2(s_d - _lane_tile(m, rc)).astype(BF16),
                 _with_ones(v_ref[lo:lo + rc, :]), preferred_element_type=F32)
    if c:
        pv = pv + jnp.dot(jnp.exp2(s_p - _lane_tile(m, lo)).astype(BF16),
                          _with_ones(v_ref[:lo, :]), preferred_element_type=F32)
    return pv


def _mla_attn_kernel(q_ref, k_ref, v_ref, o_ref, *, rc):
    for c in range(SEQ // rc):
        rows = slice(c * rc, (c + 1) * rc)
        pv = _causal_rows(q_ref[rows, :], k_ref, v_ref, c, rc)
        o_ref[rows, :] = (pv[:, :MLA_V] / pv[:, MLA_V:]).astype(BF16)


def mla_attention(q, k, v, ffn_weights, *, rc=512):
    layer, gate, up, down = ffn_weights
    split = 4
    where = lambda b, h: ((b * MLA_HEADS + h) // split, (b * MLA_HEADS + h) % split)
    rows = BATCH * MLA_HEADS // split
    up_blk = (gate.shape[1] // rows, gate.shape[2] // split)
    down_blk = (down.shape[1] // rows, down.shape[2] // split)
    casts = [_CastJob(gate, layer, up_blk, where), _CastJob(up, layer, up_blk, where),
             _CastJob(down, layer, down_blk, where)]
    (out,), cast = _call_with_casts(
        functools.partial(_mla_attn_kernel, rc=rc), casts, (q, k, v),
        grid=(BATCH, MLA_HEADS),
        in_specs=[pl.BlockSpec((SEQ, MLA_QK), lambda b, h: (b, h)),
                  pl.BlockSpec((SEQ, MLA_QK), lambda b, h: (b, h)),
                  pl.BlockSpec((SEQ, MLA_V), lambda b, h: (b, h))],
        out_specs=[pl.BlockSpec((SEQ, MLA_V), lambda b, h: (b, h))],
        out_shape=[jax.ShapeDtypeStruct((TOKENS, MLA_HEADS * MLA_V), BF16)],
        compiler_params=_cparams(("parallel", "parallel")),
        name="mla_attention")
    return out, cast


def _split3(x):
    hi = x.astype(BF16)
    r1 = x - hi.astype(F32)
    mid = r1.astype(BF16)
    lo = (r1 - mid.astype(F32)).astype(BF16)
    return hi, mid, lo


def _gla_kernel(q_ref, k_ref, v_ref, go_ref, glr_ref, wg_ref, bg_ref, gn_ref, o_ref, st_ref,
                *, chunk):
    @pl.when(pl.program_id(1) == 0)
    def _():
        st_ref[...] = jnp.zeros_like(st_ref)

    mask = _causal_mask(chunk, chunk)
    tri = mask.astype(BF16)
    for b in range(q_ref.shape[0]):
        z = jnp.dot(glr_ref[b], wg_ref[...], preferred_element_type=F32) + bg_ref[...]
        log_a = -(jnp.maximum(-z, 0.0) + jnp.log1p(jnp.exp(-jnp.abs(z)))) / GLA_GATE_NORM
        hi, mid, lo = _split3(log_a)
        cum_all = (jnp.dot(tri, hi, preferred_element_type=F32)
                   + jnp.dot(tri, mid, preferred_element_type=F32)
                   + jnp.dot(tri, lo, preferred_element_type=F32))

        for h in range(GLA_HEADS):
            ks = slice(h * GLA_DK, (h + 1) * GLA_DK)
            vs = slice(h * GLA_DV, (h + 1) * GLA_DV)
            cum = cum_all[:, ks]
            last = cum[chunk - 1:chunk, :]
            q = q_ref[b, :, ks].astype(F32) * GLA_DK ** -0.5
            k = k_ref[b, :, ks].astype(F32)
            v = v_ref[b, :, vs]
            st = st_ref[b, h]
            q_in = (q * jnp.exp(cum)).astype(BF16)
            q_hat = (q * jnp.exp(cum - last)).astype(BF16)
            k_hat = (k * jnp.exp(last - cum)).astype(BF16)

            o = _nt_dot(q_in, st.astype(BF16))
            attn = jnp.where(mask, _nt_dot(q_hat, k_hat), 0.0)
            o = o + jnp.dot(attn.astype(BF16), v, preferred_element_type=F32)
            vt = v.astype(F32).T.astype(BF16)
            st_ref[b, h] = st * jnp.exp(last) + jnp.dot(vt, k_hat, preferred_element_type=F32)

            o = _rms(o, gn_ref[...])
            g = go_ref[b, :, vs].astype(F32)
            o_ref[b, :, vs] = (o * (g * jax.nn.sigmoid(g))).astype(BF16)


def gla_mixer(proj, wg, bg, gn, *, chunk=GLA_CHUNK, nb=2):
    kw, vw = GLA_HEADS * GLA_DK, GLA_HEADS * GLA_DV
    col = lambda base, width: (lambda b, c: (b, c, base // width))
    fixed = lambda b, c: (0, 0)
    return pl.pallas_call(
        functools.partial(_gla_kernel, chunk=chunk),
        grid=(BATCH // nb, SEQ // chunk),
        in_specs=[pl.BlockSpec((nb, chunk, kw), col(EV_GQ, kw)),
                  pl.BlockSpec((nb, chunk, kw), col(EV_GK, kw)),
                  pl.BlockSpec((nb, chunk, vw), col(EV_GV, vw)),
                  pl.BlockSpec((nb, chunk, vw), col(EV_GOUT, vw)),
                  pl.BlockSpec((nb, chunk, LANE), col(EV_GLR, LANE)),
                  pl.BlockSpec((LANE, kw), fixed),
                  pl.BlockSpec((1, kw), fixed),
                  pl.BlockSpec((1, GLA_DV), fixed)],
        out_specs=pl.BlockSpec((nb, chunk, vw), lambda b, c: (b, c, 0)),
        out_shape=jax.ShapeDtypeStruct((BATCH, SEQ, vw), BF16),
        scratch_shapes=[pltpu.VMEM((nb, GLA_HEADS, GLA_DV, GLA_DK), F32)],
        compiler_params=_cparams(("parallel", "arbitrary")),
        name="gla_mixer",
    )(proj, proj, proj, proj, proj, wg, bg, gn.reshape(1, GLA_DV))


def _out_proj_kernel(r_ref, a1_ref, a2_ref, w1_ref, w2_ref, o_ref):
    o_ref[...] = (r_ref[...]
                  + jnp.dot(a1_ref[...], w1_ref[...], preferred_element_type=F32)
                  + jnp.dot(a2_ref[...], w2_ref[...], preferred_element_type=F32))


def out_proj_residual(r, a1, a2, w, *, tm=512):
    half = a1.shape[1]
    n = w.shape[1]
    return pl.pallas_call(
        _out_proj_kernel,
        grid=(TOKENS // tm,),
        in_specs=[pl.BlockSpec((tm, n), lambda i: (i, 0)),
                  pl.BlockSpec((tm, half), lambda i: (i, 0)),
                  pl.BlockSpec((tm, half), lambda i: (i, 0)),
                  pl.BlockSpec((half, n), lambda i: (0, 0)),
                  pl.BlockSpec((half, n), lambda i: (1, 0))],
        out_specs=pl.BlockSpec((tm, n), lambda i: (i, 0)),
        out_shape=jax.ShapeDtypeStruct((TOKENS, n), F32),
        compiler_params=_cparams(("parallel",)),
        name="out_proj_residual",
    )(r, a1, a2, w, w)


def _ffn_kernel(h_ref, g_ref, wg_ref, wu_ref, wd_ref, fg_ref, o_ref, xn_ref, *, final_norm):
    f = pl.program_id(1)

    @pl.when(f == 0)
    def _():
        x = h_ref[...]
        xn_ref[...] = _rms(x, g_ref[...]).astype(BF16)
        o_ref[...] = x

    xn = xn_ref[...]
    a = jnp.dot(xn, wg_ref[...], preferred_element_type=F32)
    b = jnp.dot(xn, wu_ref[...], preferred_element_type=F32)
    act = (a * jax.nn.sigmoid(a) * b).astype(BF16)
    o_ref[...] += jnp.dot(act, wd_ref[...], preferred_element_type=F32)

    if final_norm:
        @pl.when(f == pl.num_programs(1) - 1)
        def _():
            o_ref[...] = _rms(o_ref[...], fg_ref[...])


def ffn_residual(h, g, wg, wu, wd, fg, *, final_norm, next_weights=None, tm=1024, tf=512):
    d = h.shape[1]
    hidden = wg.shape[1]
    n_rows, n_f = TOKENS // tm, hidden // tf
    casts = []
    if next_weights is not None:
        layer, ngate, nup, ndown = next_weights
        up_blk = (d // n_rows, tf)
        casts = [_CastJob(ngate, layer, up_blk, lambda i, f: (i, f)),
                 _CastJob(nup, layer, up_blk, lambda i, f: (i, f)),
                 _CastJob(ndown, layer, (tf, d // n_rows), lambda i, f: (f, i))]
    (out,), cast = _call_with_casts(
        functools.partial(_ffn_kernel, final_norm=final_norm), casts,
        (h, g.reshape(1, d), wg, wu, wd, fg.reshape(1, d)),
        grid=(n_rows, n_f),
        in_specs=[pl.BlockSpec((tm, d), lambda i, f: (i, 0)),
                  pl.BlockSpec((1, d), lambda i, f: (0, 0)),
                  pl.BlockSpec((d, tf), lambda i, f: (0, f)),
                  pl.BlockSpec((d, tf), lambda i, f: (0, f)),
                  pl.BlockSpec((tf, d), lambda i, f: (f, 0)),
                  pl.BlockSpec((1, d), lambda i, f: (0, 0))],
        out_specs=[pl.BlockSpec((tm, d), lambda i, f: (i, 0))],
        out_shape=[jax.ShapeDtypeStruct((TOKENS, d), F32)],
        scratch_shapes=[pltpu.VMEM((tm, d), BF16)],
        compiler_params=_cparams(("parallel", "arbitrary"), vmem_mb=BIG_TILE_VMEM_MB),
        name="ffn_residual")
    return out, cast


def _s5_kernel(u_ref, wb_ref, a_ref, wc_ref, d_ref, z_ref, ui_ref, bu_ref, xb_ref, st_ref,
               *, ts, nl):
    @pl.when(pl.program_id(1) == 0)
    def _():
        st_ref[...] = jnp.zeros_like(st_ref)

    blocks = range(nl)
    for l in blocks:
        cols = slice(l * LANE, (l + 1) * LANE)
        for b in range(BATCH):
            ui_ref[l, pl.ds(b, ts, stride=BATCH), :] = u_ref[b, :, cols].astype(F32)
        bu_ref[l] = jnp.dot(ui_ref[l].astype(BF16), wb_ref[l],
                            preferred_element_type=F32)
    coef = [(a_ref[l, 0:1, :], a_ref[l, 1:2, :]) for l in blocks]

    def two_steps(i, carry):
        r0 = pl.ds(pl.multiple_of(i * 2 * BATCH, 2 * BATCH), BATCH)
        r1 = pl.ds(pl.multiple_of(i * 2 * BATCH + BATCH, BATCH), BATCH)
        both = pl.ds(pl.multiple_of(i * 2 * BATCH, 2 * BATCH), 2 * BATCH)
        out = []
        for l in blocks:
            (xr, xi), (ar, ai) = carry[l], coef[l]
            xr1 = ar * xr - ai * xi + bu_ref[l, r0, :S5_SL]
            xi1 = ar * xi + ai * xr + bu_ref[l, r0, S5_SL:]
            xr2 = ar * xr1 - ai * xi1 + bu_ref[l, r1, :S5_SL]
            xi2 = ar * xi1 + ai * xr1 + bu_ref[l, r1, S5_SL:]
            xb_ref[l, both, :S5_SL] = jnp.concatenate([xr1, xr2], axis=0).astype(BF16)
            xb_ref[l, both, S5_SL:] = jnp.concatenate([xi1, xi2], axis=0).astype(BF16)
            out.append((xr2, xi2))
        return tuple(out)

    init = tuple((st_ref[l, :, :S5_SL], st_ref[l, :, S5_SL:]) for l in blocks)
    final = lax.fori_loop(0, ts // 2, two_steps, init, unroll=4)
    for l in blocks:
        st_ref[l, :, :S5_SL] = final[l][0]
        st_ref[l, :, S5_SL:] = final[l][1]
        y = (jnp.dot(xb_ref[l], wc_ref[l], preferred_element_type=F32)
             + d_ref[l] * ui_ref[l])
        ui_ref[l] = jax.nn.gelu(y, approximate=True)
        cols = slice(l * LANE, (l + 1) * LANE)
        for b in range(BATCH):
            z_ref[b, :, cols] = ui_ref[l, pl.ds(b, ts, stride=BATCH), :].astype(BF16)


def s5_scan(proj, ops, *, ts=256, nl=2):
    wb, a, wc, d = ops
    rows = ts * BATCH
    j3 = lambda j, t: (j, 0, 0)
    return pl.pallas_call(
        functools.partial(_s5_kernel, ts=ts, nl=nl),
        grid=(S5_LANE_BLOCKS // nl, SEQ // ts),
        in_specs=[pl.BlockSpec((BATCH, ts, nl * LANE), lambda j, t: (0, t, j)),
                  pl.BlockSpec((nl, LANE, 2 * S5_SL), j3),
                  pl.BlockSpec((nl, 2, S5_SL), j3),
                  pl.BlockSpec((nl, 2 * S5_SL, LANE), j3),
                  pl.BlockSpec((nl, 1, LANE), j3)],
        out_specs=pl.BlockSpec((BATCH, ts, nl * LANE), lambda j, t: (0, t, j)),
        out_shape=jax.ShapeDtypeStruct((BATCH, SEQ, S5_WIDTH), BF16),
        scratch_shapes=[pltpu.VMEM((nl, rows, LANE), F32),
                        pltpu.VMEM((nl, rows, 2 * S5_SL), F32),
                        pltpu.VMEM((nl, rows, 2 * S5_SL), BF16),
                        pltpu.VMEM((nl, BATCH, 2 * S5_SL), F32)],
        compiler_params=_cparams(("parallel", "arbitrary")),
        name="s5_scan",
    )(proj, wb, a, wc, d)


def _s5_operators(a_re, a_im, log_dt, b_re, b_im, c_re, c_im, d_skip):
    dt = jnp.exp(log_dt)[:, None]
    lr, li = a_re, a_im
    mag = jnp.exp(lr * dt)
    ar, ai = mag * jnp.cos(li * dt), mag * jnp.sin(li * dt)
    den = lr * lr + li * li
    zr, zi = ar - 1.0, ai
    fr = (zr * lr + zi * li) / den
    fi = (zi * lr - zr * li) / den
    bbr = fr[..., None] * b_re - fi[..., None] * b_im
    bbi = fr[..., None] * b_im + fi[..., None] * b_re
    nb, gb = S5_LANE_BLOCKS, S5_GB
    eye = jnp.eye(gb, dtype=F32)

    def in_map(bb):
        return jnp.einsum('jgnp,gh->jgphn', bb.reshape(nb, gb, S5_N, S5_P), eye).reshape(
            nb, LANE, S5_SL)

    def out_map(cc):
        return jnp.einsum('jgpn,gh->jhngp', cc.reshape(nb, gb, S5_P, S5_N), eye).reshape(
            nb, S5_SL, LANE)

    wb = jnp.concatenate([in_map(bbr), in_map(bbi)], axis=2).astype(BF16)
    wc = jnp.concatenate([out_map(c_re), out_map(-c_im)], axis=1).astype(BF16)
    a = jnp.stack([ar.reshape(nb, S5_SL), ai.reshape(nb, S5_SL)], axis=1)
    return wb, a, wc, d_skip.reshape(nb, 1, LANE)


def _glu_kernel(z_ref, zc_ref, w_ref, b_ref, o_ref):
    gate = jnp.dot(z_ref[...], w_ref[...], preferred_element_type=F32) + b_ref[...]
    o_ref[...] = (zc_ref[...].astype(F32) * jax.nn.sigmoid(gate)).astype(BF16)


def s5_glu(z, w, b, *, tm=1024, tn=512):
    n = w.shape[1]
    return pl.pallas_call(
        _glu_kernel,
        grid=(TOKENS // tm, n // tn),
        in_specs=[pl.BlockSpec((tm, n), lambda i, j: (i, 0)),
                  pl.BlockSpec((tm, tn), lambda i, j: (i, j)),
                  pl.BlockSpec((n, tn), lambda i, j: (0, j)),
                  pl.BlockSpec((1, tn), lambda i, j: (0, j))],
        out_specs=pl.BlockSpec((tm, tn), lambda i, j: (i, j)),
        out_shape=jax.ShapeDtypeStruct((TOKENS, n), BF16),
        compiler_params=_cparams(("parallel", "arbitrary")),
        name="s5_glu",
    )(z, z, w, b.reshape(1, n))


def _diff_attn_kernel(scal_ref, q_ref, k_ref, v_ref, pq_ref, pk_ref, gn_ref, o_ref,
                      *, rc, out_scale):
    h = pl.program_id(1)
    slope = scal_ref[h]
    lam = scal_ref[DIFF_HEADS]
    lane = lax.broadcasted_iota(jnp.int32, (rc, 2 * DIFF_DQK), 1)
    for c in range(SEQ // rc):
        lo = c * rc
        rows = slice(lo, lo + rc)
        q = q_ref[rows, :].astype(F32) * (DIFF_DQK ** -0.5 * LOG2_E)
        qa = jnp.where(lane < DIFF_DQK, q, 0.0).astype(BF16)
        qb = jnp.where(lane >= DIFF_DQK, q, 0.0).astype(BF16)
        pq = jnp.broadcast_to(pq_ref[rows, :], (rc, LANE))
        bias_d = slope * jnp.abs(_lane_tile(pq, rc) - pk_ref[0, :, lo:lo + rc])
        bias_p = slope * jnp.abs(_lane_tile(pq, lo) - pk_ref[0, :, :lo]) if c else None
        pva = _causal_rows(qa, k_ref, v_ref, c, rc, bias_d, bias_p)
        pvb = _causal_rows(qb, k_ref, v_ref, c, rc, bias_d, bias_p)
        o = (pva[:, :DIFF_DV] / pva[:, DIFF_DV:]
             - lam * (pvb[:, :DIFF_DV] / pvb[:, DIFF_DV:]))
        o_ref[rows, :] = (_rms(o, gn_ref[...]) * out_scale).astype(BF16)


def diff_attention(scal, proj, posq, posk, gn, *, out_scale, rc=512):
    hb = lambda base: (lambda b, h, s: (b, base // DIFF_DV + h))
    return pl.pallas_call(
        functools.partial(_diff_attn_kernel, rc=rc, out_scale=out_scale),
        grid_spec=pltpu.PrefetchScalarGridSpec(
            num_scalar_prefetch=1,
            grid=(BATCH, DIFF_HEADS),
            in_specs=[pl.BlockSpec((SEQ, DIFF_DV), hb(OD_Q)),
                      pl.BlockSpec((SEQ, DIFF_DV), hb(OD_K)),
                      pl.BlockSpec((SEQ, DIFF_DV), hb(OD_V)),
                      pl.BlockSpec((SEQ, 1), lambda b, h, s: (b, 0)),
                      pl.BlockSpec((1, 1, SEQ), lambda b, h, s: (b, 0, 0)),
                      pl.BlockSpec((1, DIFF_DV), lambda b, h, s: (0, 0))],
            out_specs=pl.BlockSpec((SEQ, DIFF_DV), lambda b, h, s: (b, h))),
        out_shape=jax.ShapeDtypeStruct((TOKENS, DIFF_HEADS * DIFF_DV), BF16),
        compiler_params=_cparams(("parallel", "parallel")),
        name="diff_attention",
    )(scal, proj, proj, proj, posq, posk, gn.reshape(1, DIFF_DV))


OD_Q = 1024
OD_K = 2048
OD_V = 3072


def _pack_even_w_in(w):
    cq, ckv, kr, gq, gk, gv, glr, gout = jnp.split(
        w, [512, 1024, 1088, 1600, 2112, 3136, 3152], axis=1)
    half = MLA_ROPE // 2
    kr_dup = jnp.concatenate([kr, kr[:, half:], kr[:, :half]], axis=1)
    glr_pad = jnp.pad(glr, ((0, 0), (0, EV_COLS - EV_GLR - GLA_GATE_RANK)))
    return jnp.concatenate([cq, ckv, gq, gk, gv, gout, kr_dup, glr_pad], axis=1).astype(BF16)


def _pack_w_uq(w):
    w = w.reshape(MLA_RANK, MLA_HEADS, MLA_NOPE + MLA_ROPE)
    half = MLA_ROPE // 2
    t1 = w[..., MLA_NOPE:MLA_NOPE + half]
    t2 = w[..., MLA_NOPE + half:]
    w = jnp.concatenate([w[..., :MLA_NOPE], t1, t2, t2, t1], axis=-1)
    return w.reshape(MLA_RANK, MLA_HEADS * MLA_QK).astype(BF16)


def _rope_tables(positions):
    half = MLA_ROPE // 2
    inv_freq = ROPE_THETA ** (-jnp.arange(half, dtype=F32) / half)
    ang = positions.astype(F32).reshape(TOKENS, 1) * inv_freq
    cos, sin = jnp.cos(ang), jnp.sin(ang)
    zeros = jnp.zeros((TOKENS, 2 * half), F32)
    return (jnp.concatenate([cos, cos, zeros], axis=1),
            jnp.concatenate([-sin, sin, zeros], axis=1))


def _even_mixer(h, g_mix, positions, w_in, q_norm, w_uq, kv_norm, w_ukv, w_gate_up, b_gate,
                g_norm, w_out, ffn_weights, later_weights):
    proj, (w_out, *later) = norm_matmul(h, g_mix, _pack_even_w_in(w_in),
                                        later_weights=(w_out,) + tuple(later_weights))
    ct, st = _rope_tables(positions)
    q, k, v = mla_proj(proj, q_norm, kv_norm, _pack_w_uq(w_uq), w_ukv.astype(BF16), ct, st)
    o_mla, ffn_bf16 = mla_attention(q, k, v, ffn_weights)
    wg = jnp.pad(w_gate_up, ((0, LANE - GLA_GATE_RANK), (0, 0))).astype(BF16)
    o_gla = gla_mixer(proj.reshape(BATCH, SEQ, EV_COLS), wg,
                      b_gate.reshape(1, GLA_HEADS * GLA_DK), g_norm).reshape(TOKENS, -1)
    return out_proj_residual(h, o_mla, o_gla, w_out), ffn_bf16, later


def _odd_mixer(h, g_mix, positions, layer, w_in, a_re, a_im, log_dt, b_re, b_im, c_re, c_im,
               d_skip, w_glu, b_glu, lq1, lk1, lq2, lk2, d_norm, w_out):
    lambda_init = 0.8 - 0.6 * math.exp(-0.3 * layer)
    proj, _ = norm_matmul(h, g_mix, w_in)
    z = s5_scan(proj.reshape(BATCH, SEQ, -1),
                _s5_operators(a_re, a_im, log_dt, b_re, b_im, c_re, c_im, d_skip))
    o_s5 = s5_glu(z.reshape(TOKENS, S5_WIDTH), w_glu.astype(BF16), b_glu)
    lam = (jnp.exp(jnp.sum(lq1 * lk1)) - jnp.exp(jnp.sum(lq2 * lk2)) + lambda_init)
    slopes = jnp.exp2(-8.0 * jnp.arange(1, DIFF_HEADS + 1, dtype=F32) / DIFF_HEADS)
    scal = jnp.concatenate([slopes * LOG2_E, lam.reshape(1)]).astype(F32)
    posf = positions.astype(F32)
    o_diff = diff_attention(scal, proj, posf.reshape(TOKENS, 1), posf.reshape(BATCH, 1, SEQ),
                            d_norm, out_scale=1.0 - lambda_init)
    return out_proj_residual(h, o_s5, o_diff, w_out)


def kernel(x, positions, norm_mix, norm_ffn, final_norm, ffn_w_gate, ffn_w_up, ffn_w_down, ag_w_in, mla_q_norm, mla_w_uq, mla_kv_norm, mla_w_ukv, gla_w_gate_up, gla_b_gate, gla_norm, ag_w_out, cd_w_in, s5_a_re, s5_a_im, s5_log_dt, s5_b_re, s5_b_im, s5_c_re, s5_c_im, s5_d, s5_w_glu, s5_b_glu, diff_lambda_q1, diff_lambda_k1, diff_lambda_q2, diff_lambda_k2, diff_norm, cd_w_out):
    h = x.reshape(TOKENS, D_MODEL)
    ffn_f32 = lambda layer: (layer, ffn_w_gate, ffn_w_up, ffn_w_down)
    ffn_bf16 = odd_bf16 = None
    for layer in range(DEPTH):
        i = layer // 2
        if layer % 2 == 0:
            next_odd = ((cd_w_in, i), (cd_w_out, i)) if layer + 1 < DEPTH else ()
            h, ffn_bf16, odd_bf16 = _even_mixer(
                h, norm_mix[layer], positions, ag_w_in[i], mla_q_norm[i], mla_w_uq[i],
                mla_kv_norm[i], mla_w_ukv[i], gla_w_gate_up[i], gla_b_gate[i], gla_norm[i],
                (ag_w_out, i), ffn_f32(layer), next_odd)
        else:
            w_in, w_out = odd_bf16
            h = _odd_mixer(h, norm_mix[layer], positions, layer, w_in, s5_a_re[i],
                           s5_a_im[i], s5_log_dt[i], s5_b_re[i], s5_b_im[i], s5_c_re[i],
                           s5_c_im[i], s5_d[i], s5_w_glu[i], s5_b_glu[i], diff_lambda_q1[i],
                           diff_lambda_k1[i], diff_lambda_q2[i], diff_lambda_k2[i],
                           diff_norm[i], w_out)
        hosts_next = layer % 2 == 0 and layer + 1 < DEPTH
        h, next_bf16 = ffn_residual(h, norm_ffn[layer], *ffn_bf16, final_norm,
                                    final_norm=(layer == DEPTH - 1),
                                    next_weights=ffn_f32(layer + 1) if hosts_next else None)
        ffn_bf16 = next_bf16
    return h.reshape(BATCH, SEQ, D_MODEL)
```

```python
import functools
import math
from typing import Callable, NamedTuple

import jax
import jax.numpy as jnp
from jax import lax
from jax.experimental import pallas as pl
from jax.experimental.pallas import tpu as pltpu

F32 = jnp.float32
BF16 = jnp.bfloat16

D_MODEL = 2048
BATCH = 8
SEQ = 2048
DEPTH = 2
TOKENS = BATCH * SEQ
MIX_HALF = D_MODEL // 2
RMS_EPS = 1e-6

MLA_NOPE = 128
MLA_ROPE = 64
MLA_V = 128
MLA_HEADS = 8
MLA_RANK = 512
ROPE_THETA = 10000.0
MLA_QK = 256

GLA_HEADS = 4
GLA_DK = 128
GLA_DV = 256
GLA_GATE_RANK = 16
GLA_GATE_NORM = 16.0
GLA_CHUNK = 128

S5_WIDTH = MIX_HALF
S5_P = 16
S5_GROUPS = 64
S5_N = 64
S5_GB = 8
S5_LANE_BLOCKS = S5_GROUPS // S5_GB
S5_SL = S5_GB * S5_N

DIFF_DQK = 64
DIFF_DV = 128
DIFF_HEADS = 8

FFN_HIDDEN = 5632
BIG_TILE_VMEM_MB = 58

LANE = 128
NEG_BIG = -1e30
LOG2_E = math.log2(math.e)

EV_CQ = 0
EV_CKV = 512
EV_GQ = 1024
EV_GK = 1536
EV_GV = 2048
EV_GOUT = 3072
EV_KROPE = 4096
EV_GLR = 4224
EV_COLS = 4608


def _cparams(sem, vmem_mb=48):
    return pltpu.CompilerParams(dimension_semantics=sem,
                                vmem_limit_bytes=vmem_mb * 1024 * 1024)


class _CastJob(NamedTuple):
    w: jax.Array
    layer: int
    block: tuple
    index: Callable


def _call_with_casts(body, casts, args, *, in_specs, out_specs, out_shape, **kwargs):
    n_in, n_out, n_cast = len(in_specs), len(out_specs), len(casts)

    def kernel(*refs):
        ins, rest = refs[:n_in], refs[n_in:]
        cast_in, rest = rest[:n_cast], rest[n_cast:]
        outs, rest = rest[:n_out], rest[n_out:]
        cast_out, scratch = rest[:n_cast], rest[n_cast:]
        for src, dst in zip(cast_in, cast_out):
            dst[...] = src[0].astype(BF16)
        body(*ins, *outs, *scratch)

    cast_in_specs = [pl.BlockSpec((1,) + c.block, lambda *g, c=c: (c.layer,) + tuple(c.index(*g)))
                     for c in casts]
    cast_out_specs = [pl.BlockSpec(c.block, lambda *g, c=c: tuple(c.index(*g))) for c in casts]
    cast_shapes = [jax.ShapeDtypeStruct(c.w.shape[1:], BF16) for c in casts]
    res = pl.pallas_call(kernel, in_specs=list(in_specs) + cast_in_specs,
                         out_specs=list(out_specs) + cast_out_specs,
                         out_shape=list(out_shape) + cast_shapes, **kwargs)(
                             *args, *[c.w for c in casts])
    return res[:n_out], res[n_out:]


def _rms(x, g):
    ms = jnp.mean(x * x, axis=-1, keepdims=True)
    return x * lax.rsqrt(ms + RMS_EPS) * g


def _nt_dot(a, b):
    return lax.dot_general(a, b, (((1,), (1,)), ((), ())), preferred_element_type=F32)


def _norm_matmul_kernel(x_ref, g_ref, w_ref, o_ref, xn_ref):
    @pl.when(pl.program_id(1) == 0)
    def _():
        xn_ref[...] = _rms(x_ref[...].astype(F32), g_ref[...]).astype(BF16)

    o_ref[...] = jnp.dot(xn_ref[...], w_ref[...],
                         preferred_element_type=F32).astype(o_ref.dtype)


def norm_matmul(x, g, w, *, later_weights=(), tm=1024, col_tiles=2):
    t, k = x.shape
    n = w.shape[1]
    tn = n // col_tiles
    steps = t // tm * col_tiles
    casts = [_CastJob(w3, layer, (w3.shape[1] // steps, w3.shape[2]),
                      lambda i, j: (i * col_tiles + j, 0)) for w3, layer in later_weights]
    (out,), cast = _call_with_casts(
        _norm_matmul_kernel, casts, (x, g.reshape(1, k), w),
        grid=(t // tm, col_tiles),
        in_specs=[pl.BlockSpec((tm, k), lambda i, j: (i, 0)),
                  pl.BlockSpec((1, k), lambda i, j: (0, 0)),
                  pl.BlockSpec((k, tn), lambda i, j: (0, j))],
        out_specs=[pl.BlockSpec((tm, tn), lambda i, j: (i, j))],
        out_shape=[jax.ShapeDtypeStruct((t, n), BF16)],
        scratch_shapes=[pltpu.VMEM((tm, k), BF16)],
        compiler_params=_cparams(("parallel", "arbitrary"), vmem_mb=BIG_TILE_VMEM_MB),
        name="norm_matmul")
    return out, cast


def _rope_half(blk, ct, st):
    return blk * ct + pltpu.roll(blk, 64, 1) * st


def _mla_proj_kernel(cq_ref, ckv_ref, kr_ref, gq_ref, gkv_ref, wq_ref, wkv_ref, ct_ref, st_ref,
                     q_ref, k_ref, v_ref, *, scale):
    ct, st = ct_ref[...], st_ref[...]
    xq = _rms(cq_ref[...].astype(F32), gq_ref[...]).astype(BF16)
    xkv = _rms(ckv_ref[...].astype(F32), gkv_ref[...]).astype(BF16)
    kr = _rope_half(kr_ref[...].astype(F32), ct, st).astype(BF16)
    for h in range(MLA_HEADS):
        lo, mid, hi = h * MLA_QK, h * MLA_QK + LANE, (h + 1) * MLA_QK
        a = jnp.dot(xq, wq_ref[:, lo:hi], preferred_element_type=F32)
        q_ref[:, lo:mid] = (a[:, :LANE] * scale).astype(BF16)
        q_ref[:, mid:hi] = (_rope_half(a[:, LANE:], ct, st) * scale).astype(BF16)
        kv = jnp.dot(xkv, wkv_ref[:, lo:hi], preferred_element_type=F32)
        k_ref[:, lo:mid] = kv[:, :LANE].astype(BF16)
        k_ref[:, mid:hi] = kr
        v_ref[:, h * MLA_V:(h + 1) * MLA_V] = kv[:, LANE:].astype(BF16)


def mla_proj(proj, gq, gkv, wq, wkv, ct, st, *, tm=1024):
    scale = (MLA_NOPE + MLA_ROPE) ** -0.5 * LOG2_E
    row = lambda i: (i, 0)
    fixed = lambda i: (0, 0)
    return pl.pallas_call(
        functools.partial(_mla_proj_kernel, scale=scale),
        grid=(TOKENS // tm,),
        in_specs=[pl.BlockSpec((tm, MLA_RANK), lambda i: (i, EV_CQ // MLA_RANK)),
                  pl.BlockSpec((tm, MLA_RANK), lambda i: (i, EV_CKV // MLA_RANK)),
                  pl.BlockSpec((tm, LANE), lambda i: (i, EV_KROPE // LANE)),
                  pl.BlockSpec((1, MLA_RANK), fixed),
                  pl.BlockSpec((1, MLA_RANK), fixed),
                  pl.BlockSpec((MLA_RANK, MLA_HEADS * MLA_QK), fixed),
                  pl.BlockSpec((MLA_RANK, MLA_HEADS * (MLA_NOPE + MLA_V)), fixed),
                  pl.BlockSpec((tm, LANE), row),
                  pl.BlockSpec((tm, LANE), row)],
        out_specs=[pl.BlockSpec((tm, MLA_HEADS * MLA_QK), row),
                   pl.BlockSpec((tm, MLA_HEADS * MLA_QK), row),
                   pl.BlockSpec((tm, MLA_HEADS * MLA_V), row)],
        out_shape=[jax.ShapeDtypeStruct((TOKENS, MLA_HEADS * MLA_QK), BF16),
                   jax.ShapeDtypeStruct((TOKENS, MLA_HEADS * MLA_QK), BF16),
                   jax.ShapeDtypeStruct((TOKENS, MLA_HEADS * MLA_V), BF16)],
        compiler_params=_cparams(("parallel",)),
        name="mla_proj",
    )(proj, proj, proj, gq.reshape(1, MLA_RANK), gkv.reshape(1, MLA_RANK), wq, wkv, ct, st)


def _causal_mask(tq, tk):
    row = lax.broadcasted_iota(jnp.int32, (tq, tk), 0)
    col = lax.broadcasted_iota(jnp.int32, (tq, tk), 1)
    return col <= row


def _lane_tile(x, width):
    return jnp.concatenate([x] * (width // LANE), axis=1)


def _with_ones(v):
    return jnp.concatenate([v, jnp.ones_like(v)], axis=1)


def _softmax_step(s, v_ext, m_ref, l_ref, acc_ref, rows):
    m_prev = m_ref[rows, :]
    m_new = jnp.maximum(m_prev, jnp.max(s, axis=1, keepdims=True))
    p = jnp.exp2(s - _lane_tile(m_new, s.shape[1]))
    alpha = jnp.exp2(m_prev - m_new)
    pv = jnp.dot(p.astype(BF16), v_ext, preferred_element_type=F32)
    dv = acc_ref.shape[1]
    acc_ref[rows, :] = alpha * acc_ref[rows, :] + pv[:, :dv]
    l_ref[rows, :] = alpha * l_ref[rows, :] + pv[:, dv:]
    m_ref[rows, :] = m_new


def _causal_rows(q, k_ref, v_ref, c, rc, bias_d=None, bias_p=None,
                 kcols=slice(None), vcols=slice(None)):
    lo = c * rc
    s_d = _nt_dot(q, k_ref[lo:lo + rc, kcols])
    if bias_d is not None:
        s_d = s_d - bias_d
    s_d = jnp.where(_causal_mask(rc, rc), s_d, NEG_BIG)
    m = jnp.max(s_d, axis=1, keepdims=True)
    if c:
        s_p = _nt_dot(q, k_ref[:lo, kcols])
        if bias_p is not None:
            s_p = s_p - bias_p
        m = jnp.maximum(m, jnp.max(s_p, axis=1, keepdims=True))
    m = jnp.broadcast_to(m, (rc, LANE))
    pv = jnp.dot(jnp.exp2(s_d - _lane_tile(m, rc)).astype(BF16),
                 _with_ones(v_ref[lo:lo + rc, vcols]), preferred_element_type=F32)
    if c:
        pv = pv + jnp.dot(jnp.exp2(s_p - _lane_tile(m, lo)).astype(BF16),
                          _with_ones(v_ref[:lo, vcols]), preferred_element_type=F32)
    return pv


def _mla_attn_kernel(q_ref, k_ref, v_ref, o_ref, *, rc, hp):
    for h in range(hp):
        kcols = slice(h * MLA_QK, (h + 1) * MLA_QK)
        vcols = slice(h * MLA_V, (h + 1) * MLA_V)
        for c in range(SEQ // rc):
            rows = slice(c * rc, (c + 1) * rc)
            pv = _causal_rows(q_ref[rows, kcols], k_ref, v_ref, c, rc, kcols=kcols, vcols=vcols)
            o_ref[rows, vcols] = (pv[:, :MLA_V] / pv[:, MLA_V:]).astype(BF16)


def mla_attention(q, k, v, ffn_weights, *, rc=512, hp=4):
    layer, gate, up, down = ffn_weights
    nh = MLA_HEADS // hp
    rows, split = 16, BATCH * nh // 16
    where = lambda b, h: ((b * nh + h) // split, (b * nh + h) % split)
    up_blk = (gate.shape[1] // rows, gate.shape[2] // split)
    down_blk = (down.shape[1] // rows, down.shape[2] // split)
    casts = [_CastJob(gate, layer, up_blk, where), _CastJob(up, layer, up_blk, where),
             _CastJob(down, layer, down_blk, where)]
    (out,), cast = _call_with_casts(
        functools.partial(_mla_attn_kernel, rc=rc, hp=hp), casts, (q, k, v),
        grid=(BATCH, nh),
        in_specs=[pl.BlockSpec((SEQ, hp * MLA_QK), lambda b, h: (b, h)),
                  pl.BlockSpec((SEQ, hp * MLA_QK), lambda b, h: (b, h)),
                  pl.BlockSpec((SEQ, hp * MLA_V), lambda b, h: (b, h))],
        out_specs=[pl.BlockSpec((SEQ, hp * MLA_V), lambda b, h: (b, h))],
        out_shape=[jax.ShapeDtypeStruct((TOKENS, MLA_HEADS * MLA_V), BF16)],
        compiler_params=_cparams(("parallel", "parallel"), vmem_mb=BIG_TILE_VMEM_MB),
        name="mla_attention")
    return out, cast


def _split3(x):
    hi = x.astype(BF16)
    r1 = x - hi.astype(F32)
    mid = r1.astype(BF16)
    lo = (r1 - mid.astype(F32)).astype(BF16)
    return hi, mid, lo


def _gla_kernel(q_ref, k_ref, v_ref, go_ref, glr_ref, wg_ref, bg_ref, gn_ref, o_ref, st_ref,
                *, chunk):
    @pl.when(pl.program_id(1) == 0)
    def _():
        st_ref[...] = jnp.zeros_like(st_ref)

    mask = _causal_mask(chunk, chunk)
    tri = mask.astype(BF16)
    for b in range(q_ref.shape[0]):
        z = jnp.dot(glr_ref[b], wg_ref[...], preferred_element_type=F32) + bg_ref[...]
        log_a = -(jnp.maximum(-z, 0.0) + jnp.log1p(jnp.exp(-jnp.abs(z)))) / GLA_GATE_NORM
        hi, mid, lo = _split3(log_a)
        cum_all = (jnp.dot(tri, hi, preferred_element_type=F32)
                   + jnp.dot(tri, mid, preferred_element_type=F32)
                   + jnp.dot(tri, lo, preferred_element_type=F32))

        for h in range(GLA_HEADS):
            ks = slice(h * GLA_DK, (h + 1) * GLA_DK)
            vs = slice(h * GLA_DV, (h + 1) * GLA_DV)
            cum = cum_all[:, ks]
            last = cum[chunk - 1:chunk, :]
            q = q_ref[b, :, ks].astype(F32) * GLA_DK ** -0.5
            k = k_ref[b, :, ks].astype(F32)
            v = v_ref[b, :, vs]
            st = st_ref[b, h]
            q_in = (q * jnp.exp(cum)).astype(BF16)
            q_hat = (q * jnp.exp(cum - last)).astype(BF16)
            k_hat = (k * jnp.exp(last - cum)).astype(BF16)

            o = _nt_dot(q_in, st.astype(BF16))
            attn = jnp.where(mask, _nt_dot(q_hat, k_hat), 0.0)
            o = o + jnp.dot(attn.astype(BF16), v, preferred_element_type=F32)
            vt = v.astype(F32).T.astype(BF16)
            st_ref[b, h] = st * jnp.exp(last) + jnp.dot(vt, k_hat, preferred_element_type=F32)

            o = _rms(o, gn_ref[...])
            g = go_ref[b, :, vs].astype(F32)
            o_ref[b, :, vs] = (o * (g * jax.nn.sigmoid(g))).astype(BF16)


def gla_mixer(proj, wg, bg, gn, *, chunk=GLA_CHUNK, nb=2):
    kw, vw = GLA_HEADS * GLA_DK, GLA_HEADS * GLA_DV
    col = lambda base, width: (lambda b, c: (b, c, base // width))
    fixed = lambda b, c: (0, 0)
    return pl.pallas_call(
        functools.partial(_gla_kernel, chunk=chunk),
        grid=(BATCH // nb, SEQ // chunk),
        in_specs=[pl.BlockSpec((nb, chunk, kw), col(EV_GQ, kw)),
                  pl.BlockSpec((nb, chunk, kw), col(EV_GK, kw)),
                  pl.BlockSpec((nb, chunk, vw), col(EV_GV, vw)),
                  pl.BlockSpec((nb, chunk, vw), col(EV_GOUT, vw)),
                  pl.BlockSpec((nb, chunk, LANE), col(EV_GLR, LANE)),
                  pl.BlockSpec((LANE, kw), fixed),
                  pl.BlockSpec((1, kw), fixed),
                  pl.BlockSpec((1, GLA_DV), fixed)],
        out_specs=pl.BlockSpec((nb, chunk, vw), lambda b, c: (b, c, 0)),
        out_shape=jax.ShapeDtypeStruct((BATCH, SEQ, vw), BF16),
        scratch_shapes=[pltpu.VMEM((nb, GLA_HEADS, GLA_DV, GLA_DK), F32)],
        compiler_params=_cparams(("parallel", "arbitrary")),
        name="gla_mixer",
    )(proj, proj, proj, proj, proj, wg, bg, gn.reshape(1, GLA_DV))


def _out_proj_kernel(r_ref, a1_ref, a2_ref, w1_ref, w2_ref, o_ref):
    o_ref[...] = (r_ref[...]
                  + jnp.dot(a1_ref[...], w1_ref[...], preferred_element_type=F32)
                  + jnp.dot(a2_ref[...], w2_ref[...], preferred_element_type=F32))


def out_proj_residual(r, a1, a2, w, *, tm=512):
    half = a1.shape[1]
    n = w.shape[1]
    return pl.pallas_call(
        _out_proj_kernel,
        grid=(TOKENS // tm,),
        in_specs=[pl.BlockSpec((tm, n), lambda i: (i, 0)),
                  pl.BlockSpec((tm, half), lambda i: (i, 0)),
                  pl.BlockSpec((tm, half), lambda i: (i, 0)),
                  pl.BlockSpec((half, n), lambda i: (0, 0)),
                  pl.BlockSpec((half, n), lambda i: (1, 0))],
        out_specs=pl.BlockSpec((tm, n), lambda i: (i, 0)),
        out_shape=jax.ShapeDtypeStruct((TOKENS, n), F32),
        compiler_params=_cparams(("parallel",)),
        name="out_proj_residual",
    )(r, a1, a2, w, w)


def _ffn_kernel(h_ref, g_ref, wg_ref, wu_ref, wd_ref, fg_ref, o_ref, xn_ref, *, final_norm):
    f = pl.program_id(1)

    @pl.when(f == 0)
    def _():
        x = h_ref[...]
        xn_ref[...] = _rms(x, g_ref[...]).astype(BF16)
        o_ref[...] = x

    xn = xn_ref[...]
    a = jnp.dot(xn, wg_ref[...], preferred_element_type=F32)
    b = jnp.dot(xn, wu_ref[...], preferred_element_type=F32)
    act = (a * jax.nn.sigmoid(a) * b).astype(BF16)
    o_ref[...] += jnp.dot(act, wd_ref[...], preferred_element_type=F32)

    if final_norm:
        @pl.when(f == pl.num_programs(1) - 1)
        def _():
            o_ref[...] = _rms(o_ref[...], fg_ref[...])


def ffn_residual(h, g, wg, wu, wd, fg, *, final_norm, next_weights=None, tm=1024, tf=512):
    d = h.shape[1]
    hidden = wg.shape[1]
    n_rows, n_f = TOKENS // tm, hidden // tf
    casts = []
    if next_weights is not None:
        layer, ngate, nup, ndown = next_weights
        up_blk = (d // n_rows, tf)
        casts = [_CastJob(ngate, layer, up_blk, lambda i, f: (i, f)),
                 _CastJob(nup, layer, up_blk, lambda i, f: (i, f)),
                 _CastJob(ndown, layer, (tf, d // n_rows), lambda i, f: (f, i))]
    (out,), cast = _call_with_casts(
        functools.partial(_ffn_kernel, final_norm=final_norm), casts,
        (h, g.reshape(1, d), wg, wu, wd, fg.reshape(1, d)),
        grid=(n_rows, n_f),
        in_specs=[pl.BlockSpec((tm, d), lambda i, f: (i, 0)),
                  pl.BlockSpec((1, d), lambda i, f: (0, 0)),
                  pl.BlockSpec((d, tf), lambda i, f: (0, f)),
                  pl.BlockSpec((d, tf), lambda i, f: (0, f)),
                  pl.BlockSpec((tf, d), lambda i, f: (f, 0)),
                  pl.BlockSpec((1, d), lambda i, f: (0, 0))],
        out_specs=[pl.BlockSpec((tm, d), lambda i, f: (i, 0))],
        out_shape=[jax.ShapeDtypeStruct((TOKENS, d), F32)],
        scratch_shapes=[pltpu.VMEM((tm, d), BF16)],
        compiler_params=_cparams(("parallel", "arbitrary"), vmem_mb=BIG_TILE_VMEM_MB),
        name="ffn_residual")
    return out, cast


def _s5_kernel(u_ref, wb_ref, a_ref, wc_ref, d_ref, z_ref, ui_ref, bu_ref, xb_ref, st_ref,
               *, ts, nl):
    @pl.when(pl.program_id(1) == 0)
    def _():
        st_ref[...] = jnp.zeros_like(st_ref)

    blocks = range(nl)
    for l in blocks:
        cols = slice(l * LANE, (l + 1) * LANE)
        for b in range(BATCH):
            ui_ref[l, pl.ds(b, ts, stride=BATCH), :] = u_ref[b, :, cols].astype(F32)
        bu_ref[l] = jnp.dot(ui_ref[l].astype(BF16), wb_ref[l],
                            preferred_element_type=F32)
    coef = [(a_ref[l, 0:1, :], a_ref[l, 1:2, :]) for l in blocks]

    def two_steps(i, carry):
        r0 = pl.ds(pl.multiple_of(i * 2 * BATCH, 2 * BATCH), BATCH)
        r1 = pl.ds(pl.multiple_of(i * 2 * BATCH + BATCH, BATCH), BATCH)
        both = pl.ds(pl.multiple_of(i * 2 * BATCH, 2 * BATCH), 2 * BATCH)
        out = []
        for l in blocks:
            (xr, xi), (ar, ai) = carry[l], coef[l]
            xr1 = ar * xr - ai * xi + bu_ref[l, r0, :S5_SL]
            xi1 = ar * xi + ai * xr + bu_ref[l, r0, S5_SL:]
            xr2 = ar * xr1 - ai * xi1 + bu_ref[l, r1, :S5_SL]
            xi2 = ar * xi1 + ai * xr1 + bu_ref[l, r1, S5_SL:]
            xb_ref[l, both, :S5_SL] = jnp.concatenate([xr1, xr2], axis=0).astype(BF16)
            xb_ref[l, both, S5_SL:] = jnp.concatenate([xi1, xi2], axis=0).astype(BF16)
            out.append((xr2, xi2))
        return tuple(out)

    init = tuple((st_ref[l, :, :S5_SL], st_ref[l, :, S5_SL:]) for l in blocks)
    final = lax.fori_loop(0, ts // 2, two_steps, init, unroll=4)
    for l in blocks:
        st_ref[l, :, :S5_SL] = final[l][0]
        st_ref[l, :, S5_SL:] = final[l][1]
        y = (jnp.dot(xb_ref[l], wc_ref[l], preferred_element_type=F32)
             + d_ref[l] * ui_ref[l])
        ui_ref[l] = jax.nn.gelu(y, approximate=True)
        cols = slice(l * LANE, (l + 1) * LANE)
        for b in range(BATCH):
            z_ref[b, :, cols] = ui_ref[l, pl.ds(b, ts, stride=BATCH), :].astype(BF16)


def s5_scan(proj, ops, *, ts=256, nl=2):
    wb, a, wc, d = ops
    rows = ts * BATCH
    j3 = lambda j, t: (j, 0, 0)
    return pl.pallas_call(
        functools.partial(_s5_kernel, ts=ts, nl=nl),
        grid=(S5_LANE_BLOCKS // nl, SEQ // ts),
        in_specs=[pl.BlockSpec((BATCH, ts, nl * LANE), lambda j, t: (0, t, j)),
                  pl.BlockSpec((nl, LANE, 2 * S5_SL), j3),
                  pl.BlockSpec((nl, 2, S5_SL), j3),
                  pl.BlockSpec((nl, 2 * S5_SL, LANE), j3),
                  pl.BlockSpec((nl, 1, LANE), j3)],
        out_specs=pl.BlockSpec((BATCH, ts, nl * LANE), lambda j, t: (0, t, j)),
        out_shape=jax.ShapeDtypeStruct((BATCH, SEQ, S5_WIDTH), BF16),
        scratch_shapes=[pltpu.VMEM((nl, rows, LANE), F32),
                        pltpu.VMEM((nl, rows, 2 * S5_SL), F32),
                        pltpu.VMEM((nl, rows, 2 * S5_SL), BF16),
                        pltpu.VMEM((nl, BATCH, 2 * S5_SL), F32)],
        compiler_params=_cparams(("parallel", "arbitrary")),
        name="s5_scan",
    )(proj, wb, a, wc, d)


def _s5_operators(a_re, a_im, log_dt, b_re, b_im, c_re, c_im, d_skip):
    dt = jnp.exp(log_dt)[:, None]
    lr, li = a_re, a_im
    mag = jnp.exp(lr * dt)
    ar, ai = mag * jnp.cos(li * dt), mag * jnp.sin(li * dt)
    den = lr * lr + li * li
    zr, zi = ar - 1.0, ai
    fr = (zr * lr + zi * li) / den
    fi = (zi * lr - zr * li) / den
    bbr = fr[..., None] * b_re - fi[..., None] * b_im
    bbi = fr[..., None] * b_im + fi[..., None] * b_re
    nb, gb = S5_LANE_BLOCKS, S5_GB
    eye = jnp.eye(gb, dtype=F32)

    def in_map(bb):
        return jnp.einsum('jgnp,gh->jgphn', bb.reshape(nb, gb, S5_N, S5_P), eye).reshape(
            nb, LANE, S5_SL)

    def out_map(cc):
        return jnp.einsum('jgpn,gh->jhngp', cc.reshape(nb, gb, S5_P, S5_N), eye).reshape(
            nb, S5_SL, LANE)

    wb = jnp.concatenate([in_map(bbr), in_map(bbi)], axis=2).astype(BF16)
    wc = jnp.concatenate([out_map(c_re), out_map(-c_im)], axis=1).astype(BF16)
    a = jnp.stack([ar.reshape(nb, S5_SL), ai.reshape(nb, S5_SL)], axis=1)
    return wb, a, wc, d_skip.reshape(nb, 1, LANE)


def _glu_kernel(z_ref, zc_ref, w_ref, b_ref, o_ref):
    gate = jnp.dot(z_ref[...], w_ref[...], preferred_element_type=F32) + b_ref[...]
    o_ref[...] = (zc_ref[...].astype(F32) * jax.nn.sigmoid(gate)).astype(BF16)


def s5_glu(z, w, b, *, tm=1024, tn=512):
    n = w.shape[1]
    return pl.pallas_call(
        _glu_kernel,
        grid=(TOKENS // tm, n // tn),
        in_specs=[pl.BlockSpec((tm, n), lambda i, j: (i, 0)),
                  pl.BlockSpec((tm, tn), lambda i, j: (i, j)),
                  pl.BlockSpec((n, tn), lambda i, j: (0, j)),
                  pl.BlockSpec((1, tn), lambda i, j: (0, j))],
        out_specs=pl.BlockSpec((tm, tn), lambda i, j: (i, j)),
        out_shape=jax.ShapeDtypeStruct((TOKENS, n), BF16),
        compiler_params=_cparams(("parallel", "arbitrary")),
        name="s5_glu",
    )(z, z, w, b.reshape(1, n))


def _diff_attn_kernel(scal_ref, q_ref, k_ref, v_ref, pq_ref, pk_ref, gn_ref, o_ref,
                      *, rc, hp, out_scale):
    lam = scal_ref[DIFF_HEADS]
    lane = lax.broadcasted_iota(jnp.int32, (rc, 2 * DIFF_DQK), 1)
    for c in range(SEQ // rc):
        lo = c * rc
        rows = slice(lo, lo + rc)
        pq = jnp.broadcast_to(pq_ref[rows, :], (rc, LANE))
        dist_d = jnp.abs(_lane_tile(pq, rc) - pk_ref[0, :, lo:lo + rc])
        dist_p = jnp.abs(_lane_tile(pq, lo) - pk_ref[0, :, :lo]) if c else None
        for h in range(hp):
            cols = slice(h * DIFF_DV, (h + 1) * DIFF_DV)
            slope = scal_ref[pl.program_id(1) * hp + h]
            q = q_ref[rows, cols].astype(F32) * (DIFF_DQK ** -0.5 * LOG2_E)
            qa = jnp.where(lane < DIFF_DQK, q, 0.0).astype(BF16)
            qb = jnp.where(lane >= DIFF_DQK, q, 0.0).astype(BF16)
            bias_d = slope * dist_d
            bias_p = slope * dist_p if c else None
            pva = _causal_rows(qa, k_ref, v_ref, c, rc, bias_d, bias_p, cols, cols)
            pvb = _causal_rows(qb, k_ref, v_ref, c, rc, bias_d, bias_p, cols, cols)
            o = (pva[:, :DIFF_DV] / pva[:, DIFF_DV:]
                 - lam * (pvb[:, :DIFF_DV] / pvb[:, DIFF_DV:]))
            o_ref[rows, cols] = (_rms(o, gn_ref[...]) * out_scale).astype(BF16)


def diff_attention(scal, proj, posq, posk, gn, *, out_scale, rc=512, hp=4):
    width = hp * DIFF_DV
    hb = lambda base: (lambda b, h, s: (b, base // width + h))
    return pl.pallas_call(
        functools.partial(_diff_attn_kernel, rc=rc, hp=hp, out_scale=out_scale),
        grid_spec=pltpu.PrefetchScalarGridSpec(
            num_scalar_prefetch=1,
            grid=(BATCH, DIFF_HEADS // hp),
            in_specs=[pl.BlockSpec((SEQ, width), hb(OD_Q)),
                      pl.BlockSpec((SEQ, width), hb(OD_K)),
                      pl.BlockSpec((SEQ, width), hb(OD_V)),
                      pl.BlockSpec((SEQ, 1), lambda b, h, s: (b, 0)),
                      pl.BlockSpec((1, 1, SEQ), lambda b, h, s: (b, 0, 0)),
                      pl.BlockSpec((1, DIFF_DV), lambda b, h, s: (0, 0))],
            out_specs=pl.BlockSpec((SEQ, width), lambda b, h, s: (b, h))),
        out_shape=jax.ShapeDtypeStruct((TOKENS, DIFF_HEADS * DIFF_DV), BF16),
        compiler_params=_cparams(("parallel", "parallel")),
        name="diff_attention",
    )(scal, proj, proj, proj, posq, posk, gn.reshape(1, DIFF_DV))


OD_Q = 1024
OD_K = 2048
OD_V = 3072


def _pack_even_w_in(w):
    cq, ckv, kr, gq, gk, gv, glr, gout = jnp.split(
        w, [512, 1024, 1088, 1600, 2112, 3136, 3152], axis=1)
    half = MLA_ROPE // 2
    kr_dup = jnp.concatenate([kr, kr[:, half:], kr[:, :half]], axis=1)
    glr_pad = jnp.pad(glr, ((0, 0), (0, EV_COLS - EV_GLR - GLA_GATE_RANK)))
    return jnp.concatenate([cq, ckv, gq, gk, gv, gout, kr_dup, glr_pad], axis=1).astype(BF16)


def _pack_w_uq(w):
    w = w.reshape(MLA_RANK, MLA_HEADS, MLA_NOPE + MLA_ROPE)
    half = MLA_ROPE // 2
    t1 = w[..., MLA_NOPE:MLA_NOPE + half]
    t2 = w[..., MLA_NOPE + half:]
    w = jnp.concatenate([w[..., :MLA_NOPE], t1, t2, t2, t1], axis=-1)
    return w.reshape(MLA_RANK, MLA_HEADS * MLA_QK).astype(BF16)


def _rope_tables(positions):
    half = MLA_ROPE // 2
    inv_freq = ROPE_THETA ** (-jnp.arange(half, dtype=F32) / half)
    zeros = jnp.zeros((2 * half,), F32)
    freq = jnp.concatenate([inv_freq, inv_freq, zeros])
    keep = jnp.concatenate([jnp.ones((2 * half,), F32), zeros])
    sign = jnp.concatenate([-jnp.ones((half,), F32), jnp.ones((half,), F32), zeros])
    ang = positions.astype(F32).reshape(TOKENS, 1) * freq
    return jnp.cos(ang) * keep, jnp.sin(ang) * sign


def _even_mixer(h, g_mix, positions, w_in, q_norm, w_uq, kv_norm, w_ukv, w_gate_up, b_gate,
                g_norm, w_out, ffn_weights, later_weights):
    proj, (w_out, *later) = norm_matmul(h, g_mix, _pack_even_w_in(w_in),
                                        later_weights=(w_out,) + tuple(later_weights))
    ct, st = _rope_tables(positions)
    q, k, v = mla_proj(proj, q_norm, kv_norm, _pack_w_uq(w_uq), w_ukv.astype(BF16), ct, st)
    o_mla, ffn_bf16 = mla_attention(q, k, v, ffn_weights)
    wg = jnp.pad(w_gate_up, ((0, LANE - GLA_GATE_RANK), (0, 0))).astype(BF16)
    o_gla = gla_mixer(proj.reshape(BATCH, SEQ, EV_COLS), wg,
                      b_gate.reshape(1, GLA_HEADS * GLA_DK), g_norm).reshape(TOKENS, -1)
    return out_proj_residual(h, o_mla, o_gla, w_out), ffn_bf16, later


def _odd_mixer(h, g_mix, positions, layer, w_in, a_re, a_im, log_dt, b_re, b_im, c_re, c_im,
               d_skip, w_glu, b_glu, lq1, lk1, lq2, lk2, d_norm, w_out):
    lambda_init = 0.8 - 0.6 * math.exp(-0.3 * layer)
    proj, _ = norm_matmul(h, g_mix, w_in)
    z = s5_scan(proj.reshape(BATCH, SEQ, -1),
                _s5_operators(a_re, a_im, log_dt, b_re, b_im, c_re, c_im, d_skip))
    o_s5 = s5_glu(z.reshape(TOKENS, S5_WIDTH), w_glu.astype(BF16), b_glu)
    lam = (jnp.exp(jnp.sum(lq1 * lk1)) - jnp.exp(jnp.sum(lq2 * lk2)) + lambda_init)
    slopes = jnp.exp2(-8.0 * jnp.arange(1, DIFF_HEADS + 1, dtype=F32) / DIFF_HEADS)
    scal = jnp.concatenate([slopes * LOG2_E, lam.reshape(1)]).astype(F32)
    posf = positions.astype(F32)
    o_diff = diff_attention(scal, proj, posf.reshape(TOKENS, 1), posf.reshape(BATCH, 1, SEQ),
                            d_norm, out_scale=1.0 - lambda_init)
    return out_proj_residual(h, o_s5, o_diff, w_out)


def kernel(x, positions, norm_mix, norm_ffn, final_norm, ffn_w_gate, ffn_w_up, ffn_w_down, ag_w_in, mla_q_norm, mla_w_uq, mla_kv_norm, mla_w_ukv, gla_w_gate_up, gla_b_gate, gla_norm, ag_w_out, cd_w_in, s5_a_re, s5_a_im, s5_log_dt, s5_b_re, s5_b_im, s5_c_re, s5_c_im, s5_d, s5_w_glu, s5_b_glu, diff_lambda_q1, diff_lambda_k1, diff_lambda_q2, diff_lambda_k2, diff_norm, cd_w_out):
    h = x.reshape(TOKENS, D_MODEL)
    ffn_f32 = lambda layer: (layer, ffn_w_gate, ffn_w_up, ffn_w_down)
    ffn_bf16 = odd_bf16 = None
    for layer in range(DEPTH):
        i = layer // 2
        if layer % 2 == 0:
            next_odd = ((cd_w_in, i), (cd_w_out, i)) if layer + 1 < DEPTH else ()
            h, ffn_bf16, odd_bf16 = _even_mixer(
                h, norm_mix[layer], positions, ag_w_in[i], mla_q_norm[i], mla_w_uq[i],
                mla_kv_norm[i], mla_w_ukv[i], gla_w_gate_up[i], gla_b_gate[i], gla_norm[i],
                (ag_w_out, i), ffn_f32(layer), next_odd)
        else:
            w_in, w_out = odd_bf16
            h = _odd_mixer(h, norm_mix[layer], positions, layer, w_in, s5_a_re[i],
                           s5_a_im[i], s5_log_dt[i], s5_b_re[i], s5_b_im[i], s5_c_re[i],
                           s5_c_im[i], s5_d[i], s5_w_glu[i], s5_b_glu[i], diff_lambda_q1[i],
                           diff_lambda_k1[i], diff_lambda_q2[i], diff_lambda_k2[i],
                           diff_norm[i], w_out)
        hosts_next = layer % 2 == 0 and layer + 1 < DEPTH
        h, next_bf16 = ffn_residual(h, norm_ffn[layer], *ffn_bf16, final_norm,
                                    final_norm=(layer == DEPTH - 1),
                                    next_weights=ffn_f32(layer + 1) if hosts_next else None)
        ffn_bf16 = next_bf16
    return h.reshape(BATCH, SEQ, D_MODEL)
```

```python
import functools
import math
from typing import Callable, NamedTuple

import jax
import jax.numpy as jnp
from jax import lax
from jax.experimental import pallas as pl
from jax.experimental.pallas import tpu as pltpu

F32 = jnp.float32
BF16 = jnp.bfloat16

D_MODEL = 2048
BATCH = 8
SEQ = 2048
DEPTH = 2
TOKENS = BATCH * SEQ
MIX_HALF = D_MODEL // 2
RMS_EPS = 1e-6

MLA_NOPE = 128
MLA_ROPE = 64
MLA_V = 128
MLA_HEADS = 8
MLA_RANK = 512
ROPE_THETA = 10000.0
MLA_QK = 256

GLA_HEADS = 4
GLA_DK = 128
GLA_DV = 256
GLA_GATE_RANK = 16
GLA_GATE_NORM = 16.0
GLA_CHUNK = 128

S5_WIDTH = MIX_HALF
S5_P = 16
S5_GROUPS = 64
S5_N = 64
S5_GB = 8
S5_LANE_BLOCKS = S5_GROUPS // S5_GB
S5_SL = S5_GB * S5_N

DIFF_DQK = 64
DIFF_DV = 128
DIFF_HEADS = 8

FFN_HIDDEN = 5632
BIG_TILE_VMEM_MB = 58

LANE = 128
NEG_BIG = -1e30
LOG2_E = math.log2(math.e)

EV_CQ = 0
EV_CKV = 512
EV_GQ = 1024
EV_GK = 1536
EV_GV = 2048
EV_GOUT = 3072
EV_KROPE = 4096
EV_GLR = 4224
EV_COLS = 4608


def _cparams(sem, vmem_mb=48):
    return pltpu.CompilerParams(dimension_semantics=sem,
                                vmem_limit_bytes=vmem_mb * 1024 * 1024)


class _CastJob(NamedTuple):
    w: jax.Array
    layer: int
    block: tuple
    index: Callable


def _call_with_casts(body, casts, args, *, in_specs, out_specs, out_shape, **kwargs):
    n_in, n_out, n_cast = len(in_specs), len(out_specs), len(casts)

    def kernel(*refs):
        ins, rest = refs[:n_in], refs[n_in:]
        cast_in, rest = rest[:n_cast], rest[n_cast:]
        outs, rest = rest[:n_out], rest[n_out:]
        cast_out, scratch = rest[:n_cast], rest[n_cast:]
        for src, dst in zip(cast_in, cast_out):
            dst[...] = src[0].astype(BF16)
        body(*ins, *outs, *scratch)

    cast_in_specs = [pl.BlockSpec((1,) + c.block, lambda *g, c=c: (c.layer,) + tuple(c.index(*g)))
                     for c in casts]
    cast_out_specs = [pl.BlockSpec(c.block, lambda *g, c=c: tuple(c.index(*g))) for c in casts]
    cast_shapes = [jax.ShapeDtypeStruct(c.w.shape[1:], BF16) for c in casts]
    res = pl.pallas_call(kernel, in_specs=list(in_specs) + cast_in_specs,
                         out_specs=list(out_specs) + cast_out_specs,
                         out_shape=list(out_shape) + cast_shapes, **kwargs)(
                             *args, *[c.w for c in casts])
    return res[:n_out], res[n_out:]


def _rms(x, g):
    ms = jnp.mean(x * x, axis=-1, keepdims=True)
    return x * lax.rsqrt(ms + RMS_EPS) * g


def _nt_dot(a, b):
    return lax.dot_general(a, b, (((1,), (1,)), ((), ())), preferred_element_type=F32)


def _norm_matmul_kernel(x_ref, g_ref, w_ref, o_ref, xn_ref):
    @pl.when(pl.program_id(1) == 0)
    def _():
        xn_ref[...] = _rms(x_ref[...].astype(F32), g_ref[...]).astype(BF16)

    o_ref[...] = jnp.dot(xn_ref[...], w_ref[...],
                         preferred_element_type=F32).astype(o_ref.dtype)


def norm_matmul(x, g, w, *, later_weights=(), tm=1024, col_tiles=2):
    t, k = x.shape
    n = w.shape[1]
    tn = n // col_tiles
    steps = t // tm * col_tiles
    casts = [_CastJob(w3, layer, (w3.shape[1] // steps, w3.shape[2]),
                      lambda i, j: (i * col_tiles + j, 0)) for w3, layer in later_weights]
    (out,), cast = _call_with_casts(
        _norm_matmul_kernel, casts, (x, g.reshape(1, k), w),
        grid=(t // tm, col_tiles),
        in_specs=[pl.BlockSpec((tm, k), lambda i, j: (i, 0)),
                  pl.BlockSpec((1, k), lambda i, j: (0, 0)),
                  pl.BlockSpec((k, tn), lambda i, j: (0, j))],
        out_specs=[pl.BlockSpec((tm, tn), lambda i, j: (i, j))],
        out_shape=[jax.ShapeDtypeStruct((t, n), BF16)],
        scratch_shapes=[pltpu.VMEM((tm, k), BF16)],
        compiler_params=_cparams(("parallel", "arbitrary"), vmem_mb=BIG_TILE_VMEM_MB),
        name="norm_matmul")
    return out, cast


def _rope_half(blk, ct, st):
    return blk * ct + pltpu.roll(blk, 64, 1) * st


def _mla_proj_kernel(cq_ref, ckv_ref, kr_ref, gq_ref, gkv_ref, wq_ref, wkv_ref, ct_ref, st_ref,
                     q_ref, k_ref, v_ref, *, scale):
    ct, st = ct_ref[...], st_ref[...]
    xq = _rms(cq_ref[...].astype(F32), gq_ref[...]).astype(BF16)
    xkv = _rms(ckv_ref[...].astype(F32), gkv_ref[...]).astype(BF16)
    kr = _rope_half(kr_ref[...].astype(F32), ct, st).astype(BF16)
    for h in range(MLA_HEADS):
        lo, mid, hi = h * MLA_QK, h * MLA_QK + LANE, (h + 1) * MLA_QK
        a = jnp.dot(xq, wq_ref[:, lo:hi], preferred_element_type=F32)
        q_ref[:, lo:mid] = (a[:, :LANE] * scale).astype(BF16)
        q_ref[:, mid:hi] = (_rope_half(a[:, LANE:], ct, st) * scale).astype(BF16)
        kv = jnp.dot(xkv, wkv_ref[:, lo:hi], preferred_element_type=F32)
        k_ref[:, lo:mid] = kv[:, :LANE].astype(BF16)
        k_ref[:, mid:hi] = kr
        v_ref[:, h * MLA_V:(h + 1) * MLA_V] = kv[:, LANE:].astype(BF16)


def mla_proj(proj, gq, gkv, wq, wkv, ct, st, *, tm=1024):
    scale = (MLA_NOPE + MLA_ROPE) ** -0.5 * LOG2_E
    row = lambda i: (i, 0)
    fixed = lambda i: (0, 0)
    return pl.pallas_call(
        functools.partial(_mla_proj_kernel, scale=scale),
        grid=(TOKENS // tm,),
        in_specs=[pl.BlockSpec((tm, MLA_RANK), lambda i: (i, EV_CQ // MLA_RANK)),
                  pl.BlockSpec((tm, MLA_RANK), lambda i: (i, EV_CKV // MLA_RANK)),
                  pl.BlockSpec((tm, LANE), lambda i: (i, EV_KROPE // LANE)),
                  pl.BlockSpec((1, MLA_RANK), fixed),
                  pl.BlockSpec((1, MLA_RANK), fixed),
                  pl.BlockSpec((MLA_RANK, MLA_HEADS * MLA_QK), fixed),
                  pl.BlockSpec((MLA_RANK, MLA_HEADS * (MLA_NOPE + MLA_V)), fixed),
                  pl.BlockSpec((tm, LANE), row),
                  pl.BlockSpec((tm, LANE), row)],
        out_specs=[pl.BlockSpec((tm, MLA_HEADS * MLA_QK), row),
                   pl.BlockSpec((tm, MLA_HEADS * MLA_QK), row),
                   pl.BlockSpec((tm, MLA_HEADS * MLA_V), row)],
        out_shape=[jax.ShapeDtypeStruct((TOKENS, MLA_HEADS * MLA_QK), BF16),
                   jax.ShapeDtypeStruct((TOKENS, MLA_HEADS * MLA_QK), BF16),
                   jax.ShapeDtypeStruct((TOKENS, MLA_HEADS * MLA_V), BF16)],
        compiler_params=_cparams(("parallel",)),
        name="mla_proj",
    )(proj, proj, proj, gq.reshape(1, MLA_RANK), gkv.reshape(1, MLA_RANK), wq, wkv, ct, st)


def _causal_mask(tq, tk):
    row = lax.broadcasted_iota(jnp.int32, (tq, tk), 0)
    col = lax.broadcasted_iota(jnp.int32, (tq, tk), 1)
    return col <= row


def _lane_tile(x, width):
    return jnp.concatenate([x] * (width // LANE), axis=1)


def _with_ones(v):
    return jnp.concatenate([v, jnp.ones_like(v)], axis=1)


def _softmax_step(s, v_ext, m_ref, l_ref, acc_ref, rows):
    m_prev = m_ref[rows, :]
    m_new = jnp.maximum(m_prev, jnp.max(s, axis=1, keepdims=True))
    p = jnp.exp2(s - _lane_tile(m_new, s.shape[1]))
    alpha = jnp.exp2(m_prev - m_new)
    pv = jnp.dot(p.astype(BF16), v_ext, preferred_element_type=F32)
    dv = acc_ref.shape[1]
    acc_ref[rows, :] = alpha * acc_ref[rows, :] + pv[:, :dv]
    l_ref[rows, :] = alpha * l_ref[rows, :] + pv[:, dv:]
    m_ref[rows, :] = m_new


def _causal_rows(q, k_ref, v_ref, c, rc, bias_d=None, bias_p=None,
                 kcols=slice(None), vcols=slice(None)):
    lo = c * rc
    s_d = _nt_dot(q, k_ref[lo:lo + rc, kcols])
    if bias_d is not None:
        s_d = s_d - bias_d
    s_d = jnp.where(_causal_mask(rc, rc), s_d, NEG_BIG)
    m = jnp.max(s_d, axis=1, keepdims=True)
    if c:
        s_p = _nt_dot(q, k_ref[:lo, kcols])
        if bias_p is not None:
            s_p = s_p - bias_p
        m = jnp.maximum(m, jnp.max(s_p, axis=1, keepdims=True))
    m = jnp.broadcast_to(m, (rc, LANE))
    pv = jnp.dot(jnp.exp2(s_d - _lane_tile(m, rc)).astype(BF16),
                 _with_ones(v_ref[lo:lo + rc, vcols]), preferred_element_type=F32)
    if c:
        pv = pv + jnp.dot(jnp.exp2(s_p - _lane_tile(m, lo)).astype(BF16),
                          _with_ones(v_ref[:lo, vcols]), preferred_element_type=F32)
    return pv


def _mla_attn_kernel(q_ref, k_ref, v_ref, o_ref, *, rc, hp):
    for h in range(hp):
        kcols = slice(h * MLA_QK, (h + 1) * MLA_QK)
        vcols = slice(h * MLA_V, (h + 1) * MLA_V)
        for c in range(SEQ // rc):
            rows = slice(c * rc, (c + 1) * rc)
            pv = _causal_rows(q_ref[rows, kcols], k_ref, v_ref, c, rc, kcols=kcols, vcols=vcols)
            o_ref[rows, vcols] = (pv[:, :MLA_V] / pv[:, MLA_V:]).astype(BF16)


def mla_attention(q, k, v, ffn_weights, *, rc=512, hp=4):
    layer, gate, up, down = ffn_weights
    nh = MLA_HEADS // hp
    rows, split = 16, BATCH * nh // 16
    where = lambda b, h: ((b * nh + h) // split, (b * nh + h) % split)
    up_blk = (gate.shape[1] // rows, gate.shape[2] // split)
    down_blk = (down.shape[1] // rows, down.shape[2] // split)
    casts = [_CastJob(gate, layer, up_blk, where), _CastJob(up, layer, up_blk, where),
             _CastJob(down, layer, down_blk, where)]
    (out,), cast = _call_with_casts(
        functools.partial(_mla_attn_kernel, rc=rc, hp=hp), casts, (q, k, v),
        grid=(BATCH, nh),
        in_specs=[pl.BlockSpec((SEQ, hp * MLA_QK), lambda b, h: (b, h)),
                  pl.BlockSpec((SEQ, hp * MLA_QK), lambda b, h: (b, h)),
                  pl.BlockSpec((SEQ, hp * MLA_V), lambda b, h: (b, h))],
        out_specs=[pl.BlockSpec((SEQ, hp * MLA_V), lambda b, h: (b, h))],
        out_shape=[jax.ShapeDtypeStruct((TOKENS, MLA_HEADS * MLA_V), BF16)],
        compiler_params=_cparams(("parallel", "parallel"), vmem_mb=BIG_TILE_VMEM_MB),
        name="mla_attention")
    return out, cast


def _split3(x):
    hi = x.astype(BF16)
    r1 = x - hi.astype(F32)
    mid = r1.astype(BF16)
    lo = (r1 - mid.astype(F32)).astype(BF16)
    return hi, mid, lo


def _gla_kernel(q_ref, k_ref, v_ref, go_ref, glr_ref, wg_ref, bg_ref, gn_ref, o_ref, st_ref,
                *, chunk):
    @pl.when(pl.program_id(1) == 0)
    def _():
        st_ref[...] = jnp.zeros_like(st_ref)

    mask = _causal_mask(chunk, chunk)
    tri = mask.astype(BF16)
    for b in range(q_ref.shape[0]):
        z = jnp.dot(glr_ref[b], wg_ref[...], preferred_element_type=F32) + bg_ref[...]
        log_a = -(jnp.maximum(-z, 0.0) + jnp.log1p(jnp.exp(-jnp.abs(z)))) / GLA_GATE_NORM
        hi, mid, lo = _split3(log_a)
        cum_all = (jnp.dot(tri, hi, preferred_element_type=F32)
                   + jnp.dot(tri, mid, preferred_element_type=F32)
                   + jnp.dot(tri, lo, preferred_element_type=F32))

        for h in range(GLA_HEADS):
            ks = slice(h * GLA_DK, (h + 1) * GLA_DK)
            vs = slice(h * GLA_DV, (h + 1) * GLA_DV)
            cum = cum_all[:, ks]
            last = cum[chunk - 1:chunk, :]
            q = q_ref[b, :, ks].astype(F32) * GLA_DK ** -0.5
            k = k_ref[b, :, ks].astype(F32)
            v = v_ref[b, :, vs]
            st = st_ref[b, h]
            q_in = (q * jnp.exp(cum)).astype(BF16)
            q_hat = (q * jnp.exp(cum - last)).astype(BF16)
            k_hat = (k * jnp.exp(last - cum)).astype(BF16)

            o = _nt_dot(q_in, st.astype(BF16))
            attn = jnp.where(mask, _nt_dot(q_hat, k_hat), 0.0)
            o = o + jnp.dot(attn.astype(BF16), v, preferred_element_type=F32)
            vt = v.astype(F32).T.astype(BF16)
            st_ref[b, h] = st * jnp.exp(last) + jnp.dot(vt, k_hat, preferred_element_type=F32)

            o = _rms(o, gn_ref[...])
            g = go_ref[b, :, vs].astype(F32)
            o_ref[b, :, vs] = (o * (g * jax.nn.sigmoid(g))).astype(BF16)


def gla_mixer(proj, wg, bg, gn, *, chunk=GLA_CHUNK, nb=4):
    kw, vw = GLA_HEADS * GLA_DK, GLA_HEADS * GLA_DV
    col = lambda base, width: (lambda b, c: (b, c, base // width))
    fixed = lambda b, c: (0, 0)
    return pl.pallas_call(
        functools.partial(_gla_kernel, chunk=chunk),
        grid=(BATCH // nb, SEQ // chunk),
        in_specs=[pl.BlockSpec((nb, chunk, kw), col(EV_GQ, kw)),
                  pl.BlockSpec((nb, chunk, kw), col(EV_GK, kw)),
                  pl.BlockSpec((nb, chunk, vw), col(EV_GV, vw)),
                  pl.BlockSpec((nb, chunk, vw), col(EV_GOUT, vw)),
                  pl.BlockSpec((nb, chunk, LANE), col(EV_GLR, LANE)),
                  pl.BlockSpec((LANE, kw), fixed),
                  pl.BlockSpec((1, kw), fixed),
                  pl.BlockSpec((1, GLA_DV), fixed)],
        out_specs=pl.BlockSpec((nb, chunk, vw), lambda b, c: (b, c, 0)),
        out_shape=jax.ShapeDtypeStruct((BATCH, SEQ, vw), BF16),
        scratch_shapes=[pltpu.VMEM((nb, GLA_HEADS, GLA_DV, GLA_DK), F32)],
        compiler_params=_cparams(("parallel", "arbitrary")),
        name="gla_mixer",
    )(proj, proj, proj, proj, proj, wg, bg, gn.reshape(1, GLA_DV))


def _out_proj_kernel(r_ref, a1_ref, a2_ref, w1_ref, w2_ref, o_ref):
    o_ref[...] = (r_ref[...]
                  + jnp.dot(a1_ref[...], w1_ref[...], preferred_element_type=F32)
                  + jnp.dot(a2_ref[...], w2_ref[...], preferred_element_type=F32))


def out_proj_residual(r, a1, a2, w, *, tm=512):
    half = a1.shape[1]
    n = w.shape[1]
    return pl.pallas_call(
        _out_proj_kernel,
        grid=(TOKENS // tm,),
        in_specs=[pl.BlockSpec((tm, n), lambda i: (i, 0)),
                  pl.BlockSpec((tm, half), lambda i: (i, 0)),
                  pl.BlockSpec((tm, half), lambda i: (i, 0)),
                  pl.BlockSpec((half, n), lambda i: (0, 0)),
                  pl.BlockSpec((half, n), lambda i: (1, 0))],
        out_specs=pl.BlockSpec((tm, n), lambda i: (i, 0)),
        out_shape=jax.ShapeDtypeStruct((TOKENS, n), F32),
        compiler_params=_cparams(("parallel",)),
        name="out_proj_residual",
    )(r, a1, a2, w, w)


def _ffn_kernel(h_ref, g_ref, wg_ref, wu_ref, wd_ref, fg_ref, o_ref, xn_ref, *, final_norm):
    f = pl.program_id(1)

    @pl.when(f == 0)
    def _():
        x = h_ref[...]
        xn_ref[...] = _rms(x, g_ref[...]).astype(BF16)
        o_ref[...] = x

    xn = xn_ref[...]
    a = jnp.dot(xn, wg_ref[...], preferred_element_type=F32)
    b = jnp.dot(xn, wu_ref[...], preferred_element_type=F32)
    act = (a * jax.nn.sigmoid(a) * b).astype(BF16)
    o_ref[...] += jnp.dot(act, wd_ref[...], preferred_element_type=F32)

    if final_norm:
        @pl.when(f == pl.num_programs(1) - 1)
        def _():
            o_ref[...] = _rms(o_ref[...], fg_ref[...])


def ffn_residual(h, g, wg, wu, wd, fg, *, final_norm, next_weights=None, tm=1024, tf=512):
    d = h.shape[1]
    hidden = wg.shape[1]
    n_rows, n_f = TOKENS // tm, hidden // tf
    casts = []
    if next_weights is not None:
        layer, ngate, nup, ndown = next_weights
        up_blk = (d // n_rows, tf)
        casts = [_CastJob(ngate, layer, up_blk, lambda i, f: (i, f)),
                 _CastJob(nup, layer, up_blk, lambda i, f: (i, f)),
                 _CastJob(ndown, layer, (tf, d // n_rows), lambda i, f: (f, i))]
    (out,), cast = _call_with_casts(
        functools.partial(_ffn_kernel, final_norm=final_norm), casts,
        (h, g.reshape(1, d), wg, wu, wd, fg.reshape(1, d)),
        grid=(n_rows, n_f),
        in_specs=[pl.BlockSpec((tm, d), lambda i, f: (i, 0)),
                  pl.BlockSpec((1, d), lambda i, f: (0, 0)),
                  pl.BlockSpec((d, tf), lambda i, f: (0, f)),
                  pl.BlockSpec((d, tf), lambda i, f: (0, f)),
                  pl.BlockSpec((tf, d), lambda i, f: (f, 0)),
                  pl.BlockSpec((1, d), lambda i, f: (0, 0))],
        out_specs=[pl.BlockSpec((tm, d), lambda i, f: (i, 0))],
        out_shape=[jax.ShapeDtypeStruct((TOKENS, d), F32)],
        scratch_shapes=[pltpu.VMEM((tm, d), BF16)],
        compiler_params=_cparams(("parallel", "arbitrary"), vmem_mb=BIG_TILE_VMEM_MB),
        name="ffn_residual")
    return out, cast


def _s5_kernel(u_ref, wb_ref, a_ref, wc_ref, d_ref, z_ref, ui_ref, bu_ref, xb_ref, st_ref,
               *, ts, nl):
    @pl.when(pl.program_id(1) == 0)
    def _():
        st_ref[...] = jnp.zeros_like(st_ref)

    blocks = range(nl)
    for l in blocks:
        cols = slice(l * LANE, (l + 1) * LANE)
        for b in range(BATCH):
            ui_ref[l, pl.ds(b, ts, stride=BATCH), :] = u_ref[b, :, cols].astype(F32)
        bu_ref[l] = jnp.dot(ui_ref[l].astype(BF16), wb_ref[l],
                            preferred_element_type=F32)
    coef = [(a_ref[l, 0:1, :], a_ref[l, 1:2, :]) for l in blocks]

    def two_steps(i, carry):
        r0 = pl.ds(pl.multiple_of(i * 2 * BATCH, 2 * BATCH), BATCH)
        r1 = pl.ds(pl.multiple_of(i * 2 * BATCH + BATCH, BATCH), BATCH)
        both = pl.ds(pl.multiple_of(i * 2 * BATCH, 2 * BATCH), 2 * BATCH)
        out = []
        for l in blocks:
            (xr, xi), (ar, ai) = carry[l], coef[l]
            xr1 = ar * xr - ai * xi + bu_ref[l, r0, :S5_SL]
            xi1 = ar * xi + ai * xr + bu_ref[l, r0, S5_SL:]
            xr2 = ar * xr1 - ai * xi1 + bu_ref[l, r1, :S5_SL]
            xi2 = ar * xi1 + ai * xr1 + bu_ref[l, r1, S5_SL:]
            xb_ref[l, both, :S5_SL] = jnp.concatenate([xr1, xr2], axis=0).astype(BF16)
            xb_ref[l, both, S5_SL:] = jnp.concatenate([xi1, xi2], axis=0).astype(BF16)
            out.append((xr2, xi2))
        return tuple(out)

    init = tuple((st_ref[l, :, :S5_SL], st_ref[l, :, S5_SL:]) for l in blocks)
    final = lax.fori_loop(0, ts // 2, two_steps, init, unroll=4)
    for l in blocks:
        st_ref[l, :, :S5_SL] = final[l][0]
        st_ref[l, :, S5_SL:] = final[l][1]
        y = (jnp.dot(xb_ref[l], wc_ref[l], preferred_element_type=F32)
             + d_ref[l] * ui_ref[l])
        ui_ref[l] = jax.nn.gelu(y, approximate=True)
        cols = slice(l * LANE, (l + 1) * LANE)
        for b in range(BATCH):
            z_ref[b, :, cols] = ui_ref[l, pl.ds(b, ts, stride=BATCH), :].astype(BF16)


def s5_scan(proj, ops, *, ts=256, nl=2):
    wb, a, wc, d = ops
    rows = ts * BATCH
    j3 = lambda j, t: (j, 0, 0)
    return pl.pallas_call(
        functools.partial(_s5_kernel, ts=ts, nl=nl),
        grid=(S5_LANE_BLOCKS // nl, SEQ // ts),
        in_specs=[pl.BlockSpec((BATCH, ts, nl * LANE), lambda j, t: (0, t, j)),
                  pl.BlockSpec((nl, LANE, 2 * S5_SL), j3),
                  pl.BlockSpec((nl, 2, S5_SL), j3),
                  pl.BlockSpec((nl, 2 * S5_SL, LANE), j3),
                  pl.BlockSpec((nl, 1, LANE), j3)],
        out_specs=pl.BlockSpec((BATCH, ts, nl * LANE), lambda j, t: (0, t, j)),
        out_shape=jax.ShapeDtypeStruct((BATCH, SEQ, S5_WIDTH), BF16),
        scratch_shapes=[pltpu.VMEM((nl, rows, LANE), F32),
                        pltpu.VMEM((nl, rows, 2 * S5_SL), F32),
                        pltpu.VMEM((nl, rows, 2 * S5_SL), BF16),
                        pltpu.VMEM((nl, BATCH, 2 * S5_SL), F32)],
        compiler_params=_cparams(("parallel", "arbitrary")),
        name="s5_scan",
    )(proj, wb, a, wc, d)


def _s5_operators(a_re, a_im, log_dt, b_re, b_im, c_re, c_im, d_skip):
    dt = jnp.exp(log_dt)[:, None]
    lr, li = a_re, a_im
    mag = jnp.exp(lr * dt)
    ar, ai = mag * jnp.cos(li * dt), mag * jnp.sin(li * dt)
    den = lr * lr + li * li
    zr, zi = ar - 1.0, ai
    fr = (zr * lr + zi * li) / den
    fi = (zi * lr - zr * li) / den
    bbr = fr[..., None] * b_re - fi[..., None] * b_im
    bbi = fr[..., None] * b_im + fi[..., None] * b_re
    nb, gb = S5_LANE_BLOCKS, S5_GB
    eye = jnp.eye(gb, dtype=F32)

    def in_map(bb):
        return jnp.einsum('jgnp,gh->jgphn', bb.reshape(nb, gb, S5_N, S5_P), eye).reshape(
            nb, LANE, S5_SL)

    def out_map(cc):
        return jnp.einsum('jgpn,gh->jhngp', cc.reshape(nb, gb, S5_P, S5_N), eye).reshape(
            nb, S5_SL, LANE)

    wb = jnp.concatenate([in_map(bbr), in_map(bbi)], axis=2).astype(BF16)
    wc = jnp.concatenate([out_map(c_re), out_map(-c_im)], axis=1).astype(BF16)
    a = jnp.stack([ar.reshape(nb, S5_SL), ai.reshape(nb, S5_SL)], axis=1)
    return wb, a, wc, d_skip.reshape(nb, 1, LANE)


def _glu_kernel(z_ref, w_ref, b_ref, o_ref):
    z = z_ref[...]
    gate = jnp.dot(z, w_ref[...], preferred_element_type=F32) + b_ref[...]
    o_ref[...] = (z.astype(F32) * jax.nn.sigmoid(gate)).astype(BF16)


def s5_glu(z, w, b, *, tm=1024):
    n = w.shape[1]
    return pl.pallas_call(
        _glu_kernel,
        grid=(TOKENS // tm,),
        in_specs=[pl.BlockSpec((tm, n), lambda i: (i, 0)),
                  pl.BlockSpec((n, n), lambda i: (0, 0)),
                  pl.BlockSpec((1, n), lambda i: (0, 0))],
        out_specs=pl.BlockSpec((tm, n), lambda i: (i, 0)),
        out_shape=jax.ShapeDtypeStruct((TOKENS, n), BF16),
        compiler_params=_cparams(("parallel",)),
        name="s5_glu",
    )(z, w, b.reshape(1, n))


def _diff_attn_kernel(scal_ref, q_ref, k_ref, v_ref, pq_ref, pk_ref, gn_ref, o_ref,
                      *, rc, hp, out_scale):
    lam = scal_ref[DIFF_HEADS]
    lane = lax.broadcasted_iota(jnp.int32, (rc, 2 * DIFF_DQK), 1)
    for c in range(SEQ // rc):
        lo = c * rc
        rows = slice(lo, lo + rc)
        pq = jnp.broadcast_to(pq_ref[rows, :], (rc, LANE))
        dist_d = jnp.abs(_lane_tile(pq, rc) - pk_ref[0, :, lo:lo + rc])
        dist_p = jnp.abs(_lane_tile(pq, lo) - pk_ref[0, :, :lo]) if c else None
        for h in range(hp):
            cols = slice(h * DIFF_DV, (h + 1) * DIFF_DV)
            slope = scal_ref[pl.program_id(1) * hp + h]
            q = q_ref[rows, cols].astype(F32) * (DIFF_DQK ** -0.5 * LOG2_E)
            qa = jnp.where(lane < DIFF_DQK, q, 0.0).astype(BF16)
            qb = jnp.where(lane >= DIFF_DQK, q, 0.0).astype(BF16)
            bias_d = slope * dist_d
            bias_p = slope * dist_p if c else None
            pva = _causal_rows(qa, k_ref, v_ref, c, rc, bias_d, bias_p, cols, cols)
            pvb = _causal_rows(qb, k_ref, v_ref, c, rc, bias_d, bias_p, cols, cols)
            o = (pva[:, :DIFF_DV] / pva[:, DIFF_DV:]
                 - lam * (pvb[:, :DIFF_DV] / pvb[:, DIFF_DV:]))
            o_ref[rows, cols] = (_rms(o, gn_ref[...]) * out_scale).astype(BF16)


def diff_attention(scal, proj, posq, posk, gn, *, out_scale, rc=512, hp=4):
    width = hp * DIFF_DV
    hb = lambda base: (lambda b, h, s: (b, base // width + h))
    return pl.pallas_call(
        functools.partial(_diff_attn_kernel, rc=rc, hp=hp, out_scale=out_scale),
        grid_spec=pltpu.PrefetchScalarGridSpec(
            num_scalar_prefetch=1,
            grid=(BATCH, DIFF_HEADS // hp),
            in_specs=[pl.BlockSpec((SEQ, width), hb(OD_Q)),
                      pl.BlockSpec((SEQ, width), hb(OD_K)),
                      pl.BlockSpec((SEQ, width), hb(OD_V)),
                      pl.BlockSpec((SEQ, 1), lambda b, h, s: (b, 0)),
                      pl.BlockSpec((1, 1, SEQ), lambda b, h, s: (b, 0, 0)),
                      pl.BlockSpec((1, DIFF_DV), lambda b, h, s: (0, 0))],
            out_specs=pl.BlockSpec((SEQ, width), lambda b, h, s: (b, h))),
        out_shape=jax.ShapeDtypeStruct((TOKENS, DIFF_HEADS * DIFF_DV), BF16),
        compiler_params=_cparams(("parallel", "parallel")),
        name="diff_attention",
    )(scal, proj, proj, proj, posq, posk, gn.reshape(1, DIFF_DV))


OD_Q = 1024
OD_K = 2048
OD_V = 3072


def _pack_even_w_in(w):
    cq, ckv, kr, gq, gk, gv, glr, gout = jnp.split(
        w, [512, 1024, 1088, 1600, 2112, 3136, 3152], axis=1)
    half = MLA_ROPE // 2
    kr_dup = jnp.concatenate([kr, kr[:, half:], kr[:, :half]], axis=1)
    glr_pad = jnp.pad(glr, ((0, 0), (0, EV_COLS - EV_GLR - GLA_GATE_RANK)))
    return jnp.concatenate([cq, ckv, gq, gk, gv, gout, kr_dup, glr_pad], axis=1).astype(BF16)


def _pack_w_uq(w):
    w = w.reshape(MLA_RANK, MLA_HEADS, MLA_NOPE + MLA_ROPE)
    half = MLA_ROPE // 2
    t1 = w[..., MLA_NOPE:MLA_NOPE + half]
    t2 = w[..., MLA_NOPE + half:]
    w = jnp.concatenate([w[..., :MLA_NOPE], t1, t2, t2, t1], axis=-1)
    return w.reshape(MLA_RANK, MLA_HEADS * MLA_QK).astype(BF16)


def _rope_tables(positions):
    half = MLA_ROPE // 2
    inv_freq = ROPE_THETA ** (-jnp.arange(half, dtype=F32) / half)
    zeros = jnp.zeros((2 * half,), F32)
    freq = jnp.concatenate([inv_freq, inv_freq, zeros])
    keep = jnp.concatenate([jnp.ones((2 * half,), F32), zeros])
    sign = jnp.concatenate([-jnp.ones((half,), F32), jnp.ones((half,), F32), zeros])
    ang = positions.astype(F32).reshape(TOKENS, 1) * freq
    return jnp.cos(ang) * keep, jnp.sin(ang) * sign


def _even_mixer(h, g_mix, positions, w_in, q_norm, w_uq, kv_norm, w_ukv, w_gate_up, b_gate,
                g_norm, w_out, ffn_weights, later_weights):
    proj, (w_out, *later) = norm_matmul(h, g_mix, _pack_even_w_in(w_in),
                                        later_weights=(w_out,) + tuple(later_weights))
    ct, st = _rope_tables(positions)
    q, k, v = mla_proj(proj, q_norm, kv_norm, _pack_w_uq(w_uq), w_ukv.astype(BF16), ct, st)
    o_mla, ffn_bf16 = mla_attention(q, k, v, ffn_weights)
    wg = jnp.pad(w_gate_up, ((0, LANE - GLA_GATE_RANK), (0, 0))).astype(BF16)
    o_gla = gla_mixer(proj.reshape(BATCH, SEQ, EV_COLS), wg,
                      b_gate.reshape(1, GLA_HEADS * GLA_DK), g_norm).reshape(TOKENS, -1)
    return out_proj_residual(h, o_mla, o_gla, w_out), ffn_bf16, later


def _odd_mixer(h, g_mix, positions, layer, w_in, a_re, a_im, log_dt, b_re, b_im, c_re, c_im,
               d_skip, w_glu, b_glu, lq1, lk1, lq2, lk2, d_norm, w_out):
    lambda_init = 0.8 - 0.6 * math.exp(-0.3 * layer)
    proj, _ = norm_matmul(h, g_mix, w_in)
    z = s5_scan(proj.reshape(BATCH, SEQ, -1),
                _s5_operators(a_re, a_im, log_dt, b_re, b_im, c_re, c_im, d_skip))
    o_s5 = s5_glu(z.reshape(TOKENS, S5_WIDTH), w_glu.astype(BF16), b_glu)
    lam = (jnp.exp(jnp.sum(lq1 * lk1)) - jnp.exp(jnp.sum(lq2 * lk2)) + lambda_init)
    slopes = jnp.exp2(-8.0 * jnp.arange(1, DIFF_HEADS + 1, dtype=F32) / DIFF_HEADS)
    scal = jnp.concatenate([slopes * LOG2_E, lam.reshape(1)]).astype(F32)
    posf = positions.astype(F32)
    o_diff = diff_attention(scal, proj, posf.reshape(TOKENS, 1), posf.reshape(BATCH, 1, SEQ),
                            d_norm, out_scale=1.0 - lambda_init)
    return out_proj_residual(h, o_s5, o_diff, w_out)


def kernel(x, positions, norm_mix, norm_ffn, final_norm, ffn_w_gate, ffn_w_up, ffn_w_down, ag_w_in, mla_q_norm, mla_w_uq, mla_kv_norm, mla_w_ukv, gla_w_gate_up, gla_b_gate, gla_norm, ag_w_out, cd_w_in, s5_a_re, s5_a_im, s5_log_dt, s5_b_re, s5_b_im, s5_c_re, s5_c_im, s5_d, s5_w_glu, s5_b_glu, diff_lambda_q1, diff_lambda_k1, diff_lambda_q2, diff_lambda_k2, diff_norm, cd_w_out):
    h = x.reshape(TOKENS, D_MODEL)
    ffn_f32 = lambda layer: (layer, ffn_w_gate, ffn_w_up, ffn_w_down)
    ffn_bf16 = odd_bf16 = None
    for layer in range(DEPTH):
        i = layer // 2
        if layer % 2 == 0:
            next_odd = ((cd_w_in, i), (cd_w_out, i)) if layer + 1 < DEPTH else ()
            h, ffn_bf16, odd_bf16 = _even_mixer(
                h, norm_mix[layer], positions, ag_w_in[i], mla_q_norm[i], mla_w_uq[i],
                mla_kv_norm[i], mla_w_ukv[i], gla_w_gate_up[i], gla_b_gate[i], gla_norm[i],
                (ag_w_out, i), ffn_f32(layer), next_odd)
        else:
            w_in, w_out = odd_bf16
            h = _odd_mixer(h, norm_mix[layer], positions, layer, w_in, s5_a_re[i],
                           s5_a_im[i], s5_log_dt[i], s5_b_re[i], s5_b_im[i], s5_c_re[i],
                           s5_c_im[i], s5_d[i], s5_w_glu[i], s5_b_glu[i], diff_lambda_q1[i],
                           diff_lambda_k1[i], diff_lambda_q2[i], diff_lambda_k2[i],
                           diff_norm[i], w_out)
        hosts_next = layer % 2 == 0 and layer + 1 < DEPTH
        h, next_bf16 = ffn_residual(h, norm_ffn[layer], *ffn_bf16, final_norm,
                                    final_norm=(layer == DEPTH - 1),
                                    next_weights=ffn_f32(layer + 1) if hosts_next else None)
        ffn_bf16 = next_bf16
    return h.reshape(BATCH, SEQ, D_MODEL)
```

```python
import functools
import math
from typing import Callable, NamedTuple

import jax
import jax.numpy as jnp
from jax import lax
from jax.experimental import pallas as pl
from jax.experimental.pallas import tpu as pltpu

F32 = jnp.float32
BF16 = jnp.bfloat16

D_MODEL = 2048
BATCH = 8
SEQ = 2048
DEPTH = 2
TOKENS = BATCH * SEQ
MIX_HALF = D_MODEL // 2
RMS_EPS = 1e-6

MLA_NOPE = 128
MLA_ROPE = 64
MLA_V = 128
MLA_HEADS = 8
MLA_RANK = 512
ROPE_THETA = 10000.0
MLA_QK = 256

GLA_HEADS = 4
GLA_DK = 128
GLA_DV = 256
GLA_GATE_RANK = 16
GLA_GATE_NORM = 16.0
GLA_CHUNK = 128

S5_WIDTH = MIX_HALF
S5_P = 16
S5_GROUPS = 64
S5_N = 64
S5_GB = 8
S5_LANE_BLOCKS = S5_GROUPS // S5_GB
S5_SL = S5_GB * S5_N

DIFF_DQK = 64
DIFF_DV = 128
DIFF_HEADS = 8

BIG_TILE_VMEM_MB = 58

LANE = 128
NEG_BIG = -1e30
LOG2_E = math.log2(math.e)

EV_CQ = 0
EV_CKV = 512
EV_GQ = 1024
EV_GK = 1536
EV_GV = 2048
EV_GOUT = 3072
EV_KROPE = 4096
EV_GLR = 4224
EV_COLS = 4608


def _cparams(sem, vmem_mb=48):
    return pltpu.CompilerParams(dimension_semantics=sem,
                                vmem_limit_bytes=vmem_mb * 1024 * 1024)


class _CastJob(NamedTuple):
    w: jax.Array
    layer: int
    block: tuple
    index: Callable


def _call_with_casts(body, casts, args, *, in_specs, out_specs, out_shape, **kwargs):
    n_in, n_out, n_cast = len(in_specs), len(out_specs), len(casts)

    def kernel(*refs):
        ins, rest = refs[:n_in], refs[n_in:]
        cast_in, rest = rest[:n_cast], rest[n_cast:]
        outs, rest = rest[:n_out], rest[n_out:]
        cast_out, scratch = rest[:n_cast], rest[n_cast:]
        for src, dst in zip(cast_in, cast_out):
            dst[...] = src[0].astype(BF16)
        body(*ins, *outs, *scratch)

    cast_in_specs = [pl.BlockSpec((1,) + c.block, lambda *g, c=c: (c.layer,) + tuple(c.index(*g)))
                     for c in casts]
    cast_out_specs = [pl.BlockSpec(c.block, lambda *g, c=c: tuple(c.index(*g))) for c in casts]
    cast_shapes = [jax.ShapeDtypeStruct(c.w.shape[1:], BF16) for c in casts]
    res = pl.pallas_call(kernel, in_specs=list(in_specs) + cast_in_specs,
                         out_specs=list(out_specs) + cast_out_specs,
                         out_shape=list(out_shape) + cast_shapes, **kwargs)(
                             *args, *[c.w for c in casts])
    return res[:n_out], res[n_out:]


def _rms(x, g):
    ms = jnp.mean(x * x, axis=-1, keepdims=True)
    return x * lax.rsqrt(ms + RMS_EPS) * g


def _nt_dot(a, b):
    return lax.dot_general(a, b, (((1,), (1,)), ((), ())), preferred_element_type=F32)


def _norm_matmul_kernel(x_ref, g_ref, w_ref, o_ref, xn_ref):
    @pl.when(pl.program_id(1) == 0)
    def _():
        xn_ref[...] = _rms(x_ref[...].astype(F32), g_ref[...]).astype(BF16)

    o_ref[...] = jnp.dot(xn_ref[...], w_ref[...],
                         preferred_element_type=F32).astype(o_ref.dtype)


def norm_matmul(x, g, w, *, later_weights=(), tm=1024, col_tiles=2):
    t, k = x.shape
    n = w.shape[1]
    tn = n // col_tiles
    steps = t // tm * col_tiles
    casts = [_CastJob(w3, layer, (w3.shape[1] // steps, w3.shape[2]),
                      lambda i, j: (i * col_tiles + j, 0)) for w3, layer in later_weights]
    (out,), cast = _call_with_casts(
        _norm_matmul_kernel, casts, (x, g.reshape(1, k), w),
        grid=(t // tm, col_tiles),
        in_specs=[pl.BlockSpec((tm, k), lambda i, j: (i, 0)),
                  pl.BlockSpec((1, k), lambda i, j: (0, 0)),
                  pl.BlockSpec((k, tn), lambda i, j: (0, j))],
        out_specs=[pl.BlockSpec((tm, tn), lambda i, j: (i, j))],
        out_shape=[jax.ShapeDtypeStruct((t, n), BF16)],
        scratch_shapes=[pltpu.VMEM((tm, k), BF16)],
        compiler_params=_cparams(("parallel", "arbitrary"), vmem_mb=BIG_TILE_VMEM_MB),
        name="norm_matmul")
    return out, cast


def _rope_half(blk, ct, st):
    return blk * ct + pltpu.roll(blk, 64, 1) * st


def _mla_proj_kernel(cq_ref, ckv_ref, kr_ref, gq_ref, gkv_ref, wq_ref, wkv_ref, ct_ref, st_ref,
                     q_ref, k_ref, v_ref, *, scale):
    ct, st = ct_ref[...], st_ref[...]
    xq = _rms(cq_ref[...].astype(F32), gq_ref[...]).astype(BF16)
    xkv = _rms(ckv_ref[...].astype(F32), gkv_ref[...]).astype(BF16)
    kr = _rope_half(kr_ref[...].astype(F32), ct, st).astype(BF16)
    for h in range(MLA_HEADS):
        lo, mid, hi = h * MLA_QK, h * MLA_QK + LANE, (h + 1) * MLA_QK
        a = jnp.dot(xq, wq_ref[:, lo:hi], preferred_element_type=F32)
        q_ref[:, lo:mid] = (a[:, :LANE] * scale).astype(BF16)
        q_ref[:, mid:hi] = (_rope_half(a[:, LANE:], ct, st) * scale).astype(BF16)
        kv = jnp.dot(xkv, wkv_ref[:, lo:hi], preferred_element_type=F32)
        k_ref[:, lo:mid] = kv[:, :LANE].astype(BF16)
        k_ref[:, mid:hi] = kr
        v_ref[:, h * MLA_V:(h + 1) * MLA_V] = kv[:, LANE:].astype(BF16)


def mla_proj(proj, gq, gkv, wq, wkv, ct, st, *, tm=1024):
    scale = (MLA_NOPE + MLA_ROPE) ** -0.5 * LOG2_E
    row = lambda i: (i, 0)
    fixed = lambda i: (0, 0)
    return pl.pallas_call(
        functools.partial(_mla_proj_kernel, scale=scale),
        grid=(TOKENS // tm,),
        in_specs=[pl.BlockSpec((tm, MLA_RANK), lambda i: (i, EV_CQ // MLA_RANK)),
                  pl.BlockSpec((tm, MLA_RANK), lambda i: (i, EV_CKV // MLA_RANK)),
                  pl.BlockSpec((tm, LANE), lambda i: (i, EV_KROPE // LANE)),
                  pl.BlockSpec((1, MLA_RANK), fixed),
                  pl.BlockSpec((1, MLA_RANK), fixed),
                  pl.BlockSpec((MLA_RANK, MLA_HEADS * MLA_QK), fixed),
                  pl.BlockSpec((MLA_RANK, MLA_HEADS * (MLA_NOPE + MLA_V)), fixed),
                  pl.BlockSpec((tm, LANE), row),
                  pl.BlockSpec((tm, LANE), row)],
        out_specs=[pl.BlockSpec((tm, MLA_HEADS * MLA_QK), row),
                   pl.BlockSpec((tm, MLA_HEADS * MLA_QK), row),
                   pl.BlockSpec((tm, MLA_HEADS * MLA_V), row)],
        out_shape=[jax.ShapeDtypeStruct((TOKENS, MLA_HEADS * MLA_QK), BF16),
                   jax.ShapeDtypeStruct((TOKENS, MLA_HEADS * MLA_QK), BF16),
                   jax.ShapeDtypeStruct((TOKENS, MLA_HEADS * MLA_V), BF16)],
        compiler_params=_cparams(("parallel",)),
        name="mla_proj",
    )(proj, proj, proj, gq.reshape(1, MLA_RANK), gkv.reshape(1, MLA_RANK), wq, wkv, ct, st)


def _causal_mask(tq, tk):
    row = lax.broadcasted_iota(jnp.int32, (tq, tk), 0)
    col = lax.broadcasted_iota(jnp.int32, (tq, tk), 1)
    return col <= row


def _lane_tile(x, width):
    return jnp.concatenate([x] * (width // LANE), axis=1)


def _with_ones(v):
    return jnp.concatenate([v, jnp.ones_like(v)], axis=1)


def _causal_rows(q, k_ref, v_ref, c, rc, bias_d=None, bias_p=None,
                 kcols=slice(None), vcols=slice(None)):
    lo = c * rc
    s_d = _nt_dot(q, k_ref[lo:lo + rc, kcols])
    if bias_d is not None:
        s_d = s_d - bias_d
    s_d = jnp.where(_causal_mask(rc, rc), s_d, NEG_BIG)
    m = jnp.max(s_d, axis=1, keepdims=True)
    if c:
        s_p = _nt_dot(q, k_ref[:lo, kcols])
        if bias_p is not None:
            s_p = s_p - bias_p
        m = jnp.maximum(m, jnp.max(s_p, axis=1, keepdims=True))
    m = jnp.broadcast_to(m, (rc, LANE))
    pv = jnp.dot(jnp.exp2(s_d - _lane_tile(m, rc)).astype(BF16),
                 _with_ones(v_ref[lo:lo + rc, vcols]), preferred_element_type=F32)
    if c:
        pv = pv + jnp.dot(jnp.exp2(s_p - _lane_tile(m, lo)).astype(BF16),
                          _with_ones(v_ref[:lo, vcols]), preferred_element_type=F32)
    return pv


def _mla_attn_kernel(q_ref, k_ref, v_ref, o_ref, *, rc, hp):
    for h in range(hp):
        kcols = slice(h * MLA_QK, (h + 1) * MLA_QK)
        vcols = slice(h * MLA_V, (h + 1) * MLA_V)
        for c in range(SEQ // rc):
            rows = slice(c * rc, (c + 1) * rc)
            pv = _causal_rows(q_ref[rows, kcols], k_ref, v_ref, c, rc, kcols=kcols, vcols=vcols)
            o_ref[rows, vcols] = (pv[:, :MLA_V] / pv[:, MLA_V:]).astype(BF16)


def mla_attention(q, k, v, ffn_weights, *, rc=512, hp=4):
    layer, gate, up, down = ffn_weights
    nh = MLA_HEADS // hp
    rows, split = 16, BATCH * nh // 16
    where = lambda b, h: ((b * nh + h) // split, (b * nh + h) % split)
    up_blk = (gate.shape[1] // rows, gate.shape[2] // split)
    down_blk = (down.shape[1] // rows, down.shape[2] // split)
    casts = [_CastJob(gate, layer, up_blk, where), _CastJob(up, layer, up_blk, where),
             _CastJob(down, layer, down_blk, where)]
    (out,), cast = _call_with_casts(
        functools.partial(_mla_attn_kernel, rc=rc, hp=hp), casts, (q, k, v),
        grid=(BATCH, nh),
        in_specs=[pl.BlockSpec((SEQ, hp * MLA_QK), lambda b, h: (b, h)),
                  pl.BlockSpec((SEQ, hp * MLA_QK), lambda b, h: (b, h)),
                  pl.BlockSpec((SEQ, hp * MLA_V), lambda b, h: (b, h))],
        out_specs=[pl.BlockSpec((SEQ, hp * MLA_V), lambda b, h: (b, h))],
        out_shape=[jax.ShapeDtypeStruct((TOKENS, MLA_HEADS * MLA_V), BF16)],
        compiler_params=_cparams(("parallel", "parallel"), vmem_mb=BIG_TILE_VMEM_MB),
        name="mla_attention")
    return out, cast


def _split3(x):
    hi = x.astype(BF16)
    r1 = x - hi.astype(F32)
    mid = r1.astype(BF16)
    lo = (r1 - mid.astype(F32)).astype(BF16)
    return hi, mid, lo


def _gla_kernel(q_ref, k_ref, v_ref, go_ref, glr_ref, wg_ref, bg_ref, gn_ref, o_ref, st_ref,
                *, chunk):
    @pl.when(pl.program_id(1) == 0)
    def _():
        st_ref[...] = jnp.zeros_like(st_ref)

    mask = _causal_mask(chunk, chunk)
    tri = mask.astype(BF16)
    for b in range(q_ref.shape[0]):
        z = jnp.dot(glr_ref[b], wg_ref[...], preferred_element_type=F32) + bg_ref[...]
        log_a = -(jnp.maximum(-z, 0.0) + jnp.log1p(jnp.exp(-jnp.abs(z)))) / GLA_GATE_NORM
        hi, mid, lo = _split3(log_a)
        cum_all = (jnp.dot(tri, hi, preferred_element_type=F32)
                   + jnp.dot(tri, mid, preferred_element_type=F32)
                   + jnp.dot(tri, lo, preferred_element_type=F32))

        for h in range(GLA_HEADS):
            ks = slice(h * GLA_DK, (h + 1) * GLA_DK)
            vs = slice(h * GLA_DV, (h + 1) * GLA_DV)
            cum = cum_all[:, ks]
            last = cum[chunk - 1:chunk, :]
            q = q_ref[b, :, ks].astype(F32) * GLA_DK ** -0.5
            k = k_ref[b, :, ks].astype(F32)
            v = v_ref[b, :, vs]
            st = st_ref[b, h]
            q_in = (q * jnp.exp(cum)).astype(BF16)
            q_hat = (q * jnp.exp(cum - last)).astype(BF16)
            k_hat = (k * jnp.exp(last - cum)).astype(BF16)

            o = _nt_dot(q_in, st.astype(BF16))
            attn = jnp.where(mask, _nt_dot(q_hat, k_hat), 0.0)
            o = o + jnp.dot(attn.astype(BF16), v, preferred_element_type=F32)
            vt = v.astype(F32).T.astype(BF16)
            st_ref[b, h] = st * jnp.exp(last) + jnp.dot(vt, k_hat, preferred_element_type=F32)

            o = _rms(o, gn_ref[...])
            g = go_ref[b, :, vs].astype(F32)
            o_ref[b, :, vs] = (o * (g * jax.nn.sigmoid(g))).astype(BF16)


def gla_mixer(proj, wg, bg, gn, *, chunk=GLA_CHUNK, nb=4):
    kw, vw = GLA_HEADS * GLA_DK, GLA_HEADS * GLA_DV
    col = lambda base, width: (lambda b, c: (b, c, base // width))
    fixed = lambda b, c: (0, 0)
    return pl.pallas_call(
        functools.partial(_gla_kernel, chunk=chunk),
        grid=(BATCH // nb, SEQ // chunk),
        in_specs=[pl.BlockSpec((nb, chunk, kw), col(EV_GQ, kw)),
                  pl.BlockSpec((nb, chunk, kw), col(EV_GK, kw)),
                  pl.BlockSpec((nb, chunk, vw), col(EV_GV, vw)),
                  pl.BlockSpec((nb, chunk, vw), col(EV_GOUT, vw)),
                  pl.BlockSpec((nb, chunk, LANE), col(EV_GLR, LANE)),
                  pl.BlockSpec((LANE, kw), fixed),
                  pl.BlockSpec((1, kw), fixed),
                  pl.BlockSpec((1, GLA_DV), fixed)],
        out_specs=pl.BlockSpec((nb, chunk, vw), lambda b, c: (b, c, 0)),
        out_shape=jax.ShapeDtypeStruct((BATCH, SEQ, vw), BF16),
        scratch_shapes=[pltpu.VMEM((nb, GLA_HEADS, GLA_DV, GLA_DK), F32)],
        compiler_params=_cparams(("parallel", "arbitrary")),
        name="gla_mixer",
    )(proj, proj, proj, proj, proj, wg, bg, gn.reshape(1, GLA_DV))


def _out_proj_kernel(r_ref, a1_ref, a2_ref, w1_ref, w2_ref, o_ref):
    o_ref[...] = (r_ref[...]
                  + jnp.dot(a1_ref[...], w1_ref[...], preferred_element_type=F32)
                  + jnp.dot(a2_ref[...], w2_ref[...], preferred_element_type=F32))


def out_proj_residual(r, a1, a2, w, *, tm=512):
    half = a1.shape[1]
    n = w.shape[1]
    return pl.pallas_call(
        _out_proj_kernel,
        grid=(TOKENS // tm,),
        in_specs=[pl.BlockSpec((tm, n), lambda i: (i, 0)),
                  pl.BlockSpec((tm, half), lambda i: (i, 0)),
                  pl.BlockSpec((tm, half), lambda i: (i, 0)),
                  pl.BlockSpec((half, n), lambda i: (0, 0)),
                  pl.BlockSpec((half, n), lambda i: (1, 0))],
        out_specs=pl.BlockSpec((tm, n), lambda i: (i, 0)),
        out_shape=jax.ShapeDtypeStruct((TOKENS, n), F32),
        compiler_params=_cparams(("parallel",)),
        name="out_proj_residual",
    )(r, a1, a2, w, w)


def _ffn_kernel(h_ref, g_ref, wg_ref, wu_ref, wd_ref, fg_ref, o_ref, xn_ref, *, final_norm):
    f = pl.program_id(1)

    @pl.when(f == 0)
    def _():
        x = h_ref[...]
        xn_ref[...] = _rms(x, g_ref[...]).astype(BF16)
        o_ref[...] = x

    xn = xn_ref[...]
    a = jnp.dot(xn, wg_ref[...], preferred_element_type=F32)
    b = jnp.dot(xn, wu_ref[...], preferred_element_type=F32)
    act = (a * jax.nn.sigmoid(a) * b).astype(BF16)
    o_ref[...] += jnp.dot(act, wd_ref[...], preferred_element_type=F32)

    if final_norm:
        @pl.when(f == pl.num_programs(1) - 1)
        def _():
            o_ref[...] = _rms(o_ref[...], fg_ref[...])


def ffn_residual(h, g, wg, wu, wd, fg, *, final_norm, next_weights=None, tm=1024, tf=512):
    d = h.shape[1]
    hidden = wg.shape[1]
    n_rows, n_f = TOKENS // tm, hidden // tf
    casts = []
    if next_weights is not None:
        layer, ngate, nup, ndown = next_weights
        up_blk = (d // n_rows, tf)
        casts = [_CastJob(ngate, layer, up_blk, lambda i, f: (i, f)),
                 _CastJob(nup, layer, up_blk, lambda i, f: (i, f)),
                 _CastJob(ndown, layer, (tf, d // n_rows), lambda i, f: (f, i))]
    (out,), cast = _call_with_casts(
        functools.partial(_ffn_kernel, final_norm=final_norm), casts,
        (h, g.reshape(1, d), wg, wu, wd, fg.reshape(1, d)),
        grid=(n_rows, n_f),
        in_specs=[pl.BlockSpec((tm, d), lambda i, f: (i, 0)),
                  pl.BlockSpec((1, d), lambda i, f: (0, 0)),
                  pl.BlockSpec((d, tf), lambda i, f: (0, f)),
                  pl.BlockSpec((d, tf), lambda i, f: (0, f)),
                  pl.BlockSpec((tf, d), lambda i, f: (f, 0)),
                  pl.BlockSpec((1, d), lambda i, f: (0, 0))],
        out_specs=[pl.BlockSpec((tm, d), lambda i, f: (i, 0))],
        out_shape=[jax.ShapeDtypeStruct((TOKENS, d), F32)],
        scratch_shapes=[pltpu.VMEM((tm, d), BF16)],
        compiler_params=_cparams(("parallel", "arbitrary"), vmem_mb=BIG_TILE_VMEM_MB),
        name="ffn_residual")
    return out, cast


def _s5_kernel(u_ref, wb_ref, a_ref, wc_ref, d_ref, z_ref, ui_ref, bu_ref, xb_ref, st_ref,
               *, ts, nl):
    @pl.when(pl.program_id(1) == 0)
    def _():
        st_ref[...] = jnp.zeros_like(st_ref)

    blocks = range(nl)
    for l in blocks:
        cols = slice(l * LANE, (l + 1) * LANE)
        for b in range(BATCH):
            ui_ref[l, pl.ds(b, ts, stride=BATCH), :] = u_ref[b, :, cols].astype(F32)
        bu_ref[l] = jnp.dot(ui_ref[l].astype(BF16), wb_ref[l],
                            preferred_element_type=F32)
    coef = [(a_ref[l, 0:1, :], a_ref[l, 1:2, :]) for l in blocks]

    def two_steps(i, carry):
        r0 = pl.ds(pl.multiple_of(i * 2 * BATCH, 2 * BATCH), BATCH)
        r1 = pl.ds(pl.multiple_of(i * 2 * BATCH + BATCH, BATCH), BATCH)
        both = pl.ds(pl.multiple_of(i * 2 * BATCH, 2 * BATCH), 2 * BATCH)
        out = []
        for l in blocks:
            (xr, xi), (ar, ai) = carry[l], coef[l]
            xr1 = ar * xr - ai * xi + bu_ref[l, r0, :S5_SL]
            xi1 = ar * xi + ai * xr + bu_ref[l, r0, S5_SL:]
            xr2 = ar * xr1 - ai * xi1 + bu_ref[l, r1, :S5_SL]
            xi2 = ar * xi1 + ai * xr1 + bu_ref[l, r1, S5_SL:]
            xb_ref[l, both, :S5_SL] = jnp.concatenate([xr1, xr2], axis=0).astype(BF16)
            xb_ref[l, both, S5_SL:] = jnp.concatenate([xi1, xi2], axis=0).astype(BF16)
            out.append((xr2, xi2))
        return tuple(out)

    init = tuple((st_ref[l, :, :S5_SL], st_ref[l, :, S5_SL:]) for l in blocks)
    final = lax.fori_loop(0, ts // 2, two_steps, init, unroll=4)
    for l in blocks:
        st_ref[l, :, :S5_SL] = final[l][0]
        st_ref[l, :, S5_SL:] = final[l][1]
        y = (jnp.dot(xb_ref[l], wc_ref[l], preferred_element_type=F32)
             + d_ref[l] * ui_ref[l])
        ui_ref[l] = jax.nn.gelu(y, approximate=True)
        cols = slice(l * LANE, (l + 1) * LANE)
        for b in range(BATCH):
            z_ref[b, :, cols] = ui_ref[l, pl.ds(b, ts, stride=BATCH), :].astype(BF16)


def s5_scan(proj, ops, *, ts=256, nl=2):
    wb, a, wc, d = ops
    rows = ts * BATCH
    j3 = lambda j, t: (j, 0, 0)
    return pl.pallas_call(
        functools.partial(_s5_kernel, ts=ts, nl=nl),
        grid=(S5_LANE_BLOCKS // nl, SEQ // ts),
        in_specs=[pl.BlockSpec((BATCH, ts, nl * LANE), lambda j, t: (0, t, j)),
                  pl.BlockSpec((nl, LANE, 2 * S5_SL), j3),
                  pl.BlockSpec((nl, 2, S5_SL), j3),
                  pl.BlockSpec((nl, 2 * S5_SL, LANE), j3),
                  pl.BlockSpec((nl, 1, LANE), j3)],
        out_specs=pl.BlockSpec((BATCH, ts, nl * LANE), lambda j, t: (0, t, j)),
        out_shape=jax.ShapeDtypeStruct((BATCH, SEQ, S5_WIDTH), BF16),
        scratch_shapes=[pltpu.VMEM((nl, rows, LANE), F32),
                        pltpu.VMEM((nl, rows, 2 * S5_SL), F32),
                        pltpu.VMEM((nl, rows, 2 * S5_SL), BF16),
                        pltpu.VMEM((nl, BATCH, 2 * S5_SL), F32)],
        compiler_params=_cparams(("parallel", "arbitrary")),
        name="s5_scan",
    )(proj, wb, a, wc, d)


def _s5_operators(a_re, a_im, log_dt, b_re, b_im, c_re, c_im, d_skip):
    dt = jnp.exp(log_dt)[:, None]
    lr, li = a_re, a_im
    mag = jnp.exp(lr * dt)
    ar, ai = mag * jnp.cos(li * dt), mag * jnp.sin(li * dt)
    den = lr * lr + li * li
    zr, zi = ar - 1.0, ai
    fr = (zr * lr + zi * li) / den
    fi = (zi * lr - zr * li) / den
    bbr = fr[..., None] * b_re - fi[..., None] * b_im
    bbi = fr[..., None] * b_im + fi[..., None] * b_re
    nb, gb = S5_LANE_BLOCKS, S5_GB
    eye = jnp.eye(gb, dtype=F32)

    def in_map(bb):
        return jnp.einsum('jgnp,gh->jgphn', bb.reshape(nb, gb, S5_N, S5_P), eye).reshape(
            nb, LANE, S5_SL)

    def out_map(cc):
        return jnp.einsum('jgpn,gh->jhngp', cc.reshape(nb, gb, S5_P, S5_N), eye).reshape(
            nb, S5_SL, LANE)

    wb = jnp.concatenate([in_map(bbr), in_map(bbi)], axis=2).astype(BF16)
    wc = jnp.concatenate([out_map(c_re), out_map(-c_im)], axis=1).astype(BF16)
    a = jnp.stack([ar.reshape(nb, S5_SL), ai.reshape(nb, S5_SL)], axis=1)
    return wb, a, wc, d_skip.reshape(nb, 1, LANE)


def _glu_kernel(z_ref, w_ref, b_ref, o_ref):
    z = z_ref[...]
    gate = jnp.dot(z, w_ref[...], preferred_element_type=F32) + b_ref[...]
    o_ref[...] = (z.astype(F32) * jax.nn.sigmoid(gate)).astype(BF16)


def s5_glu(z, w, b, *, tm=1024):
    n = w.shape[1]
    return pl.pallas_call(
        _glu_kernel,
        grid=(TOKENS // tm,),
        in_specs=[pl.BlockSpec((tm, n), lambda i: (i, 0)),
                  pl.BlockSpec((n, n), lambda i: (0, 0)),
                  pl.BlockSpec((1, n), lambda i: (0, 0))],
        out_specs=pl.BlockSpec((tm, n), lambda i: (i, 0)),
        out_shape=jax.ShapeDtypeStruct((TOKENS, n), BF16),
        compiler_params=_cparams(("parallel",)),
        name="s5_glu",
    )(z, w, b.reshape(1, n))


def _diff_attn_kernel(scal_ref, q_ref, k_ref, v_ref, pq_ref, pk_ref, gn_ref, o_ref,
                      *, rc, hp, out_scale):
    lam = scal_ref[DIFF_HEADS]
    lane = lax.broadcasted_iota(jnp.int32, (rc, 2 * DIFF_DQK), 1)
    for c in range(SEQ // rc):
        lo = c * rc
        rows = slice(lo, lo + rc)
        pq = jnp.broadcast_to(pq_ref[rows, :], (rc, LANE))
        dist_d = jnp.abs(_lane_tile(pq, rc) - pk_ref[0, :, lo:lo + rc])
        dist_p = jnp.abs(_lane_tile(pq, lo) - pk_ref[0, :, :lo]) if c else None
        for h in range(hp):
            cols = slice(h * DIFF_DV, (h + 1) * DIFF_DV)
            slope = scal_ref[pl.program_id(1) * hp + h]
            q = q_ref[rows, cols].astype(F32) * (DIFF_DQK ** -0.5 * LOG2_E)
            qa = jnp.where(lane < DIFF_DQK, q, 0.0).astype(BF16)
            qb = jnp.where(lane >= DIFF_DQK, q, 0.0).astype(BF16)
            bias_d = slope * dist_d
            bias_p = slope * dist_p if c else None
            pva = _causal_rows(qa, k_ref, v_ref, c, rc, bias_d, bias_p, cols, cols)
            pvb = _causal_rows(qb, k_ref, v_ref, c, rc, bias_d, bias_p, cols, cols)
            o = (pva[:, :DIFF_DV] / pva[:, DIFF_DV:]
                 - lam * (pvb[:, :DIFF_DV] / pvb[:, DIFF_DV:]))
            o_ref[rows, cols] = (_rms(o, gn_ref[...]) * out_scale).astype(BF16)


def diff_attention(scal, proj, posq, posk, gn, *, out_scale, rc=512, hp=4):
    width = hp * DIFF_DV
    hb = lambda base: (lambda b, h, s: (b, base // width + h))
    return pl.pallas_call(
        functools.partial(_diff_attn_kernel, rc=rc, hp=hp, out_scale=out_scale),
        grid_spec=pltpu.PrefetchScalarGridSpec(
            num_scalar_prefetch=1,
            grid=(BATCH, DIFF_HEADS // hp),
            in_specs=[pl.BlockSpec((SEQ, width), hb(OD_Q)),
                      pl.BlockSpec((SEQ, width), hb(OD_K)),
                      pl.BlockSpec((SEQ, width), hb(OD_V)),
                      pl.BlockSpec((SEQ, 1), lambda b, h, s: (b, 0)),
                      pl.BlockSpec((1, 1, SEQ), lambda b, h, s: (b, 0, 0)),
                      pl.BlockSpec((1, DIFF_DV), lambda b, h, s: (0, 0))],
            out_specs=pl.BlockSpec((SEQ, width), lambda b, h, s: (b, h))),
        out_shape=jax.ShapeDtypeStruct((TOKENS, DIFF_HEADS * DIFF_DV), BF16),
        compiler_params=_cparams(("parallel", "parallel")),
        name="diff_attention",
    )(scal, proj, proj, proj, posq, posk, gn.reshape(1, DIFF_DV))


OD_Q = 1024
OD_K = 2048
OD_V = 3072


def _pack_even_w_in(w):
    cq, ckv, kr, gq, gk, gv, glr, gout = jnp.split(
        w, [512, 1024, 1088, 1600, 2112, 3136, 3152], axis=1)
    half = MLA_ROPE // 2
    kr_dup = jnp.concatenate([kr, kr[:, half:], kr[:, :half]], axis=1)
    glr_pad = jnp.pad(glr, ((0, 0), (0, EV_COLS - EV_GLR - GLA_GATE_RANK)))
    return jnp.concatenate([cq, ckv, gq, gk, gv, gout, kr_dup, glr_pad], axis=1).astype(BF16)


def _pack_w_uq(w):
    w = w.reshape(MLA_RANK, MLA_HEADS, MLA_NOPE + MLA_ROPE)
    half = MLA_ROPE // 2
    t1 = w[..., MLA_NOPE:MLA_NOPE + half]
    t2 = w[..., MLA_NOPE + half:]
    w = jnp.concatenate([w[..., :MLA_NOPE], t1, t2, t2, t1], axis=-1)
    return w.reshape(MLA_RANK, MLA_HEADS * MLA_QK).astype(BF16)


def _rope_tables(positions):
    half = MLA_ROPE // 2
    inv_freq = ROPE_THETA ** (-jnp.arange(half, dtype=F32) / half)
    zeros = jnp.zeros((2 * half,), F32)
    freq = jnp.concatenate([inv_freq, inv_freq, zeros])
    keep = jnp.concatenate([jnp.ones((2 * half,), F32), zeros])
    sign = jnp.concatenate([-jnp.ones((half,), F32), jnp.ones((half,), F32), zeros])
    ang = positions.astype(F32).reshape(TOKENS, 1) * freq
    return jnp.cos(ang) * keep, jnp.sin(ang) * sign


def _even_mixer(h, g_mix, positions, w_in, q_norm, w_uq, kv_norm, w_ukv, w_gate_up, b_gate,
                g_norm, w_out, ffn_weights, later_weights):
    proj, (w_out, *later) = norm_matmul(h, g_mix, _pack_even_w_in(w_in),
                                        later_weights=(w_out,) + tuple(later_weights))
    ct, st = _rope_tables(positions)
    q, k, v = mla_proj(proj, q_norm, kv_norm, _pack_w_uq(w_uq), w_ukv.astype(BF16), ct, st)
    o_mla, ffn_bf16 = mla_attention(q, k, v, ffn_weights)
    wg = jnp.pad(w_gate_up, ((0, LANE - GLA_GATE_RANK), (0, 0))).astype(BF16)
    o_gla = gla_mixer(proj.reshape(BATCH, SEQ, EV_COLS), wg,
                      b_gate.reshape(1, GLA_HEADS * GLA_DK), g_norm).reshape(TOKENS, -1)
    return out_proj_residual(h, o_mla, o_gla, w_out), ffn_bf16, later


def _odd_mixer(h, g_mix, positions, layer, w_in, a_re, a_im, log_dt, b_re, b_im, c_re, c_im,
               d_skip, w_glu, b_glu, lq1, lk1, lq2, lk2, d_norm, w_out):
    lambda_init = 0.8 - 0.6 * math.exp(-0.3 * layer)
    proj, _ = norm_matmul(h, g_mix, w_in)
    z = s5_scan(proj.reshape(BATCH, SEQ, -1),
                _s5_operators(a_re, a_im, log_dt, b_re, b_im, c_re, c_im, d_skip))
    o_s5 = s5_glu(z.reshape(TOKENS, S5_WIDTH), w_glu.astype(BF16), b_glu)
    lam = (jnp.exp(jnp.sum(lq1 * lk1)) - jnp.exp(jnp.sum(lq2 * lk2)) + lambda_init)
    slopes = jnp.exp2(-8.0 * jnp.arange(1, DIFF_HEADS + 1, dtype=F32) / DIFF_HEADS)
    scal = jnp.concatenate([slopes * LOG2_E, lam.reshape(1)]).astype(F32)
    posf = positions.astype(F32)
    o_diff = diff_attention(scal, proj, posf.reshape(TOKENS, 1), posf.reshape(BATCH, 1, SEQ),
                            d_norm, out_scale=1.0 - lambda_init)
    return out_proj_residual(h, o_s5, o_diff, w_out)


def kernel(x, positions, norm_mix, norm_ffn, final_norm, ffn_w_gate, ffn_w_up, ffn_w_down, ag_w_in, mla_q_norm, mla_w_uq, mla_kv_norm, mla_w_ukv, gla_w_gate_up, gla_b_gate, gla_norm, ag_w_out, cd_w_in, s5_a_re, s5_a_im, s5_log_dt, s5_b_re, s5_b_im, s5_c_re, s5_c_im, s5_d, s5_w_glu, s5_b_glu, diff_lambda_q1, diff_lambda_k1, diff_lambda_q2, diff_lambda_k2, diff_norm, cd_w_out):
    h = x.reshape(TOKENS, D_MODEL)
    ffn_f32 = lambda layer: (layer, ffn_w_gate, ffn_w_up, ffn_w_down)
    ffn_bf16 = odd_bf16 = None
    for layer in range(DEPTH):
        i = layer // 2
        if layer % 2 == 0:
            next_odd = ((cd_w_in, i), (cd_w_out, i)) if layer + 1 < DEPTH else ()
            h, ffn_bf16, odd_bf16 = _even_mixer(
                h, norm_mix[layer], positions, ag_w_in[i], mla_q_norm[i], mla_w_uq[i],
                mla_kv_norm[i], mla_w_ukv[i], gla_w_gate_up[i], gla_b_gate[i], gla_norm[i],
                (ag_w_out, i), ffn_f32(layer), next_odd)
        else:
            w_in, w_out = odd_bf16
            h = _odd_mixer(h, norm_mix[layer], positions, layer, w_in, s5_a_re[i],
                           s5_a_im[i], s5_log_dt[i], s5_b_re[i], s5_b_im[i], s5_c_re[i],
                           s5_c_im[i], s5_d[i], s5_w_glu[i], s5_b_glu[i], diff_lambda_q1[i],
                           diff_lambda_k1[i], diff_lambda_q2[i], diff_lambda_k2[i],
                           diff_norm[i], w_out)
        hosts_next = layer % 2 == 0 and layer + 1 < DEPTH
        h, next_bf16 = ffn_residual(h, norm_ffn[layer], *ffn_bf16, final_norm,
                                    final_norm=(layer == DEPTH - 1),
                                    next_weights=ffn_f32(layer + 1) if hosts_next else None)
        ffn_bf16 = next_bf16
    return h.reshape(BATCH, SEQ, D_MODEL)
```

```python
import functools
import math
from typing import Callable, NamedTuple

import jax
import jax.numpy as jnp
from jax import lax
from jax.experimental import pallas as pl
from jax.experimental.pallas import tpu as pltpu

F32 = jnp.float32
BF16 = jnp.bfloat16

D_MODEL = 2048
BATCH = 8
SEQ = 2048
DEPTH = 2
TOKENS = BATCH * SEQ
MIX_HALF = D_MODEL // 2
RMS_EPS = 1e-6

MLA_NOPE = 128
MLA_ROPE = 64
MLA_V = 128
MLA_HEADS = 8
MLA_RANK = 512
ROPE_THETA = 10000.0
MLA_QK = 256

GLA_HEADS = 4
GLA_DK = 128
GLA_DV = 256
GLA_GATE_RANK = 16
GLA_GATE_NORM = 16.0
GLA_CHUNK = 128

S5_WIDTH = MIX_HALF
S5_P = 16
S5_GROUPS = 64
S5_N = 64
S5_GB = 8
S5_LANE_BLOCKS = S5_GROUPS // S5_GB
S5_SL = S5_GB * S5_N

DIFF_DQK = 64
DIFF_DV = 128
DIFF_HEADS = 8

BIG_TILE_VMEM_MB = 62

LANE = 128
NEG_BIG = -1e30
LOG2_E = math.log2(math.e)

EV_CQ = 0
EV_CKV = 512
EV_GQ = 1024
EV_GK = 1536
EV_GV = 2048
EV_GOUT = 3072
EV_KROPE = 4096
EV_GLR = 4224
EV_COLS = 4608


def _cparams(sem, vmem_mb=48):
    return pltpu.CompilerParams(dimension_semantics=sem,
                                vmem_limit_bytes=vmem_mb * 1024 * 1024)


class _CastJob(NamedTuple):
    w: jax.Array
    layer: int
    block: tuple
    index: Callable


def _call_with_casts(body, casts, args, *, in_specs, out_specs, out_shape, **kwargs):
    n_in, n_out, n_cast = len(in_specs), len(out_specs), len(casts)

    def kernel(*refs):
        ins, rest = refs[:n_in], refs[n_in:]
        cast_in, rest = rest[:n_cast], rest[n_cast:]
        outs, rest = rest[:n_out], rest[n_out:]
        cast_out, scratch = rest[:n_cast], rest[n_cast:]
        for src, dst in zip(cast_in, cast_out):
            dst[...] = src[0].astype(BF16)
        body(*ins, *outs, *scratch)

    cast_in_specs = [pl.BlockSpec((1,) + c.block, lambda *g, c=c: (c.layer,) + tuple(c.index(*g)))
                     for c in casts]
    cast_out_specs = [pl.BlockSpec(c.block, lambda *g, c=c: tuple(c.index(*g))) for c in casts]
    cast_shapes = [jax.ShapeDtypeStruct(c.w.shape[1:], BF16) for c in casts]
    res = pl.pallas_call(kernel, in_specs=list(in_specs) + cast_in_specs,
                         out_specs=list(out_specs) + cast_out_specs,
                         out_shape=list(out_shape) + cast_shapes, **kwargs)(
                             *args, *[c.w for c in casts])
    return res[:n_out], res[n_out:]


def _rms(x, g):
    ms = jnp.mean(x * x, axis=-1, keepdims=True)
    return x * lax.rsqrt(ms + RMS_EPS) * g


def _nt_dot(a, b):
    return lax.dot_general(a, b, (((1,), (1,)), ((), ())), preferred_element_type=F32)


def _norm_matmul_kernel(x_ref, g_ref, w_ref, o_ref, xn_ref):
    @pl.when(pl.program_id(1) == 0)
    def _():
        xn_ref[...] = _rms(x_ref[...].astype(F32), g_ref[...]).astype(BF16)

    o_ref[...] = jnp.dot(xn_ref[...], w_ref[...],
                         preferred_element_type=F32).astype(o_ref.dtype)


def norm_matmul(x, g, w, *, later_weights=(), tm=1024, col_tiles=2):
    t, k = x.shape
    n = w.shape[1]
    tn = n // col_tiles
    steps = t // tm * col_tiles
    casts = [_CastJob(w3, layer, (w3.shape[1] // steps, w3.shape[2]),
                      lambda i, j: (i * col_tiles + j, 0)) for w3, layer in later_weights]
    (out,), cast = _call_with_casts(
        _norm_matmul_kernel, casts, (x, g.reshape(1, k), w),
        grid=(t // tm, col_tiles),
        in_specs=[pl.BlockSpec((tm, k), lambda i, j: (i, 0)),
                  pl.BlockSpec((1, k), lambda i, j: (0, 0)),
                  pl.BlockSpec((k, tn), lambda i, j: (0, j))],
        out_specs=[pl.BlockSpec((tm, tn), lambda i, j: (i, j))],
        out_shape=[jax.ShapeDtypeStruct((t, n), BF16)],
        scratch_shapes=[pltpu.VMEM((tm, k), BF16)],
        compiler_params=_cparams(("parallel", "arbitrary"), vmem_mb=BIG_TILE_VMEM_MB),
        name="norm_matmul")
    return out, cast


def _rope_half(blk, ct, st):
    return blk * ct + pltpu.roll(blk, 64, 1) * st


def _mla_proj_kernel(cq_ref, ckv_ref, kr_ref, gq_ref, gkv_ref, wq_ref, wkv_ref, ct_ref, st_ref,
                     q_ref, k_ref, v_ref, *, scale):
    ct, st = ct_ref[...], st_ref[...]
    xq = _rms(cq_ref[...].astype(F32), gq_ref[...]).astype(BF16)
    xkv = _rms(ckv_ref[...].astype(F32), gkv_ref[...]).astype(BF16)
    kr = _rope_half(kr_ref[...].astype(F32), ct, st).astype(BF16)
    for h in range(MLA_HEADS):
        lo, mid, hi = h * MLA_QK, h * MLA_QK + LANE, (h + 1) * MLA_QK
        a = jnp.dot(xq, wq_ref[:, lo:hi], preferred_element_type=F32)
        q_ref[:, lo:mid] = (a[:, :LANE] * scale).astype(BF16)
        q_ref[:, mid:hi] = (_rope_half(a[:, LANE:], ct, st) * scale).astype(BF16)
        kv = jnp.dot(xkv, wkv_ref[:, lo:hi], preferred_element_type=F32)
        k_ref[:, lo:mid] = kv[:, :LANE].astype(BF16)
        k_ref[:, mid:hi] = kr
        v_ref[:, h * MLA_V:(h + 1) * MLA_V] = kv[:, LANE:].astype(BF16)


def mla_proj(proj, gq, gkv, wq, wkv, ct, st, *, tm=1024):
    scale = (MLA_NOPE + MLA_ROPE) ** -0.5 * LOG2_E
    row = lambda i: (i, 0)
    fixed = lambda i: (0, 0)
    return pl.pallas_call(
        functools.partial(_mla_proj_kernel, scale=scale),
        grid=(TOKENS // tm,),
        in_specs=[pl.BlockSpec((tm, MLA_RANK), lambda i: (i, EV_CQ // MLA_RANK)),
                  pl.BlockSpec((tm, MLA_RANK), lambda i: (i, EV_CKV // MLA_RANK)),
                  pl.BlockSpec((tm, LANE), lambda i: (i, EV_KROPE // LANE)),
                  pl.BlockSpec((1, MLA_RANK), fixed),
                  pl.BlockSpec((1, MLA_RANK), fixed),
                  pl.BlockSpec((MLA_RANK, MLA_HEADS * MLA_QK), fixed),
                  pl.BlockSpec((MLA_RANK, MLA_HEADS * (MLA_NOPE + MLA_V)), fixed),
                  pl.BlockSpec((tm, LANE), row),
                  pl.BlockSpec((tm, LANE), row)],
        out_specs=[pl.BlockSpec((tm, MLA_HEADS * MLA_QK), row),
                   pl.BlockSpec((tm, MLA_HEADS * MLA_QK), row),
                   pl.BlockSpec((tm, MLA_HEADS * MLA_V), row)],
        out_shape=[jax.ShapeDtypeStruct((TOKENS, MLA_HEADS * MLA_QK), BF16),
                   jax.ShapeDtypeStruct((TOKENS, MLA_HEADS * MLA_QK), BF16),
                   jax.ShapeDtypeStruct((TOKENS, MLA_HEADS * MLA_V), BF16)],
        compiler_params=_cparams(("parallel",)),
        name="mla_proj",
    )(proj, proj, proj, gq.reshape(1, MLA_RANK), gkv.reshape(1, MLA_RANK), wq, wkv, ct, st)


def _causal_mask(tq, tk):
    row = lax.broadcasted_iota(jnp.int32, (tq, tk), 0)
    col = lax.broadcasted_iota(jnp.int32, (tq, tk), 1)
    return col <= row


def _lane_tile(x, width):
    return jnp.concatenate([x] * (width // LANE), axis=1)


def _with_ones(v):
    return jnp.concatenate([v, jnp.ones_like(v)], axis=1)


def _causal_rows(q, k_ref, v_ref, c, rc, bias_d=None, bias_p=None,
                 kcols=slice(None), vcols=slice(None)):
    lo = c * rc
    s_d = _nt_dot(q, k_ref[lo:lo + rc, kcols])
    if bias_d is not None:
        s_d = s_d - bias_d
    s_d = jnp.where(_causal_mask(rc, rc), s_d, NEG_BIG)
    m = jnp.max(s_d, axis=1, keepdims=True)
    if c:
        s_p = _nt_dot(q, k_ref[:lo, kcols])
        if bias_p is not None:
            s_p = s_p - bias_p
        m = jnp.maximum(m, jnp.max(s_p, axis=1, keepdims=True))
    m = jnp.broadcast_to(m, (rc, LANE))
    pv = jnp.dot(jnp.exp2(s_d - _lane_tile(m, rc)).astype(BF16),
                 _with_ones(v_ref[lo:lo + rc, vcols]), preferred_element_type=F32)
    if c:
        pv = pv + jnp.dot(jnp.exp2(s_p - _lane_tile(m, lo)).astype(BF16),
                          _with_ones(v_ref[:lo, vcols]), preferred_element_type=F32)
    return pv


def _mla_attn_kernel(q_ref, k_ref, v_ref, o_ref, *, rc, hp):
    for h in range(hp):
        kcols = slice(h * MLA_QK, (h + 1) * MLA_QK)
        vcols = slice(h * MLA_V, (h + 1) * MLA_V)
        for c in range(SEQ // rc):
            rows = slice(c * rc, (c + 1) * rc)
            pv = _causal_rows(q_ref[rows, kcols], k_ref, v_ref, c, rc, kcols=kcols, vcols=vcols)
            o_ref[rows, vcols] = (pv[:, :MLA_V] / pv[:, MLA_V:]).astype(BF16)


def mla_attention(q, k, v, ffn_weights, *, rc=512, hp=4):
    layer, gate, up, down = ffn_weights
    nh = MLA_HEADS // hp
    rows, split = 16, BATCH * nh // 16
    where = lambda b, h: ((b * nh + h) // split, (b * nh + h) % split)
    up_blk = (gate.shape[1] // rows, gate.shape[2] // split)
    down_blk = (down.shape[1] // rows, down.shape[2] // split)
    casts = [_CastJob(gate, layer, up_blk, where), _CastJob(up, layer, up_blk, where),
             _CastJob(down, layer, down_blk, where)]
    (out,), cast = _call_with_casts(
        functools.partial(_mla_attn_kernel, rc=rc, hp=hp), casts, (q, k, v),
        grid=(BATCH, nh),
        in_specs=[pl.BlockSpec((SEQ, hp * MLA_QK), lambda b, h: (b, h)),
                  pl.BlockSpec((SEQ, hp * MLA_QK), lambda b, h: (b, h)),
                  pl.BlockSpec((SEQ, hp * MLA_V), lambda b, h: (b, h))],
        out_specs=[pl.BlockSpec((SEQ, hp * MLA_V), lambda b, h: (b, h))],
        out_shape=[jax.ShapeDtypeStruct((TOKENS, MLA_HEADS * MLA_V), BF16)],
        compiler_params=_cparams(("parallel", "parallel"), vmem_mb=BIG_TILE_VMEM_MB),
        name="mla_attention")
    return out, cast


def _split3(x):
    hi = x.astype(BF16)
    r1 = x - hi.astype(F32)
    mid = r1.astype(BF16)
    lo = (r1 - mid.astype(F32)).astype(BF16)
    return hi, mid, lo


def _gla_kernel(q_ref, k_ref, v_ref, go_ref, glr_ref, wg_ref, bg_ref, gn_ref, o_ref, st_ref,
                *, chunk):
    @pl.when(pl.program_id(1) == 0)
    def _():
        st_ref[...] = jnp.zeros_like(st_ref)

    mask = _causal_mask(chunk, chunk)
    tri = mask.astype(BF16)
    for b in range(q_ref.shape[0]):
        z = jnp.dot(glr_ref[b], wg_ref[...], preferred_element_type=F32) + bg_ref[...]
        log_a = -(jnp.maximum(-z, 0.0) + jnp.log1p(jnp.exp(-jnp.abs(z)))) / GLA_GATE_NORM
        hi, mid, lo = _split3(log_a)
        cum_all = (jnp.dot(tri, hi, preferred_element_type=F32)
                   + jnp.dot(tri, mid, preferred_element_type=F32)
                   + jnp.dot(tri, lo, preferred_element_type=F32))

        for h in range(GLA_HEADS):
            ks = slice(h * GLA_DK, (h + 1) * GLA_DK)
            vs = slice(h * GLA_DV, (h + 1) * GLA_DV)
            cum = cum_all[:, ks]
            last = cum[chunk - 1:chunk, :]
            q = q_ref[b, :, ks].astype(F32) * GLA_DK ** -0.5
            k = k_ref[b, :, ks].astype(F32)
            v = v_ref[b, :, vs]
            st = st_ref[b, h]
            q_in = (q * jnp.exp(cum)).astype(BF16)
            q_hat = (q * jnp.exp(cum - last)).astype(BF16)
            k_hat = (k * jnp.exp(last - cum)).astype(BF16)

            o = _nt_dot(q_in, st.astype(BF16))
            attn = jnp.where(mask, _nt_dot(q_hat, k_hat), 0.0)
            o = o + jnp.dot(attn.astype(BF16), v, preferred_element_type=F32)
            vt = v.astype(F32).T.astype(BF16)
            st_ref[b, h] = st * jnp.exp(last) + jnp.dot(vt, k_hat, preferred_element_type=F32)

            o = _rms(o, gn_ref[...])
            g = go_ref[b, :, vs].astype(F32)
            o_ref[b, :, vs] = (o * (g * jax.nn.sigmoid(g))).astype(BF16)


def gla_mixer(proj, wg, bg, gn, *, chunk=GLA_CHUNK, nb=4):
    kw, vw = GLA_HEADS * GLA_DK, GLA_HEADS * GLA_DV
    col = lambda base, width: (lambda b, c: (b, c, base // width))
    fixed = lambda b, c: (0, 0)
    return pl.pallas_call(
        functools.partial(_gla_kernel, chunk=chunk),
        grid=(BATCH // nb, SEQ // chunk),
        in_specs=[pl.BlockSpec((nb, chunk, kw), col(EV_GQ, kw)),
                  pl.BlockSpec((nb, chunk, kw), col(EV_GK, kw)),
                  pl.BlockSpec((nb, chunk, vw), col(EV_GV, vw)),
                  pl.BlockSpec((nb, chunk, vw), col(EV_GOUT, vw)),
                  pl.BlockSpec((nb, chunk, LANE), col(EV_GLR, LANE)),
                  pl.BlockSpec((LANE, kw), fixed),
                  pl.BlockSpec((1, kw), fixed),
                  pl.BlockSpec((1, GLA_DV), fixed)],
        out_specs=pl.BlockSpec((nb, chunk, vw), lambda b, c: (b, c, 0)),
        out_shape=jax.ShapeDtypeStruct((BATCH, SEQ, vw), BF16),
        scratch_shapes=[pltpu.VMEM((nb, GLA_HEADS, GLA_DV, GLA_DK), F32)],
        compiler_params=_cparams(("parallel", "arbitrary")),
        name="gla_mixer",
    )(proj, proj, proj, proj, proj, wg, bg, gn.reshape(1, GLA_DV))


def _out_proj_kernel(r_ref, a1_ref, a2_ref, w1_ref, w2_ref, g_ref, o_ref, hn_ref):
    o = (r_ref[...]
         + jnp.dot(a1_ref[...], w1_ref[...], preferred_element_type=F32)
         + jnp.dot(a2_ref[...], w2_ref[...], preferred_element_type=F32))
    o_ref[...] = o
    hn_ref[...] = _rms(o, g_ref[...]).astype(BF16)


def out_proj_residual(r, a1, a2, w, g, *, tm=512):
    half = a1.shape[1]
    n = w.shape[1]
    return pl.pallas_call(
        _out_proj_kernel,
        grid=(TOKENS // tm,),
        in_specs=[pl.BlockSpec((tm, n), lambda i: (i, 0)),
                  pl.BlockSpec((tm, half), lambda i: (i, 0)),
                  pl.BlockSpec((tm, half), lambda i: (i, 0)),
                  pl.BlockSpec((half, n), lambda i: (0, 0)),
                  pl.BlockSpec((half, n), lambda i: (1, 0)),
                  pl.BlockSpec((1, n), lambda i: (0, 0))],
        out_specs=[pl.BlockSpec((tm, n), lambda i: (i, 0)),
                   pl.BlockSpec((tm, n), lambda i: (i, 0))],
        out_shape=[jax.ShapeDtypeStruct((TOKENS, n), F32),
                   jax.ShapeDtypeStruct((TOKENS, n), BF16)],
        compiler_params=_cparams(("parallel",)),
        name="out_proj_residual",
    )(r, a1, a2, w, w, g.reshape(1, n))


def _ffn_kernel(h_ref, xn_ref, wg_ref, wu_ref, wd_ref, fg_ref, o_ref, *, final_norm):
    f = pl.program_id(1)

    @pl.when(f == 0)
    def _():
        o_ref[...] = h_ref[...]

    xn = xn_ref[...]
    a = jnp.dot(xn, wg_ref[...], preferred_element_type=F32)
    b = jnp.dot(xn, wu_ref[...], preferred_element_type=F32)
    act = (a * jax.nn.sigmoid(a) * b).astype(BF16)
    o_ref[...] += jnp.dot(act, wd_ref[...], preferred_element_type=F32)

    if final_norm:
        @pl.when(f == pl.num_programs(1) - 1)
        def _():
            o_ref[...] = _rms(o_ref[...], fg_ref[...])


def ffn_residual(h, hn, wg, wu, wd, fg, *, final_norm, next_weights=None, tm=1024, tf=512):
    d = h.shape[1]
    hidden = wg.shape[1]
    n_rows, n_f = TOKENS // tm, hidden // tf
    casts = []
    if next_weights is not None:
        layer, ngate, nup, ndown = next_weights
        up_blk = (d // n_rows, tf)
        casts = [_CastJob(ngate, layer, up_blk, lambda i, f: (i, f)),
                 _CastJob(nup, layer, up_blk, lambda i, f: (i, f)),
                 _CastJob(ndown, layer, (tf, d // n_rows), lambda i, f: (f, i))]
    (out,), cast = _call_with_casts(
        functools.partial(_ffn_kernel, final_norm=final_norm), casts,
        (h, hn, wg, wu, wd, fg.reshape(1, d)),
        grid=(n_rows, n_f),
        in_specs=[pl.BlockSpec((tm, d), lambda i, f: (i, 0)),
                  pl.BlockSpec((tm, d), lambda i, f: (i, 0)),
                  pl.BlockSpec((d, tf), lambda i, f: (0, f)),
                  pl.BlockSpec((d, tf), lambda i, f: (0, f)),
                  pl.BlockSpec((tf, d), lambda i, f: (f, 0)),
                  pl.BlockSpec((1, d), lambda i, f: (0, 0))],
        out_specs=[pl.BlockSpec((tm, d), lambda i, f: (i, 0))],
        out_shape=[jax.ShapeDtypeStruct((TOKENS, d), F32)],
        compiler_params=_cparams(("parallel", "arbitrary"), vmem_mb=BIG_TILE_VMEM_MB),
        name="ffn_residual")
    return out, cast


def _s5_kernel(u_ref, wb_ref, a_ref, wc_ref, d_ref, z_ref, ui_ref, bu_ref, xb_ref, st_ref,
               *, ts, nl):
    @pl.when(pl.program_id(1) == 0)
    def _():
        st_ref[...] = jnp.zeros_like(st_ref)

    blocks = range(nl)
    for l in blocks:
        cols = slice(l * LANE, (l + 1) * LANE)
        for b in range(BATCH):
            ui_ref[l, pl.ds(b, ts, stride=BATCH), :] = u_ref[b, :, cols].astype(F32)
        bu_ref[l] = jnp.dot(ui_ref[l].astype(BF16), wb_ref[l],
                            preferred_element_type=F32)
    coef = [(a_ref[l, 0:1, :], a_ref[l, 1:2, :]) for l in blocks]

    def two_steps(i, carry):
        r0 = pl.ds(pl.multiple_of(i * 2 * BATCH, 2 * BATCH), BATCH)
        r1 = pl.ds(pl.multiple_of(i * 2 * BATCH + BATCH, BATCH), BATCH)
        both = pl.ds(pl.multiple_of(i * 2 * BATCH, 2 * BATCH), 2 * BATCH)
        out = []
        for l in blocks:
            (xr, xi), (ar, ai) = carry[l], coef[l]
            xr1 = ar * xr - ai * xi + bu_ref[l, r0, :S5_SL]
            xi1 = ar * xi + ai * xr + bu_ref[l, r0, S5_SL:]
            xr2 = ar * xr1 - ai * xi1 + bu_ref[l, r1, :S5_SL]
            xi2 = ar * xi1 + ai * xr1 + bu_ref[l, r1, S5_SL:]
            xb_ref[l, both, :S5_SL] = jnp.concatenate([xr1, xr2], axis=0).astype(BF16)
            xb_ref[l, both, S5_SL:] = jnp.concatenate([xi1, xi2], axis=0).astype(BF16)
            out.append((xr2, xi2))
        return tuple(out)

    init = tuple((st_ref[l, :, :S5_SL], st_ref[l, :, S5_SL:]) for l in blocks)
    final = lax.fori_loop(0, ts // 2, two_steps, init, unroll=4)
    for l in blocks:
        st_ref[l, :, :S5_SL] = final[l][0]
        st_ref[l, :, S5_SL:] = final[l][1]
        y = (jnp.dot(xb_ref[l], wc_ref[l], preferred_element_type=F32)
             + d_ref[l] * ui_ref[l])
        ui_ref[l] = jax.nn.gelu(y, approximate=True)
        cols = slice(l * LANE, (l + 1) * LANE)
        for b in range(BATCH):
            z_ref[b, :, cols] = ui_ref[l, pl.ds(b, ts, stride=BATCH), :].astype(BF16)


def s5_scan(proj, ops, *, ts=256, nl=2):
    wb, a, wc, d = ops
    rows = ts * BATCH
    j3 = lambda j, t: (j, 0, 0)
    return pl.pallas_call(
        functools.partial(_s5_kernel, ts=ts, nl=nl),
        grid=(S5_LANE_BLOCKS // nl, SEQ // ts),
        in_specs=[pl.BlockSpec((BATCH, ts, nl * LANE), lambda j, t: (0, t, j)),
                  pl.BlockSpec((nl, LANE, 2 * S5_SL), j3),
                  pl.BlockSpec((nl, 2, S5_SL), j3),
                  pl.BlockSpec((nl, 2 * S5_SL, LANE), j3),
                  pl.BlockSpec((nl, 1, LANE), j3)],
        out_specs=pl.BlockSpec((BATCH, ts, nl * LANE), lambda j, t: (0, t, j)),
        out_shape=jax.ShapeDtypeStruct((BATCH, SEQ, S5_WIDTH), BF16),
        scratch_shapes=[pltpu.VMEM((nl, rows, LANE), F32),
                        pltpu.VMEM((nl, rows, 2 * S5_SL), F32),
                        pltpu.VMEM((nl, rows, 2 * S5_SL), BF16),
                        pltpu.VMEM((nl, BATCH, 2 * S5_SL), F32)],
        compiler_params=_cparams(("parallel", "arbitrary")),
        name="s5_scan",
    )(proj, wb, a, wc, d)


def _s5_operators(a_re, a_im, log_dt, b_re, b_im, c_re, c_im, d_skip):
    dt = jnp.exp(log_dt)[:, None]
    lr, li = a_re, a_im
    mag = jnp.exp(lr * dt)
    ar, ai = mag * jnp.cos(li * dt), mag * jnp.sin(li * dt)
    den = lr * lr + li * li
    zr, zi = ar - 1.0, ai
    fr = (zr * lr + zi * li) / den
    fi = (zi * lr - zr * li) / den
    bbr = fr[..., None] * b_re - fi[..., None] * b_im
    bbi = fr[..., None] * b_im + fi[..., None] * b_re
    nb, gb = S5_LANE_BLOCKS, S5_GB
    eye = jnp.eye(gb, dtype=F32)

    def in_map(bb):
        return jnp.einsum('jgnp,gh->jgphn', bb.reshape(nb, gb, S5_N, S5_P), eye).reshape(
            nb, LANE, S5_SL)

    def out_map(cc):
        return jnp.einsum('jgpn,gh->jhngp', cc.reshape(nb, gb, S5_P, S5_N), eye).reshape(
            nb, S5_SL, LANE)

    wb = jnp.concatenate([in_map(bbr), in_map(bbi)], axis=2).astype(BF16)
    wc = jnp.concatenate([out_map(c_re), out_map(-c_im)], axis=1).astype(BF16)
    a = jnp.stack([ar.reshape(nb, S5_SL), ai.reshape(nb, S5_SL)], axis=1)
    return wb, a, wc, d_skip.reshape(nb, 1, LANE)


def _glu_kernel(z_ref, w_ref, b_ref, o_ref):
    z = z_ref[...]
    gate = jnp.dot(z, w_ref[...], preferred_element_type=F32) + b_ref[...]
    o_ref[...] = (z.astype(F32) * jax.nn.sigmoid(gate)).astype(BF16)


def s5_glu(z, w, b, *, tm=1024):
    n = w.shape[1]
    return pl.pallas_call(
        _glu_kernel,
        grid=(TOKENS // tm,),
        in_specs=[pl.BlockSpec((tm, n), lambda i: (i, 0)),
                  pl.BlockSpec((n, n), lambda i: (0, 0)),
                  pl.BlockSpec((1, n), lambda i: (0, 0))],
        out_specs=pl.BlockSpec((tm, n), lambda i: (i, 0)),
        out_shape=jax.ShapeDtypeStruct((TOKENS, n), BF16),
        compiler_params=_cparams(("parallel",)),
        name="s5_glu",
    )(z, w, b.reshape(1, n))


def _diff_attn_kernel(scal_ref, q_ref, k_ref, v_ref, pq_ref, pk_ref, gn_ref, o_ref,
                      *, rc, hp, out_scale):
    lam = scal_ref[DIFF_HEADS]
    lane = lax.broadcasted_iota(jnp.int32, (rc, 2 * DIFF_DQK), 1)
    for c in range(SEQ // rc):
        lo = c * rc
        rows = slice(lo, lo + rc)
        pq = jnp.broadcast_to(pq_ref[rows, :], (rc, LANE))
        dist_d = jnp.abs(_lane_tile(pq, rc) - pk_ref[0, :, lo:lo + rc])
        dist_p = jnp.abs(_lane_tile(pq, lo) - pk_ref[0, :, :lo]) if c else None
        for h in range(hp):
            cols = slice(h * DIFF_DV, (h + 1) * DIFF_DV)
            slope = scal_ref[pl.program_id(1) * hp + h]
            q = q_ref[rows, cols].astype(F32) * (DIFF_DQK ** -0.5 * LOG2_E)
            qa = jnp.where(lane < DIFF_DQK, q, 0.0).astype(BF16)
            qb = jnp.where(lane >= DIFF_DQK, q, 0.0).astype(BF16)
            bias_d = slope * dist_d
            bias_p = slope * dist_p if c else None
            pva = _causal_rows(qa, k_ref, v_ref, c, rc, bias_d, bias_p, cols, cols)
            pvb = _causal_rows(qb, k_ref, v_ref, c, rc, bias_d, bias_p, cols, cols)
            o = (pva[:, :DIFF_DV] / pva[:, DIFF_DV:]
                 - lam * (pvb[:, :DIFF_DV] / pvb[:, DIFF_DV:]))
            o_ref[rows, cols] = (_rms(o, gn_ref[...]) * out_scale).astype(BF16)


def diff_attention(scal, proj, posq, posk, gn, *, out_scale, rc=512, hp=4):
    width = hp * DIFF_DV
    hb = lambda base: (lambda b, h, s: (b, base // width + h))
    return pl.pallas_call(
        functools.partial(_diff_attn_kernel, rc=rc, hp=hp, out_scale=out_scale),
        grid_spec=pltpu.PrefetchScalarGridSpec(
            num_scalar_prefetch=1,
            grid=(BATCH, DIFF_HEADS // hp),
            in_specs=[pl.BlockSpec((SEQ, width), hb(OD_Q)),
                      pl.BlockSpec((SEQ, width), hb(OD_K)),
                      pl.BlockSpec((SEQ, width), hb(OD_V)),
                      pl.BlockSpec((SEQ, 1), lambda b, h, s: (b, 0)),
                      pl.BlockSpec((1, 1, SEQ), lambda b, h, s: (b, 0, 0)),
                      pl.BlockSpec((1, DIFF_DV), lambda b, h, s: (0, 0))],
            out_specs=pl.BlockSpec((SEQ, width), lambda b, h, s: (b, h))),
        out_shape=jax.ShapeDtypeStruct((TOKENS, DIFF_HEADS * DIFF_DV), BF16),
        compiler_params=_cparams(("parallel", "parallel")),
        name="diff_attention",
    )(scal, proj, proj, proj, posq, posk, gn.reshape(1, DIFF_DV))


OD_Q = 1024
OD_K = 2048
OD_V = 3072


def _pack_even_w_in(w):
    cq, ckv, kr, gq, gk, gv, glr, gout = jnp.split(
        w, [512, 1024, 1088, 1600, 2112, 3136, 3152], axis=1)
    half = MLA_ROPE // 2
    kr_dup = jnp.concatenate([kr, kr[:, half:], kr[:, :half]], axis=1)
    glr_pad = jnp.pad(glr, ((0, 0), (0, EV_COLS - EV_GLR - GLA_GATE_RANK)))
    return jnp.concatenate([cq, ckv, gq, gk, gv, gout, kr_dup, glr_pad], axis=1).astype(BF16)


def _pack_w_uq(w):
    w = w.reshape(MLA_RANK, MLA_HEADS, MLA_NOPE + MLA_ROPE)
    half = MLA_ROPE // 2
    t1 = w[..., MLA_NOPE:MLA_NOPE + half]
    t2 = w[..., MLA_NOPE + half:]
    w = jnp.concatenate([w[..., :MLA_NOPE], t1, t2, t2, t1], axis=-1)
    return w.reshape(MLA_RANK, MLA_HEADS * MLA_QK).astype(BF16)


def _rope_tables(positions):
    half = MLA_ROPE // 2
    inv_freq = ROPE_THETA ** (-jnp.arange(half, dtype=F32) / half)
    zeros = jnp.zeros((2 * half,), F32)
    freq = jnp.concatenate([inv_freq, inv_freq, zeros])
    keep = jnp.concatenate([jnp.ones((2 * half,), F32), zeros])
    sign = jnp.concatenate([-jnp.ones((half,), F32), jnp.ones((half,), F32), zeros])
    ang = positions.astype(F32).reshape(TOKENS, 1) * freq
    return jnp.cos(ang) * keep, jnp.sin(ang) * sign


def _even_mixer(h, g_mix, positions, w_in, q_norm, w_uq, kv_norm, w_ukv, w_gate_up, b_gate,
                g_norm, w_out, ffn_weights, later_weights, g_ffn):
    proj, (w_out, *later) = norm_matmul(h, g_mix, _pack_even_w_in(w_in),
                                        later_weights=(w_out,) + tuple(later_weights))
    ct, st = _rope_tables(positions)
    q, k, v = mla_proj(proj, q_norm, kv_norm, _pack_w_uq(w_uq), w_ukv.astype(BF16), ct, st)
    o_mla, ffn_bf16 = mla_attention(q, k, v, ffn_weights)
    wg = jnp.pad(w_gate_up, ((0, LANE - GLA_GATE_RANK), (0, 0))).astype(BF16)
    o_gla = gla_mixer(proj.reshape(BATCH, SEQ, EV_COLS), wg,
                      b_gate.reshape(1, GLA_HEADS * GLA_DK), g_norm).reshape(TOKENS, -1)
    return out_proj_residual(h, o_mla, o_gla, w_out, g_ffn), ffn_bf16, later


def _odd_mixer(h, g_mix, positions, layer, w_in, a_re, a_im, log_dt, b_re, b_im, c_re, c_im,
               d_skip, w_glu, b_glu, lq1, lk1, lq2, lk2, d_norm, w_out, g_ffn):
    lambda_init = 0.8 - 0.6 * math.exp(-0.3 * layer)
    proj, _ = norm_matmul(h, g_mix, w_in)
    z = s5_scan(proj.reshape(BATCH, SEQ, -1),
                _s5_operators(a_re, a_im, log_dt, b_re, b_im, c_re, c_im, d_skip))
    o_s5 = s5_glu(z.reshape(TOKENS, S5_WIDTH), w_glu.astype(BF16), b_glu)
    lam = (jnp.exp(jnp.sum(lq1 * lk1)) - jnp.exp(jnp.sum(lq2 * lk2)) + lambda_init)
    slopes = jnp.exp2(-8.0 * jnp.arange(1, DIFF_HEADS + 1, dtype=F32) / DIFF_HEADS)
    scal = jnp.concatenate([slopes * LOG2_E, lam.reshape(1)]).astype(F32)
    posf = positions.astype(F32)
    o_diff = diff_attention(scal, proj, posf.reshape(TOKENS, 1), posf.reshape(BATCH, 1, SEQ),
                            d_norm, out_scale=1.0 - lambda_init)
    return out_proj_residual(h, o_s5, o_diff, w_out, g_ffn)


def kernel(x, positions, norm_mix, norm_ffn, final_norm, ffn_w_gate, ffn_w_up, ffn_w_down, ag_w_in, mla_q_norm, mla_w_uq, mla_kv_norm, mla_w_ukv, gla_w_gate_up, gla_b_gate, gla_norm, ag_w_out, cd_w_in, s5_a_re, s5_a_im, s5_log_dt, s5_b_re, s5_b_im, s5_c_re, s5_c_im, s5_d, s5_w_glu, s5_b_glu, diff_lambda_q1, diff_lambda_k1, diff_lambda_q2, diff_lambda_k2, diff_norm, cd_w_out):
    h = x.reshape(TOKENS, D_MODEL)
    ffn_f32 = lambda layer: (layer, ffn_w_gate, ffn_w_up, ffn_w_down)
    ffn_bf16 = odd_bf16 = None
    for layer in range(DEPTH):
        i = layer // 2
        if layer % 2 == 0:
            next_odd = ((cd_w_in, i), (cd_w_out, i)) if layer + 1 < DEPTH else ()
            (h, hn), ffn_bf16, odd_bf16 = _even_mixer(
                h, norm_mix[layer], positions, ag_w_in[i], mla_q_norm[i], mla_w_uq[i],
                mla_kv_norm[i], mla_w_ukv[i], gla_w_gate_up[i], gla_b_gate[i], gla_norm[i],
                (ag_w_out, i), ffn_f32(layer), next_odd, norm_ffn[layer])
        else:
            w_in, w_out = odd_bf16
            h, hn = _odd_mixer(h, norm_mix[layer], positions, layer, w_in, s5_a_re[i],
                               s5_a_im[i], s5_log_dt[i], s5_b_re[i], s5_b_im[i], s5_c_re[i],
                               s5_c_im[i], s5_d[i], s5_w_glu[i], s5_b_glu[i], diff_lambda_q1[i],
                               diff_lambda_k1[i], diff_lambda_q2[i], diff_lambda_k2[i],
                               diff_norm[i], w_out, norm_ffn[layer])
        hosts_next = layer % 2 == 0 and layer + 1 < DEPTH
        h, next_bf16 = ffn_residual(h, hn, *ffn_bf16, final_norm,
                                    final_norm=(layer == DEPTH - 1),
                                    next_weights=ffn_f32(layer + 1) if hosts_next else None)
        ffn_bf16 = next_bf16
    return h.reshape(BATCH, SEQ, D_MODEL)
```
